```python
import math
import jax
import jax.numpy as jnp
from jax import lax
import numpy as np

D_MODEL = 2048
BATCH = 4
SEQ = 2048
DEPTH = 2

GRID_W = 64
CTX_LEN = 256
EPS = 1e-6
ROPE_THETA = 10000.0
Q_BLOCK = 128
MLA_HEADS = 4
MLA_Q_RANK = 512
MLA_KV_RANK = 512
MLA_NOPE = 128
MLA_ROPE = 64
MLA_V = 128
GQA_HEADS = 8
GQA_KV_HEADS = 2
GQA_HD = 128
GDN_HEADS = 4
GDN_DK = 128
GDN_DV = 128
GDN_CONV = 5
GDN_CHUNK = 64
N_BRANCH = 3
D_FF = 4 * D_MODEL
N_MOD = 6
IN_SPLITS = (MLA_Q_RANK, MLA_KV_RANK, MLA_ROPE,
             GQA_HEADS * GQA_HD, GQA_KV_HEADS * GQA_HD, GQA_KV_HEADS * GQA_HD,
             GDN_HEADS * (2 * GDN_DK + GDN_DV), 2 * GDN_HEADS, 2 * GDN_HEADS, GDN_HEADS * GDN_DV,
             N_BRANCH * D_MODEL)
D_IN = sum(IN_SPLITS)

kernel_name = 'hybrid_mla_gqa_gdn_prefix_dit_block'


def rmsnorm(x, g):
    xf = x.astype(jnp.float32)
    y = xf * lax.rsqrt(jnp.mean(xf * xf, axis=-1, keepdims=True) + EPS)
    return (y * g.astype(jnp.float32)).astype(x.dtype)


def l2norm(x):
    xf = x.astype(jnp.float32)
    return (xf * lax.rsqrt(jnp.sum(xf * xf, axis=-1, keepdims=True) + EPS)).astype(x.dtype)


def split_cols(p):
    out, off = [], 0
    for n in IN_SPLITS:
        out.append(p[..., off:off + n])
        off += n
    return out


def rope_1d(x, pos):
    half = x.shape[-1] // 2
    inv_freq = ROPE_THETA ** (-jnp.arange(half, dtype=jnp.float32) / half)
    ang = pos.astype(jnp.float32)[:, None] * inv_freq[None, :]
    cos = jnp.cos(ang)[:, None, :]
    sin = jnp.sin(ang)[:, None, :]
    xf = x.astype(jnp.float32)
    x1, x2 = xf[..., :half], xf[..., half:]
    return jnp.concatenate([x1 * cos - x2 * sin, x2 * cos + x1 * sin], axis=-1).astype(x.dtype)


def rope_2d(x, row, col):
    n = x.shape[-1] // 2
    return jnp.concatenate([rope_1d(x[..., :n], row), rope_1d(x[..., n:], col)], axis=-1)


def blocked_attention(q, k, v, scale):
    b, t, hq, d = q.shape
    hk = k.shape[2]
    grp = hq // hk
    nb = t // Q_BLOCK
    qb = q.reshape(b, nb, Q_BLOCK, hk, grp, d).transpose(1, 0, 2, 3, 4, 5)

    def one_block(qblk):
        s = jnp.einsum('bqhgd,bshd->bhgqs', qblk, k).astype(jnp.float32) * scale
        p = jax.nn.softmax(s, axis=-1).astype(v.dtype)
        return jnp.einsum('bhgqs,bshe->bqhge', p, v)

    o = lax.map(one_block, qb)
    return o.transpose(1, 0, 2, 3, 4, 5).reshape(b, t, hq, v.shape[-1])


def centred_dwconv(x, w):
    pad = GDN_CONV // 2
    return lax.conv_general_dilated(x, w[:, None, :].astype(x.dtype), window_strides=(1,),
                                    padding=((pad, pad),), dimension_numbers=('NWC', 'WIO', 'NWC'),
                                    feature_group_count=x.shape[-1])


def gated_delta_chunked(q, k, v, g, beta, s0):
    f32 = jnp.float32
    b, t, h, dk = q.shape
    cs = GDN_CHUNK
    n = t // cs

    def chunks(a):
        a = a.astype(f32).reshape((b, n, cs, h) + a.shape[3:])
        return jnp.moveaxis(a, (1, 3), (0, 2))

    qc = chunks(q) * (dk ** -0.5)
    kc, vc, bc = chunks(k), chunks(v), chunks(beta)
    gc = jnp.cumsum(chunks(g), axis=-1)
    lower = jnp.tril(jnp.ones((cs, cs), bool))
    strict = jnp.tril(jnp.ones((cs, cs), bool), -1)
    diff = gc[..., :, None] - gc[..., None, :]
    decay = jnp.where(lower, jnp.exp(jnp.where(lower, diff, 0.0)), 0.0)
    kb = kc * bc[..., None]
    a = jnp.where(strict, jnp.einsum('nbhid,nbhjd->nbhij', kb, kc) * decay, 0.0)
    eye = jnp.broadcast_to(jnp.eye(cs, dtype=f32), a.shape)
    tinv = lax.linalg.triangular_solve(eye + a, eye, left_side=True, lower=True, unit_diagonal=True)
    u = jnp.einsum('nbhij,nbhje->nbhie', tinv, vc * bc[..., None])
    w = jnp.einsum('nbhij,nbhjd->nbhid', tinv, kb * jnp.exp(gc)[..., None])
    qk = jnp.einsum('nbhid,nbhjd->nbhij', qc, kc) * decay

    def step(s, inp):
        q_i, k_i, u_i, w_i, g_i, qk_i = inp
        v_new = u_i - jnp.einsum('bhcd,bhde->bhce', w_i, s)
        o = (jnp.einsum('bhcd,bhde->bhce', q_i * jnp.exp(g_i)[..., None], s)
             + jnp.einsum('bhij,bhje->bhie', qk_i, v_new))
        g_last = g_i[..., -1:]
        s = (s * jnp.exp(g_last)[..., None]
             + jnp.einsum('bhcd,bhce->bhde', k_i * jnp.exp(g_last - g_i)[..., None], v_new))
        return s, o

    s_fin, o = lax.scan(step, s0.astype(f32), (qc, kc, u, w, gc, qk))
    o = jnp.moveaxis(o, (0, 2), (1, 3)).reshape(b, t, h, -1)
    return o, s_fin


def gdn_mixer(qkv_raw, beta_raw, a_raw, w_conv, a_log, dt_bias, s0_f, s0_b):
    b, t, _ = qkv_raw.shape
    qkv = jax.nn.silu(centred_dwconv(qkv_raw, w_conv))
    q, k, v = jnp.split(qkv, [GDN_HEADS * GDN_DK, 2 * GDN_HEADS * GDN_DK], axis=-1)
    q = l2norm(q.reshape(b, t, GDN_HEADS, GDN_DK))
    k = l2norm(k.reshape(b, t, GDN_HEADS, GDN_DK))
    v = v.reshape(b, t, GDN_HEADS, GDN_DV)
    beta = jax.nn.sigmoid(beta_raw.astype(jnp.float32)).reshape(b, t, 2, GDN_HEADS)
    g = -jnp.exp(a_log.astype(jnp.float32)) * jax.nn.softplus(
        a_raw.astype(jnp.float32).reshape(b, t, 2, GDN_HEADS) + dt_bias.astype(jnp.float32))
    o_f, s_f = gated_delta_chunked(q, k, v, g[:, :, 0], beta[:, :, 0], s0_f)
    rev = lambda arr: jnp.flip(arr, axis=1)
    o_b, s_b = gated_delta_chunked(rev(q), rev(k), rev(v), rev(g[:, :, 1]), rev(beta[:, :, 1]), s0_b)
    return o_f + rev(o_b), s_f, s_b


def gdn_output(o, gate_raw, g_out, dtype):
    b, t = o.shape[:2]
    gate = jax.nn.silu(gate_raw.astype(jnp.float32).reshape(b, t, GDN_HEADS, GDN_DV))
    return (rmsnorm(o, g_out) * gate).reshape(b, t, GDN_HEADS * GDN_DV).astype(dtype)


def project_mixers(h, w_in, g_mla_q, w_mla_qb, g_mla_kv, w_mla_kvb, g_gqa_q, g_gqa_k, pos):
    b, t, _ = h.shape
    (cq, ckv, kpe, gq, gk, gv, qkv, beta, dec, ogate, bgate) = split_cols(h @ w_in)
    qa = (rmsnorm(cq, g_mla_q) @ w_mla_qb).reshape(b, t, MLA_HEADS, MLA_NOPE + MLA_ROPE)
    kva = (rmsnorm(ckv, g_mla_kv) @ w_mla_kvb).reshape(b, t, MLA_HEADS, MLA_NOPE + MLA_V)
    q_nope, q_pe = qa[..., :MLA_NOPE], qa[..., MLA_NOPE:]
    k_nope, va = kva[..., :MLA_NOPE], kva[..., MLA_NOPE:]
    k_pe = kpe.reshape(b, t, 1, MLA_ROPE)
    qb = rmsnorm(gq.reshape(b, t, GQA_HEADS, GQA_HD), g_gqa_q)
    kb = rmsnorm(gk.reshape(b, t, GQA_KV_HEADS, GQA_HD), g_gqa_k)
    vb = gv.reshape(b, t, GQA_KV_HEADS, GQA_HD)
    if pos is not None:
        row, col = pos
        q_pe, k_pe = rope_2d(q_pe, row, col), rope_2d(k_pe, row, col)
        qb, kb = rope_2d(qb, row, col), rope_2d(kb, row, col)
    qa = jnp.concatenate([q_nope, q_pe], axis=-1)
    ka = jnp.concatenate([k_nope, jnp.broadcast_to(k_pe, (b, t, MLA_HEADS, MLA_ROPE))], axis=-1)
    return (qa, ka, va, qb, kb, vb, qkv, beta, dec, ogate, bgate)


def merge_branches(o_a, o_b, o_c, bgate, w_up_a, w_up_b, w_up_c, w_out):
    ga, gb, gc = jnp.split(jax.nn.sigmoid(bgate), N_BRANCH, axis=-1)
    y = ga * (o_a @ w_up_a) + gb * (o_b @ w_up_b) + gc * (o_c @ w_up_c)
    return y @ w_out


def sqrelu_mlp(h, w1, w2):
    return jnp.square(jax.nn.relu(h @ w1)) @ w2


def setup_inputs(seed: int = 0) -> dict:
    key = jax.random.key(seed)
    ks = jax.random.split(key, 26)
    f32 = jnp.float32
    L, D = DEPTH, D_MODEL

    def dense(k, fan_in, shape, gain=1.0):
        return jax.random.normal(k, shape, f32) * (gain * fan_in ** -0.5)

    def norm_gain(k, shape):
        return 1.0 + 0.05 * jax.random.normal(k, shape, f32)

    dt = jnp.exp(jax.random.uniform(ks[16], (L, 2, GDN_HEADS), f32, math.log(1e-3), math.log(1e-1)))
    return {
        'x': jax.random.normal(ks[0], (BATCH, SEQ, D), f32),
        'c': jax.random.normal(ks[1], (BATCH, D), f32),
        'ctx': jax.random.normal(ks[2], (BATCH, CTX_LEN, D), f32),
        'c_ctx': jax.random.normal(ks[3], (D,), f32),
        'w_mod': dense(ks[4], D, (L, D, N_MOD * D), 0.5),
        'b_mod': 0.02 * jax.random.normal(ks[5], (L, N_MOD * D), f32),
        'g_norm1': norm_gain(ks[6], (L, D)),
        'w_in': dense(ks[7], D, (L, D, D_IN)),
        'g_mla_q': norm_gain(ks[8], (L, MLA_Q_RANK)),
        'w_mla_qb': dense(ks[9], MLA_Q_RANK, (L, MLA_Q_RANK, MLA_HEADS * (MLA_NOPE + MLA_ROPE))),
        'g_mla_kv': norm_gain(ks[10], (L, MLA_KV_RANK)),
        'w_mla_kvb': dense(ks[11], MLA_KV_RANK, (L, MLA_KV_RANK, MLA_HEADS * (MLA_NOPE + MLA_V))),
        'g_gqa_q': norm_gain(ks[12], (L, GQA_HD)),
        'g_gqa_k': norm_gain(ks[13], (L, GQA_HD)),
        'w_conv': dense(ks[14], GDN_CONV, (L, GDN_CONV, GDN_HEADS * (2 * GDN_DK + GDN_DV))),
        'a_log': jnp.log(jax.random.uniform(ks[15], (L, 2, GDN_HEADS), f32, 1.0, 16.0)),
        'dt_bias': dt + jnp.log(-jnp.expm1(-dt)),
        'g_gdn_out': norm_gain(ks[17], (L, GDN_DV)),
        'w_up_a': dense(ks[18], MLA_HEADS * MLA_V, (L, MLA_HEADS * MLA_V, D)),
        'w_up_b': dense(ks[19], GQA_HEADS * GQA_HD, (L, GQA_HEADS * GQA_HD, D)),
        'w_up_c': dense(ks[20], GDN_HEADS * GDN_DV, (L, GDN_HEADS * GDN_DV, D)),
        'w_out': dense(ks[21], D, (L, D, D)),
        'g_norm2': norm_gain(ks[22], (L, D)),
        'w_ff1': dense(ks[23], D, (L, D, D_FF)),
        'w_ff2': dense(ks[24], D_FF, (L, D_FF, D), 0.5),
        'g_final': norm_gain(ks[25], (D,)),
    }


def reference(x, c, ctx, c_ctx, w_mod, b_mod, g_norm1, w_in, g_mla_q, w_mla_qb, g_mla_kv, w_mla_kvb,
              g_gqa_q, g_gqa_k, w_conv, a_log, dt_bias, g_gdn_out, w_up_a, w_up_b, w_up_c, w_out,
              g_norm2, w_ff1, w_ff2, g_final):
    b, t, _ = x.shape
    rows = t // GRID_W
    row = jnp.repeat(jnp.arange(rows, dtype=jnp.int32), GRID_W)
    col = jnp.tile(jnp.arange(GRID_W, dtype=jnp.int32), rows)
    pos = (row, col)
    mla_scale = (MLA_NOPE + MLA_ROPE) ** -0.5
    gqa_scale = GQA_HD ** -0.5
    s_zero = jnp.zeros((b, GDN_HEADS, GDN_DK, GDN_DV), jnp.float32)
    sc = jax.nn.silu(c)
    scc = jax.nn.silu(c_ctx)
    for l in range(DEPTH):
        last = l == DEPTH - 1
        mod_x = jnp.split((sc @ w_mod[l] + b_mod[l])[:, None, :], N_MOD, axis=-1)
        mod_c = jnp.split(scc @ w_mod[l] + b_mod[l], N_MOD, axis=-1)
        proj_w = (w_in[l], g_mla_q[l], w_mla_qb[l], g_mla_kv[l], w_mla_kvb[l], g_gqa_q[l], g_gqa_k[l])
        hx = rmsnorm(x, g_norm1[l]) * (1 + mod_x[1]) + mod_x[0]
        hc = rmsnorm(ctx, g_norm1[l]) * (1 + mod_c[1]) + mod_c[0]
        (qa_x, ka_x, va_x, qb_x, kb_x, vb_x, qkv_x, beta_x, dec_x, og_x, bg_x) = project_mixers(hx, *proj_w, pos)
        (qa_c, ka_c, va_c, qb_c, kb_c, vb_c, qkv_c, beta_c, dec_c, og_c, bg_c) = project_mixers(hc, *proj_w, None)
        oc_c, s_f, s_b = gdn_mixer(qkv_c, beta_c, dec_c, w_conv[l], a_log[l], dt_bias[l], s_zero, s_zero)
        oc_x, _, _ = gdn_mixer(qkv_x, beta_x, dec_x, w_conv[l], a_log[l], dt_bias[l], s_f, s_b)
        oa_x = blocked_attention(qa_x, jnp.concatenate([ka_x, ka_c], axis=1),
                                 jnp.concatenate([va_x, va_c], axis=1), mla_scale)
        ob_x = blocked_attention(qb_x, jnp.concatenate([kb_x, kb_c], axis=1),
                                 jnp.concatenate([vb_x, vb_c], axis=1), gqa_scale)
        mix_x = merge_branches(oa_x.reshape(b, t, -1), ob_x.reshape(b, t, -1),
                               gdn_output(oc_x, og_x, g_gdn_out[l], x.dtype), bg_x,
                               w_up_a[l], w_up_b[l], w_up_c[l], w_out[l])
        x = x + mod_x[2] * mix_x
        hx2 = rmsnorm(x, g_norm2[l]) * (1 + mod_x[4]) + mod_x[3]
        x = x + mod_x[5] * sqrelu_mlp(hx2, w_ff1[l], w_ff2[l])
        if not last:
            tc = ctx.shape[1]
            oa_c = blocked_attention(qa_c, ka_c, va_c, mla_scale)
            ob_c = blocked_attention(qb_c, kb_c, vb_c, gqa_scale)
            mix_c = merge_branches(oa_c.reshape(b, tc, -1), ob_c.reshape(b, tc, -1),
                                   gdn_output(oc_c, og_c, g_gdn_out[l], ctx.dtype), bg_c,
                                   w_up_a[l], w_up_b[l], w_up_c[l], w_out[l])
            ctx = ctx + mod_c[2] * mix_c
            hc2 = rmsnorm(ctx, g_norm2[l]) * (1 + mod_c[4]) + mod_c[3]
            ctx = ctx + mod_c[5] * sqrelu_mlp(hc2, w_ff1[l], w_ff2[l])
    return rmsnorm(x, g_final)
```

```python
import functools
import math

import numpy as np
import jax
import jax.numpy as jnp
from jax import lax
from jax.experimental import pallas as pl
from jax.experimental.pallas import tpu as pltpu

F32 = jnp.float32
BF16 = jnp.bfloat16

GRID_W = 64
EPS = 1e-6
ROPE_THETA = 10000.0
MLA_HEADS, MLA_Q_RANK, MLA_KV_RANK, MLA_NOPE, MLA_ROPE, MLA_V = 4, 512, 512, 128, 64, 128
GQA_HEADS, GQA_KV_HEADS, GQA_HD = 8, 2, 128
GDN_HEADS, GDN_DK, GDN_DV, GDN_CONV, GDN_CHUNK = 4, 128, 128, 5, 64
N_MOD = 6
LANE = 128
GDN_BLOCK = 256
VMEM_BIG = 56 * 1024 * 1024

OFF_GQ, OFF_CQ, OFF_CKV, OFF_GK, OFF_GV, OFF_OG, OFF_QKV, OFF_MISC = 0, 1024, 1536, 2048, 2304, 2560, 3072, 4608
N_SMALL = 5120
MISC_BETA, MISC_DEC = 64, 72


def _mm(a, b):
    return jnp.dot(a, b, preferred_element_type=F32)


def _mm_nt(a, b):
    return lax.dot_general(a, b, (((1,), (1,)), ((), ())), preferred_element_type=F32)


def _silu(x):
    return x / (1.0 + jnp.exp(-x))


def _sigmoid(x):
    return 1.0 / (1.0 + jnp.exp(-x))


def _rms(x, g):
    return x * lax.rsqrt(jnp.mean(x * x, axis=-1, keepdims=True) + EPS) * g


def _norm_mod(x, g, shift, scale):
    return _rms(x, g) * (1.0 + scale) + shift


def _params(sem, vmem=None):
    kw = dict(dimension_semantics=sem)
    if vmem is not None:
        kw["vmem_limit_bytes"] = vmem
    return pltpu.CompilerParams(**kw)


def _pick_block(pref, *sizes):
    b = pref
    while any(s % b for s in sizes):
        b //= 2
    return b


def _mod_kernel(s_ref, w_ref, b_ref, o_ref):
    s = _silu(s_ref[...])
    o_ref[0] = _mm(s.astype(BF16), w_ref[0].astype(BF16)) + b_ref[0]


def _modulation(c, c_ctx, w_mod, b_mod):
    L, D, N = w_mod.shape
    B = c.shape[0]
    rows = jnp.zeros((8, D), F32).at[:B].set(c).at[B].set(c_ctx)
    bn = 1024
    out = pl.pallas_call(
        _mod_kernel,
        grid=(L, N // bn),
        in_specs=[pl.BlockSpec((8, D), lambda l, j: (0, 0)),
                  pl.BlockSpec((1, D, bn), lambda l, j: (l, 0, j)),
                  pl.BlockSpec((1, 1, bn), lambda l, j: (l, 0, j))],
        out_specs=pl.BlockSpec((1, 8, bn), lambda l, j: (l, 0, j)),
        out_shape=jax.ShapeDtypeStruct((L, 8, N), F32),
        compiler_params=_params(("parallel", "parallel"), VMEM_BIG),
        name="adaln_mod",
    )(rows, w_mod, b_mod.reshape(L, 1, N))
    return out.reshape(L, 8, N_MOD, D)


def _inproj_kernel(x_ref, mod_ref, g_ref, w_ref, o_ref, hx_ref):
    @pl.when(pl.program_id(1) == 0)
    def _():
        m = mod_ref[0]
        hx_ref[...] = _norm_mod(x_ref[...], g_ref[...], m[0:1], m[1:2]).astype(BF16)

    o_ref[...] = _mm(hx_ref[...], w_ref[...])


def _in_proj(xs, mod_l, g1, wa, geom):
    R, D = xs.shape
    N = wa.shape[1]
    bm = _pick_block(512, geom["T"], geom["RC"])
    bn = 1024
    sel = geom["mod_sel"](bm)
    return pl.pallas_call(
        _inproj_kernel,
        grid=(R // bm, N // bn),
        in_specs=[pl.BlockSpec((bm, D), lambda i, j: (i, 0)),
                  pl.BlockSpec((1, N_MOD, D), lambda i, j: (sel(i), 0, 0)),
                  pl.BlockSpec((1, D), lambda i, j: (0, 0)),
                  pl.BlockSpec((D, bn), lambda i, j: (0, j))],
        out_specs=pl.BlockSpec((bm, bn), lambda i, j: (i, j)),
        out_shape=jax.ShapeDtypeStruct((R, N), F32),
        scratch_shapes=[pltpu.VMEM((bm, D), BF16)],
        compiler_params=_params(("parallel", "arbitrary"), VMEM_BIG),
        name="in_proj",
    )(xs, mod_l, g1.reshape(1, D), wa)


def _prep_kernel(gq_ref, cq_ref, ckv_ref, gk_ref, gv_ref, misc_ref, gmq_ref, gmkv_ref, wqb_ref, wkvb_ref,
                 ggq_ref, ggk_ref, cosg_ref, sing_ref, cosm_ref, sinm_ref,
                 qm_ref, km_ref, vm_ref, qg_ref, kg_ref, vg_ref):
    lane = lax.broadcasted_iota(jnp.int32, (1, LANE), 1)
    cosm, sinm = cosm_ref[...], sinm_ref[...]
    cosg, sing = cosg_ref[...], sing_ref[...]
    first_m = (lane & 31) < 16
    first_g = (lane & 63) < 32

    def rope_m(x):
        partner = jnp.where(first_m, pltpu.roll(x, LANE - 16, 1), pltpu.roll(x, 16, 1))
        return x * cosm + partner * sinm

    def rope_g(x):
        partner = jnp.where(first_g, pltpu.roll(x, LANE - 32, 1), pltpu.roll(x, 32, 1))
        return x * cosg + partner * sing

    qa = _mm(_rms(cq_ref[...], gmq_ref[...]).astype(BF16), wqb_ref[...])
    kva = _mm(_rms(ckv_ref[...], gmkv_ref[...]).astype(BF16), wkvb_ref[...])
    kpe = rope_m(jnp.where(lane < MLA_ROPE, misc_ref[...], 0.0)).astype(BF16)
    nh = MLA_HEADS * LANE
    for h in range(MLA_HEADS):
        hs = slice(h * LANE, (h + 1) * LANE)
        ps = slice(nh + h * LANE, nh + (h + 1) * LANE)
        qm_ref[h, :, 0:LANE] = qa[:, hs].astype(BF16)
        qm_ref[h, :, LANE:2 * LANE] = rope_m(qa[:, ps]).astype(BF16)
        km_ref[h, :, 0:LANE] = kva[:, hs].astype(BF16)
        km_ref[h, :, LANE:2 * LANE] = kpe
        vm_ref[h] = kva[:, ps].astype(BF16)

    gq = gq_ref[...]
    ggq, ggk = ggq_ref[...], ggk_ref[...]
    for h in range(GQA_HEADS):
        qg_ref[h] = rope_g(_rms(gq[:, h * LANE:(h + 1) * LANE], ggq)).astype(BF16)
    gk, gv = gk_ref[...], gv_ref[...]
    for h in range(GQA_KV_HEADS):
        hs = slice(h * LANE, (h + 1) * LANE)
        kg_ref[h] = rope_g(_rms(gk[:, hs], ggk)).astype(BF16)
        vg_ref[h] = gv[:, hs].astype(BF16)


def _attn_prep(P, gmq, gmkv, wqb, wkvb, ggq, ggk, tabs, geom):
    R = P.shape[0]
    bm = _pick_block(512, geom["T"], geom["RC"])
    nlat = geom["RL"] // bm
    per = geom["T"] // bm
    tsel = lambda i: jnp.where(i < nlat, i % per, per)
    col = lambda w, off: pl.BlockSpec((bm, w), lambda i: (i, off // w))
    full = lambda a: pl.BlockSpec(a.shape, lambda i: (0,) * a.ndim)
    tab = pl.BlockSpec((bm, LANE), lambda i: (tsel(i), 0))
    outs = [((MLA_HEADS, R, 2 * LANE), 2 * LANE), ((MLA_HEADS, R, 2 * LANE), 2 * LANE), ((MLA_HEADS, R, LANE), LANE),
            ((GQA_HEADS, R, LANE), LANE), ((GQA_KV_HEADS, R, LANE), LANE), ((GQA_KV_HEADS, R, LANE), LANE)]
    gmq, gmkv, ggq, ggk = (a.reshape(1, -1) for a in (gmq, gmkv, ggq, ggk))
    return pl.pallas_call(
        _prep_kernel,
        grid=(R // bm,),
        in_specs=[col(1024, OFF_GQ), col(512, OFF_CQ), col(512, OFF_CKV), col(256, OFF_GK), col(256, OFF_GV),
                  col(LANE, OFF_MISC), full(gmq), full(gmkv), full(wqb), full(wkvb), full(ggq), full(ggk),
                  tab, tab, tab, tab],
        out_specs=[pl.BlockSpec((s[0], bm, w), lambda i: (0, i, 0)) for s, w in outs],
        out_shape=[jax.ShapeDtypeStruct(s, BF16) for s, _ in outs],
        compiler_params=_params(("parallel",), VMEM_BIG),
        name="attn_prep",
    )(P, P, P, P, P, P, gmq, gmkv, wqb, wkvb, ggq, ggk, *tabs)


def _attn_kernel(q_ref, kx_ref, kc_ref, vx_ref, vc_ref, o_ref, *, scale):
    q = q_ref[0]
    s1 = _mm_nt(q, kx_ref[0]) * scale
    s2 = _mm_nt(q, kc_ref[0]) * scale
    m = jnp.maximum(jnp.max(s1, axis=-1, keepdims=True), jnp.max(s2, axis=-1, keepdims=True))
    p1 = jnp.exp(s1 - m)
    p2 = jnp.exp(s2 - m)
    l = jnp.sum(p1, axis=-1, keepdims=True) + jnp.sum(p2, axis=-1, keepdims=True)
    o = _mm(p1.astype(BF16), vx_ref[0]) + _mm(p2.astype(BF16), vc_ref[0])
    o_ref[...] = (o / l).astype(BF16)


def _attn_ctx_kernel(q_ref, kc_ref, vc_ref, o_ref, *, scale):
    s = _mm_nt(q_ref[0], kc_ref[0]) * scale
    p = jnp.exp(s - jnp.max(s, axis=-1, keepdims=True))
    l = jnp.sum(p, axis=-1, keepdims=True)
    o_ref[...] = (_mm(p.astype(BF16), vc_ref[0]) / l).astype(BF16)


def _attention(q, k, v, scale, geom, name):
    H, _, dk = q.shape
    Hk, _, dv = v.shape
    grp = H // Hk
    B, T, C, RL = geom["B"], geom["T"], geom["C"], geom["RL"]
    bq = _pick_block(512, T)
    nq = T // bq
    cb = RL // C
    return pl.pallas_call(
        functools.partial(_attn_kernel, scale=scale),
        grid=(B, H, nq),
        in_specs=[pl.BlockSpec((1, bq, dk), lambda b, h, j: (h, b * nq + j, 0)),
                  pl.BlockSpec((1, T, dk), lambda b, h, j: (h // grp, b, 0)),
                  pl.BlockSpec((1, C, dk), lambda b, h, j: (h // grp, cb + b, 0)),
                  pl.BlockSpec((1, T, dv), lambda b, h, j: (h // grp, b, 0)),
                  pl.BlockSpec((1, C, dv), lambda b, h, j: (h // grp, cb + b, 0))],
        out_specs=pl.BlockSpec((bq, dv), lambda b, h, j: (b * nq + j, h)),
        out_shape=jax.ShapeDtypeStruct((RL, H * dv), BF16),
        compiler_params=_params(("parallel", "parallel", "parallel"), VMEM_BIG),
        name=name,
    )(q, k, k, v, v)


def _attention_ctx(q, k, v, scale, geom, name):
    H, _, dk = q.shape
    Hk, _, dv = v.shape
    grp = H // Hk
    B, C, RL, RC = geom["B"], geom["C"], geom["RL"], geom["RC"]
    cb = RL // C
    return pl.pallas_call(
        functools.partial(_attn_ctx_kernel, scale=scale),
        grid=(B, H),
        in_specs=[pl.BlockSpec((1, C, dk), lambda b, h: (h, cb + b, 0)),
                  pl.BlockSpec((1, C, dk), lambda b, h: (h // grp, cb + b, 0)),
                  pl.BlockSpec((1, C, dv), lambda b, h: (h // grp, cb + b, 0))],
        out_specs=pl.BlockSpec((C, dv), lambda b, h: (b, h)),
        out_shape=jax.ShapeDtypeStruct((RC, H * dv), BF16),
        compiler_params=_params(("parallel", "parallel")),
        name=name,
    )(q, k, v)


def _gdn_a_kernel(cur_ref, prev_ref, next_ref, misc_ref, wc_ref, gvec_ref, qkvn_ref, gates_ref, ext_ref,
                  *, nlat, per_seq):
    r = pl.program_id(0)
    pos = r % per_seq
    is_ctx = r >= nlat
    pf = jnp.where(jnp.logical_or(is_ctx, pos == 0), 0.0, 1.0).astype(F32)
    nf = jnp.where(jnp.logical_or(is_ctx, pos == per_seq - 1), 0.0, 1.0).astype(F32)
    nb = GDN_BLOCK
    ext_ref[0:8, :] = prev_ref[...] * pf
    ext_ref[8:8 + nb, :] = cur_ref[...]
    ext_ref[8 + nb:16 + nb, :] = next_ref[...] * nf
    pad = GDN_CONV // 2
    n_qk = 2 * GDN_HEADS
    for c in range(3 * GDN_HEADS):
        cs = slice(c * LANE, (c + 1) * LANE)
        acc = None
        for j in range(GDN_CONV):
            t = ext_ref[8 - pad + j:8 - pad + j + nb, cs] * wc_ref[j:j + 1, cs]
            acc = t if acc is None else acc + t
        y = _silu(acc)
        if c < n_qk:
            y = y * lax.rsqrt(jnp.sum(y * y, axis=-1, keepdims=True) + EPS)
        qkvn_ref[:, cs] = y

    lane = lax.broadcasted_iota(jnp.int32, (1, LANE), 1)
    raw = misc_ref[...]
    beta = _sigmoid(raw)
    z = raw + gvec_ref[1:2, :]
    softplus = jnp.maximum(z, 0.0) + jnp.log1p(jnp.exp(-jnp.abs(z)))
    g = -jnp.exp(gvec_ref[0:1, :]) * softplus
    is_beta = jnp.logical_and(lane >= MISC_BETA, lane < MISC_DEC)
    is_g = jnp.logical_and(lane >= MISC_DEC, lane < MISC_DEC + 2 * GDN_HEADS)
    gates_ref[...] = jnp.where(is_beta, beta, jnp.where(is_g, g, 0.0))


def _gdn_conv(P, w_conv, a_log, dt_bias, geom):
    R = P.shape[0]
    nb = GDN_BLOCK
    W = 3 * GDN_HEADS * LANE
    nblk = R // nb
    sub = nb // 8
    gvec = jnp.zeros((2, LANE), F32)
    gvec = gvec.at[0, MISC_DEC:MISC_DEC + 2 * GDN_HEADS].set(a_log.reshape(-1))
    gvec = gvec.at[1, MISC_DEC:MISC_DEC + 2 * GDN_HEADS].set(dt_bias.reshape(-1))
    qc = OFF_QKV // W
    return pl.pallas_call(
        functools.partial(_gdn_a_kernel, nlat=geom["RL"] // nb, per_seq=geom["T"] // nb),
        grid=(nblk,),
        in_specs=[pl.BlockSpec((nb, W), lambda r: (r, qc)),
                  pl.BlockSpec((8, W), lambda r: (jnp.maximum(r * sub - 1, 0), qc)),
                  pl.BlockSpec((8, W), lambda r: (jnp.minimum((r + 1) * sub, R // 8 - 1), qc)),
                  pl.BlockSpec((nb, LANE), lambda r: (r, OFF_MISC // LANE)),
                  pl.BlockSpec((GDN_CONV, W), lambda r: (0, 0)),
                  pl.BlockSpec((2, LANE), lambda r: (0, 0))],
        out_specs=[pl.BlockSpec((nb, W), lambda r: (r, 0)),
                   pl.BlockSpec((nb, LANE), lambda r: (r, 0))],
        out_shape=[jax.ShapeDtypeStruct((R, W), F32), jax.ShapeDtypeStruct((R, LANE), F32)],
        scratch_shapes=[pltpu.VMEM((nb + 16, W), F32)],
        compiler_params=_params(("parallel",)),
        name="gdn_conv",
    )(P, P, P, P, w_conv, gvec)


def _split3(x):
    h = x.astype(BF16)
    r = x - h.astype(F32)
    m = r.astype(BF16)
    l = (r - m.astype(F32)).astype(BF16)
    return h, m, l


def _mm3(a, b):
    ah = a.astype(BF16)
    al = (a - ah.astype(F32)).astype(BF16)
    bh = b.astype(BF16)
    bl = (b - bh.astype(F32)).astype(BF16)
    return _mm(ah, bl) + _mm(al, bh) + _mm(ah, bh)


def _gdn_b_kernel(qkvn_ref, gates_ref, u_ref, w_ref, qg_ref, kgt_ref, qk_ref):
    nb = GDN_BLOCK
    nh = GDN_HEADS
    ri = lax.broadcasted_iota(jnp.int32, (nb, nb), 0)
    ci = lax.broadcasted_iota(jnp.int32, (nb, nb), 1)
    same = (ri >> 6) == (ci >> 6)
    low = jnp.logical_and(same, ri >= ci)
    upp = jnp.logical_and(same, ri <= ci)
    slow = jnp.logical_and(same, ri > ci)
    supp = jnp.logical_and(same, ri < ci)
    ltri = jnp.where(low, 1.0, 0.0).astype(BF16)
    utri = jnp.where(upp, 1.0, 0.0).astype(BF16)

    gt = gates_ref[...]
    g3 = _split3(gt)
    g3t = _split3(gt.T)
    cum_c = (sum(_mm(ltri, p) for p in reversed(g3)), sum(_mm(utri, p) for p in reversed(g3)))
    cum_r = (sum(_mm(p, utri) for p in reversed(g3t)), sum(_mm(p, ltri) for p in reversed(g3t)))
    tot_c = cum_c[0] + cum_c[1] - gt

    scale = GDN_DK ** -0.5
    for d in range(2):
        mask, smask = (low, slow) if d == 0 else (upp, supp)
        for h in range(nh):
            ib = MISC_BETA + d * nh + h
            ig = MISC_DEC + d * nh + h
            beta = gt[:, ib:ib + 1]
            gc = cum_c[d][:, ig:ig + 1]
            gr = cum_r[d][ig:ig + 1, :]
            gl = tot_c[:, ig:ig + 1]
            q = qkvn_ref[:, h * LANE:(h + 1) * LANE]
            k = qkvn_ref[:, (nh + h) * LANE:(nh + h + 1) * LANE]
            v = qkvn_ref[:, (2 * nh + h) * LANE:(2 * nh + h + 1) * LANE]
            decay = jnp.where(mask, jnp.exp(jnp.where(mask, gc - gr, 0.0)), 0.0)
            kb = k * beta
            k16 = k.astype(BF16)
            neg_a = jnp.where(smask, -(_mm_nt(kb.astype(BF16), k16) * decay), 0.0)
            x = jnp.concatenate([v * beta, kb * jnp.exp(gc)], axis=1)
            p = neg_a
            n_fac = int(math.log2(GDN_CHUNK))
            for it in range(n_fac):
                x = x + _mm3(p, x)
                if it + 1 < n_fac:
                    p = _mm3(p, p)
            hs = slice(h * LANE, (h + 1) * LANE)
            u_ref[d, :, hs] = x[:, :LANE]
            w_ref[d, :, hs] = x[:, LANE:].astype(BF16)
            qs = q * scale
            qk_ref[d, :, h * nb:(h + 1) * nb] = (_mm_nt(qs.astype(BF16), k16) * decay).astype(BF16)
            qg_ref[d, :, hs] = (qs * jnp.exp(gc)).astype(BF16)
            kgt_ref[d, hs, :] = (k * jnp.exp(gl - gc)).T.astype(BF16)


def _gdn_local(qkvn, gates):
    R, W = qkvn.shape
    nb = GDN_BLOCK
    HW = GDN_HEADS * LANE
    return pl.pallas_call(
        _gdn_b_kernel,
        grid=(R // nb,),
        in_specs=[pl.BlockSpec((nb, W), lambda r: (r, 0)),
                  pl.BlockSpec((nb, LANE), lambda r: (r, 0))],
        out_specs=[pl.BlockSpec((2, nb, HW), lambda r: (0, r, 0)),
                   pl.BlockSpec((2, nb, HW), lambda r: (0, r, 0)),
                   pl.BlockSpec((2, nb, HW), lambda r: (0, r, 0)),
                   pl.BlockSpec((2, HW, nb), lambda r: (0, 0, r)),
                   pl.BlockSpec((2, nb, GDN_HEADS * nb), lambda r: (0, r, 0))],
        out_shape=[jax.ShapeDtypeStruct((2, R, HW), F32),
                   jax.ShapeDtypeStruct((2, R, HW), BF16),
                   jax.ShapeDtypeStruct((2, R, HW), BF16),
                   jax.ShapeDtypeStruct((2, HW, R), BF16),
                   jax.ShapeDtypeStruct((2, R, GDN_HEADS * nb), BF16)],
        compiler_params=_params(("parallel",), VMEM_BIG),
        name="gdn_local",
    )(qkvn, gates)


def _gdn_c_kernel(u_ref, w_ref, qg_ref, kgt_ref, qk_ref, gates_ref, o_ref, s_ref, vn_ref, *, d):
    nb = GDN_BLOCK
    nh = GDN_HEADS

    @pl.when(pl.program_id(1) == 0)
    def _():
        s_ref[...] = jnp.zeros_like(s_ref)
        vn_ref[...] = jnp.zeros_like(vn_ref)

    rowi = lax.broadcasted_iota(jnp.int32, (nb, 1), 0)
    nchunk = nb // GDN_CHUNK
    order = range(nchunk) if d == 0 else range(nchunk - 1, -1, -1)
    for c in order:
        rs = slice(c * GDN_CHUNK, (c + 1) * GDN_CHUNK)
        decay_all = jnp.exp(jnp.sum(gates_ref[rs, :], axis=0, keepdims=True))
        cmask = jnp.logical_and(rowi >= c * GDN_CHUNK, rowi < (c + 1) * GDN_CHUNK)
        for h in range(nh):
            hs = slice(h * LANE, (h + 1) * LANE)
            ig = MISC_DEC + d * nh + h
            s = s_ref[h]
            s16 = s.astype(BF16)
            v_new = u_ref[0, rs, hs] - _mm(w_ref[0, rs, hs], s16)
            vn_ref[h, rs, :] = v_new.astype(BF16)
            vn = vn_ref[h]
            o_ref[rs, hs] = _mm(qg_ref[0, rs, hs], s16) + _mm(qk_ref[0, rs, h * nb:(h + 1) * nb], vn)
            v_cur = jnp.where(cmask, vn, jnp.zeros_like(vn))
            s_ref[h] = s * decay_all[:, ig:ig + 1] + _mm(kgt_ref[0, hs, :], v_cur)


def _gdn_scan(u, w, qg, kgt, qk, gates, geom, d):
    nb = GDN_BLOCK
    HW = GDN_HEADS * LANE
    B, RL, R = geom["B"], geom["RL"], geom["R"]
    per = geom["T"] // nb
    nlat = RL // nb

    def blk(b, j):
        lat = b * per + (j - 1 if d == 0 else per - j)
        return jnp.where(j == 0, nlat + b, lat)

    tok = lambda w_: pl.BlockSpec((1, nb, w_), lambda b, j: (d, blk(b, j), 0))
    return pl.pallas_call(
        functools.partial(_gdn_c_kernel, d=d),
        grid=(B, per + 1),
        in_specs=[tok(HW), tok(HW), tok(HW),
                  pl.BlockSpec((1, HW, nb), lambda b, j: (d, 0, blk(b, j))),
                  tok(GDN_HEADS * nb),
                  pl.BlockSpec((nb, LANE), lambda b, j: (blk(b, j), 0))],
        out_specs=pl.BlockSpec((nb, HW), lambda b, j: (blk(b, j), 0)),
        out_shape=jax.ShapeDtypeStruct((R, HW), F32),
        scratch_shapes=[pltpu.VMEM((GDN_HEADS, GDN_DK, GDN_DV), F32),
                        pltpu.VMEM((GDN_HEADS, nb, GDN_DV), BF16)],
        compiler_params=_params(("parallel", "arbitrary")),
        name="gdn_scan_fwd" if d == 0 else "gdn_scan_bwd",
    )(u, w, qg, kgt, qk, gates)


def _gdn_out_kernel(of_ref, ob_ref, og_ref, g_ref, o_ref):
    o = of_ref[...] + ob_ref[...]
    gate = _silu(og_ref[...])
    g = g_ref[...]
    for h in range(GDN_HEADS):
        hs = slice(h * LANE, (h + 1) * LANE)
        o_ref[:, hs] = (_rms(o[:, hs], g) * gate[:, hs]).astype(BF16)


def _gdn_out(o_f, o_b, P, g_out, rows):
    HW = GDN_HEADS * LANE
    bm = 512 if rows % 512 == 0 else 256
    return pl.pallas_call(
        _gdn_out_kernel,
        grid=(rows // bm,),
        in_specs=[pl.BlockSpec((bm, HW), lambda i: (i, 0)),
                  pl.BlockSpec((bm, HW), lambda i: (i, 0)),
                  pl.BlockSpec((bm, HW), lambda i: (i, OFF_OG // HW)),
                  pl.BlockSpec((1, LANE), lambda i: (0, 0))],
        out_specs=pl.BlockSpec((bm, HW), lambda i: (i, 0)),
        out_shape=jax.ShapeDtypeStruct((rows, HW), BF16),
        compiler_params=_params(("parallel",)),
        name="gdn_out",
    )(o_f, o_b, P, g_out.reshape(1, LANE))


def _merge_kernel(x_ref, mod_ref, g_ref, oa_ref, ob_ref, oc_ref, wg_ref, wua_ref, wub_ref, wuc_ref, wo_ref,
                  o_ref, hx_ref, acc_ref):
    j = pl.program_id(1)
    m = mod_ref[0]

    @pl.when(j == 0)
    def _():
        hx_ref[...] = _norm_mod(x_ref[...], g_ref[...], m[0:1], m[1:2]).astype(BF16)
        acc_ref[...] = jnp.zeros_like(acc_ref)

    hx = hx_ref[...]
    y = (_sigmoid(_mm(hx, wg_ref[0])) * _mm(oa_ref[...], wua_ref[...])
         + _sigmoid(_mm(hx, wg_ref[1])) * _mm(ob_ref[...], wub_ref[...])
         + _sigmoid(_mm(hx, wg_ref[2])) * _mm(oc_ref[...], wuc_ref[...]))
    acc_ref[...] += _mm(y.astype(BF16), wo_ref[...])

    @pl.when(j == pl.num_programs(1) - 1)
    def _():
        o_ref[...] = x_ref[...] + m[2:3] * acc_ref[...]


def _merge(xs, mod_l, g1, oa, ob, oc, wg, wua, wub, wuc, wo, geom, rows):
    D = xs.shape[1]
    bm = _pick_block(512, geom["T"], geom["RC"])
    hc = 256
    sel = geom["mod_sel"](bm)
    rowb = lambda w_: pl.BlockSpec((bm, w_), lambda i, j: (i, 0))
    return pl.pallas_call(
        _merge_kernel,
        grid=(rows // bm, D // hc),
        in_specs=[rowb(D),
                  pl.BlockSpec((1, N_MOD, D), lambda i, j: (sel(i), 0, 0)),
                  pl.BlockSpec((1, D), lambda i, j: (0, 0)),
                  rowb(oa.shape[1]), rowb(ob.shape[1]), rowb(oc.shape[1]),
                  pl.BlockSpec((3, D, hc), lambda i, j: (0, 0, j)),
                  pl.BlockSpec((wua.shape[0], hc), lambda i, j: (0, j)),
                  pl.BlockSpec((wub.shape[0], hc), lambda i, j: (0, j)),
                  pl.BlockSpec((wuc.shape[0], hc), lambda i, j: (0, j)),
                  pl.BlockSpec((hc, D), lambda i, j: (j, 0))],
        out_specs=rowb(D),
        out_shape=jax.ShapeDtypeStruct((rows, D), F32),
        scratch_shapes=[pltpu.VMEM((bm, D), BF16), pltpu.VMEM((bm, D), F32)],
        compiler_params=_params(("parallel", "arbitrary"), VMEM_BIG),
        name="merge",
    )(xs, mod_l, g1.reshape(1, D), oa, ob, oc, wg, wua, wub, wuc, wo)


def _mlp_kernel(x_ref, mod_ref, g_ref, w1_ref, w2_ref, gf_ref, o_ref, hx_ref, acc_ref, *, final):
    j = pl.program_id(1)
    m = mod_ref[0]

    @pl.when(j == 0)
    def _():
        hx_ref[...] = _norm_mod(x_ref[...], g_ref[...], m[3:4], m[4:5]).astype(BF16)
        acc_ref[...] = jnp.zeros_like(acc_ref)

    h = jnp.maximum(_mm(hx_ref[...], w1_ref[...]), 0.0)
    acc_ref[...] += _mm((h * h).astype(BF16), w2_ref[...])

    @pl.when(j == pl.num_programs(1) - 1)
    def _():
        x2 = x_ref[...] + m[5:6] * acc_ref[...]
        o_ref[...] = _rms(x2, gf_ref[...]) if final else x2


def _mlp(x1, mod_l, g2, w1, w2, g_final, geom, final):
    rows, D = x1.shape
    F = w1.shape[1]
    bm = _pick_block(512, geom["T"], geom["RC"])
    fc = 512
    sel = geom["mod_sel"](bm)
    return pl.pallas_call(
        functools.partial(_mlp_kernel, final=final),
        grid=(rows // bm, F // fc),
        in_specs=[pl.BlockSpec((bm, D), lambda i, j: (i, 0)),
                  pl.BlockSpec((1, N_MOD, D), lambda i, j: (sel(i), 0, 0)),
                  pl.BlockSpec((1, D), lambda i, j: (0, 0)),
                  pl.BlockSpec((D, fc), lambda i, j: (0, j)),
                  pl.BlockSpec((fc, D), lambda i, j: (j, 0)),
                  pl.BlockSpec((1, D), lambda i, j: (0, 0))],
        out_specs=pl.BlockSpec((bm, D), lambda i, j: (i, 0)),
        out_shape=jax.ShapeDtypeStruct((rows, D), F32),
        scratch_shapes=[pltpu.VMEM((bm, D), BF16), pltpu.VMEM((bm, D), F32)],
        compiler_params=_params(("parallel", "arbitrary"), VMEM_BIG),
        name="mlp",
    )(x1, mod_l, g2.reshape(1, D), w1, w2, g_final.reshape(1, D))


def _rope_tables(T, bm):
    t = np.arange(T)
    row, col = t // GRID_W, t % GRID_W

    def table(n_rot):
        n = n_rot // 2
        half = n // 2
        inv = ROPE_THETA ** (-np.arange(half, dtype=np.float64) / half)
        cos = np.zeros((T + bm, LANE))
        sin = np.zeros((T + bm, LANE))
        for s, pos in enumerate((row, col)):
            ang = pos[:, None].astype(np.float32).astype(np.float64) * inv.astype(np.float32)[None, :]
            ang = ang.astype(np.float32).astype(np.float64)
            c, sn = np.cos(ang), np.sin(ang)
            cos[:T, s * n:s * n + half] = c
            cos[:T, s * n + half:(s + 1) * n] = c
            sin[:T, s * n:s * n + half] = -sn
            sin[:T, s * n + half:(s + 1) * n] = sn
        cos[T:, :] = 1.0
        return jnp.asarray(cos, F32), jnp.asarray(sin, F32)

    cg, sg = table(GQA_HD)
    cm, sm = table(MLA_ROPE)
    return cg, sg, cm, sm


def _prep_layer_weights(w_in, w_qb, w_kvb):
    D = w_in.shape[0]
    splits = (MLA_Q_RANK, MLA_KV_RANK, MLA_ROPE, GQA_HEADS * GQA_HD, GQA_KV_HEADS * GQA_HD, GQA_KV_HEADS * GQA_HD,
              GDN_HEADS * (2 * GDN_DK + GDN_DV), 2 * GDN_HEADS, 2 * GDN_HEADS, GDN_HEADS * GDN_DV, 3 * D)
    off = np.cumsum((0,) + splits)
    cq, ckv, kpe, gq, gk, gv, qkv, beta, dec, og, bg = (w_in[:, off[i]:off[i + 1]] for i in range(len(splits)))
    used = OFF_MISC + MLA_ROPE + 4 * GDN_HEADS
    wa = jnp.concatenate([gq, cq, ckv, gk, gv, og, qkv, kpe, beta, dec, jnp.zeros((D, N_SMALL - used), F32)],
                         axis=1).astype(BF16)
    wg = bg.reshape(D, 3, D).transpose(1, 0, 2).astype(BF16)
    qb = w_qb.reshape(-1, MLA_HEADS, MLA_NOPE + MLA_ROPE)
    qpe = jnp.pad(qb[:, :, MLA_NOPE:], ((0, 0), (0, 0), (0, LANE - MLA_ROPE)))
    wqb = jnp.concatenate([qb[:, :, :MLA_NOPE].reshape(-1, MLA_HEADS * LANE),
                           qpe.reshape(-1, MLA_HEADS * LANE)], axis=1).astype(BF16)
    kvb = w_kvb.reshape(-1, MLA_HEADS, MLA_NOPE + MLA_V)
    wkvb = jnp.concatenate([kvb[:, :, :MLA_NOPE].reshape(-1, MLA_HEADS * LANE),
                            kvb[:, :, MLA_NOPE:].reshape(-1, MLA_HEADS * LANE)], axis=1).astype(BF16)
    return wa, wg, wqb, wkvb


def kernel(x, c, ctx, c_ctx, w_mod, b_mod, g_norm1, w_in, g_mla_q, w_mla_qb, g_mla_kv, w_mla_kvb, g_gqa_q, g_gqa_k,
           w_conv, a_log, dt_bias, g_gdn_out, w_up_a, w_up_b, w_up_c, w_out, g_norm2, w_ff1, w_ff2, g_final):
    B, T, D = x.shape
    C = ctx.shape[1]
    L = w_mod.shape[0]
    assert C == GDN_BLOCK and T % GDN_BLOCK == 0 and T % GRID_W == 0 and B < 8
    RL, RC = B * T, B * C
    geom = dict(B=B, T=T, C=C, RL=RL, RC=RC, R=RL + RC)
    geom["mod_sel"] = lambda bm: (lambda i: jnp.where(i < RL // bm, i // (T // bm), B))

    mod = _modulation(c, c_ctx, w_mod, b_mod)
    tabs = _rope_tables(T, _pick_block(512, T, RC))
    xs = jnp.concatenate([x.reshape(RL, D), ctx.reshape(RC, D)], axis=0)
    mla_scale = (MLA_NOPE + MLA_ROPE) ** -0.5
    gqa_scale = GQA_HD ** -0.5

    for l in range(L):
        last = l == L - 1
        wa, wg, wqb, wkvb = _prep_layer_weights(w_in[l], w_mla_qb[l], w_mla_kvb[l])
        P = _in_proj(xs, mod[l], g_norm1[l], wa, geom)
        qm, km, vm, qg, kg, vg = _attn_prep(P, g_mla_q[l], g_mla_kv[l], wqb, wkvb, g_gqa_q[l], g_gqa_k[l], tabs, geom)
        oa = _attention(qm, km, vm, mla_scale, geom, "attn_mla")
        ob = _attention(qg, kg, vg, gqa_scale, geom, "attn_gqa")
        qkvn, gates = _gdn_conv(P, w_conv[l], a_log[l], dt_bias[l], geom)
        u, w, qgd, kgt, qk = _gdn_local(qkvn, gates)
        o_f = _gdn_scan(u, w, qgd, kgt, qk, gates, geom, 0)
        o_b = _gdn_scan(u, w, qgd, kgt, qk, gates, geom, 1)
        rows = RL if last else RL + RC
        oc = _gdn_out(o_f, o_b, P, g_gdn_out[l], rows)
        if not last:
            oa = jnp.concatenate([oa, _attention_ctx(qm, km, vm, mla_scale, geom, "attn_mla_ctx")], axis=0)
            ob = jnp.concatenate([ob, _attention_ctx(qg, kg, vg, gqa_scale, geom, "attn_gqa_ctx")], axis=0)
        x1 = _merge(xs, mod[l], g_norm1[l], oa, ob, oc, wg, w_up_a[l].astype(BF16), w_up_b[l].astype(BF16),
                    w_up_c[l].astype(BF16), w_out[l].astype(BF16), geom, rows)
        xs = _mlp(x1, mod[l], g_norm2[l], w_ff1[l].astype(BF16), w_ff2[l].astype(BF16), g_final, geom, last)
    return xs.reshape(B, T, D)
```

```python
import functools
import math

import numpy as np
import jax
import jax.numpy as jnp
from jax import lax
from jax.experimental import pallas as pl
from jax.experimental.pallas import tpu as pltpu

F32 = jnp.float32
BF16 = jnp.bfloat16

GRID_W = 64
EPS = 1e-6
ROPE_THETA = 10000.0
MLA_HEADS, MLA_Q_RANK, MLA_KV_RANK, MLA_NOPE, MLA_ROPE, MLA_V = 4, 512, 512, 128, 64, 128
GQA_HEADS, GQA_KV_HEADS, GQA_HD = 8, 2, 128
GDN_HEADS, GDN_DK, GDN_DV, GDN_CONV, GDN_CHUNK = 4, 128, 128, 5, 64
N_MOD = 6
LANE = 128
GDN_BLOCK = 256
VMEM_BIG = 56 * 1024 * 1024

OFF_GQ, OFF_CQ, OFF_CKV, OFF_GK, OFF_GV, OFF_OG, OFF_QKV, OFF_MISC = 0, 1024, 1536, 2048, 2304, 2560, 3072, 4608
N_SMALL = 5120
MISC_BETA, MISC_DEC = 64, 72


def _mm(a, b):
    return jnp.dot(a, b, preferred_element_type=F32)


def _mm_nt(a, b):
    return lax.dot_general(a, b, (((1,), (1,)), ((), ())), preferred_element_type=F32)


def _silu(x):
    return x / (1.0 + jnp.exp(-x))


def _sigmoid(x):
    return 1.0 / (1.0 + jnp.exp(-x))


def _rms(x, g):
    return x * lax.rsqrt(jnp.mean(x * x, axis=-1, keepdims=True) + EPS) * g


def _norm_mod(x, g, shift, scale):
    return _rms(x, g) * (1.0 + scale) + shift


def _params(sem, vmem=None):
    kw = dict(dimension_semantics=sem)
    if vmem is not None:
        kw["vmem_limit_bytes"] = vmem
    return pltpu.CompilerParams(**kw)


def _pick_block(pref, *sizes):
    b = pref
    while any(s % b for s in sizes):
        b //= 2
    return b


def _mod_kernel(s_ref, w_ref, b_ref, o_ref):
    s = _silu(s_ref[...])
    o_ref[0] = _mm(s.astype(BF16), w_ref[0].astype(BF16)) + b_ref[0]


def _modulation(c, c_ctx, w_mod, b_mod):
    L, D, N = w_mod.shape
    B = c.shape[0]
    rows = jnp.zeros((8, D), F32).at[:B].set(c).at[B].set(c_ctx)
    bn = 1024
    out = pl.pallas_call(
        _mod_kernel,
        grid=(L, N // bn),
        in_specs=[pl.BlockSpec((8, D), lambda l, j: (0, 0)),
                  pl.BlockSpec((1, D, bn), lambda l, j: (l, 0, j)),
                  pl.BlockSpec((1, 1, bn), lambda l, j: (l, 0, j))],
        out_specs=pl.BlockSpec((1, 8, bn), lambda l, j: (l, 0, j)),
        out_shape=jax.ShapeDtypeStruct((L, 8, N), F32),
        compiler_params=_params(("parallel", "parallel"), VMEM_BIG),
        name="adaln_mod",
    )(rows, w_mod, b_mod.reshape(L, 1, N))
    return out.reshape(L, 8, N_MOD, D)


def _inproj_kernel(x_ref, mod_ref, g_ref, w_ref, o_ref, hx_ref):
    @pl.when(pl.program_id(1) == 0)
    def _():
        m = mod_ref[0]
        hx_ref[...] = _norm_mod(x_ref[...], g_ref[...], m[0:1], m[1:2]).astype(BF16)

    o_ref[...] = _mm(hx_ref[...], w_ref[...])


def _in_proj(xs, mod_l, g1, wa, geom):
    R, D = xs.shape
    N = wa.shape[1]
    bm = _pick_block(512, geom["T"], geom["RC"])
    bn = 1024
    sel = geom["mod_sel"](bm)
    return pl.pallas_call(
        _inproj_kernel,
        grid=(R // bm, N // bn),
        in_specs=[pl.BlockSpec((bm, D), lambda i, j: (i, 0)),
                  pl.BlockSpec((1, N_MOD, D), lambda i, j: (sel(i), 0, 0)),
                  pl.BlockSpec((1, D), lambda i, j: (0, 0)),
                  pl.BlockSpec((D, bn), lambda i, j: (0, j))],
        out_specs=pl.BlockSpec((bm, bn), lambda i, j: (i, j)),
        out_shape=jax.ShapeDtypeStruct((R, N), F32),
        scratch_shapes=[pltpu.VMEM((bm, D), BF16)],
        compiler_params=_params(("parallel", "arbitrary"), VMEM_BIG),
        name="in_proj",
    )(xs, mod_l, g1.reshape(1, D), wa)


def _prep_kernel(gq_ref, cq_ref, ckv_ref, gk_ref, gv_ref, misc_ref, gmq_ref, gmkv_ref, wqb_ref, wkvb_ref,
                 ggq_ref, ggk_ref, cosg_ref, sing_ref, cosm_ref, sinm_ref,
                 qm_ref, km_ref, vm_ref, qg_ref, kg_ref, vg_ref):
    lane = lax.broadcasted_iota(jnp.int32, (1, LANE), 1)
    cosm, sinm = cosm_ref[...], sinm_ref[...]
    cosg, sing = cosg_ref[...], sing_ref[...]
    first_m = (lane & 31) < 16
    first_g = (lane & 63) < 32
    ones_col = jnp.where(lane == 0, 1.0, 0.0).astype(BF16) + jnp.zeros((cq_ref.shape[0], LANE), BF16)

    def rope_m(x):
        partner = jnp.where(first_m, pltpu.roll(x, LANE - 16, 1), pltpu.roll(x, 16, 1))
        return x * cosm + partner * sinm

    def rope_g(x):
        partner = jnp.where(first_g, pltpu.roll(x, LANE - 32, 1), pltpu.roll(x, 32, 1))
        return x * cosg + partner * sing

    qa = _mm(_rms(cq_ref[...], gmq_ref[...]).astype(BF16), wqb_ref[...])
    kva = _mm(_rms(ckv_ref[...], gmkv_ref[...]).astype(BF16), wkvb_ref[...])
    kpe = rope_m(jnp.where(lane < MLA_ROPE, misc_ref[...], 0.0)).astype(BF16)
    nh = MLA_HEADS * LANE
    for h in range(MLA_HEADS):
        hs = slice(h * LANE, (h + 1) * LANE)
        ps = slice(nh + h * LANE, nh + (h + 1) * LANE)
        qm_ref[h, :, 0:LANE] = qa[:, hs].astype(BF16)
        qm_ref[h, :, LANE:2 * LANE] = rope_m(qa[:, ps]).astype(BF16)
        km_ref[h, :, 0:LANE] = kva[:, hs].astype(BF16)
        km_ref[h, :, LANE:2 * LANE] = kpe
        vm_ref[h, :, 0:LANE] = kva[:, ps].astype(BF16)
        vm_ref[h, :, LANE:2 * LANE] = ones_col

    gq = gq_ref[...]
    ggq, ggk = ggq_ref[...], ggk_ref[...]
    for h in range(GQA_HEADS):
        qg_ref[h] = rope_g(_rms(gq[:, h * LANE:(h + 1) * LANE], ggq)).astype(BF16)
    gk, gv = gk_ref[...], gv_ref[...]
    for h in range(GQA_KV_HEADS):
        hs = slice(h * LANE, (h + 1) * LANE)
        kg_ref[h] = rope_g(_rms(gk[:, hs], ggk)).astype(BF16)
        vg_ref[h, :, 0:LANE] = gv[:, hs].astype(BF16)
        vg_ref[h, :, LANE:2 * LANE] = ones_col


def _attn_prep(P, gmq, gmkv, wqb, wkvb, ggq, ggk, tabs, geom):
    R = P.shape[0]
    bm = _pick_block(512, geom["T"], geom["RC"])
    nlat = geom["RL"] // bm
    per = geom["T"] // bm
    tsel = lambda i: jnp.where(i < nlat, i % per, per)
    col = lambda w, off: pl.BlockSpec((bm, w), lambda i: (i, off // w))
    full = lambda a: pl.BlockSpec(a.shape, lambda i: (0,) * a.ndim)
    tab = pl.BlockSpec((bm, LANE), lambda i: (tsel(i), 0))
    outs = [((MLA_HEADS, R, 2 * LANE), 2 * LANE), ((MLA_HEADS, R, 2 * LANE), 2 * LANE),
            ((MLA_HEADS, R, 2 * LANE), 2 * LANE), ((GQA_HEADS, R, LANE), LANE), ((GQA_KV_HEADS, R, LANE), LANE),
            ((GQA_KV_HEADS, R, 2 * LANE), 2 * LANE)]
    gmq, gmkv, ggq, ggk = (a.reshape(1, -1) for a in (gmq, gmkv, ggq, ggk))
    return pl.pallas_call(
        _prep_kernel,
        grid=(R // bm,),
        in_specs=[col(1024, OFF_GQ), col(512, OFF_CQ), col(512, OFF_CKV), col(256, OFF_GK), col(256, OFF_GV),
                  col(LANE, OFF_MISC), full(gmq), full(gmkv), full(wqb), full(wkvb), full(ggq), full(ggk),
                  tab, tab, tab, tab],
        out_specs=[pl.BlockSpec((s[0], bm, w), lambda i: (0, i, 0)) for s, w in outs],
        out_shape=[jax.ShapeDtypeStruct(s, BF16) for s, _ in outs],
        compiler_params=_params(("parallel",), VMEM_BIG),
        name="attn_prep",
    )(P, P, P, P, P, P, gmq, gmkv, wqb, wkvb, ggq, ggk, *tabs)


def _softmax_pv(q, kv_chunks, s_ref, o_ref, exp2_scale):
    sub = min(q.shape)
    for r0 in range(0, q.shape[0], sub):
        rows = slice(r0, r0 + sub)
        qr = q[rows]
        m_acc = None
        off = 0
        for k_ref, _, st, n in kv_chunks:
            s = _mm_nt(qr, k_ref[0, st:st + n, :])
            s_ref[rows, off:off + n] = s
            for t in range(n // LANE):
                tile = s[:, t * LANE:(t + 1) * LANE]
                m_acc = tile if m_acc is None else jnp.maximum(m_acc, tile)
            off += n
        m = jnp.max(m_acc, axis=-1, keepdims=True)
        acc = None
        off = 0
        for _, v_ref, st, n in kv_chunks:
            p = jnp.exp2((s_ref[rows, off:off + n] - m) * exp2_scale).astype(BF16)
            pv = _mm(p, v_ref[0, st:st + n, :])
            acc = pv if acc is None else acc + pv
            off += n
        o_ref[rows, :] = (acc[:, :LANE] / acc[:, LANE:LANE + 1]).astype(BF16)


def _attn_kernel(q_ref, kx_ref, kc_ref, vx_ref, vc_ref, o_ref, s_ref, *, exp2_scale, kchunk):
    chunks = [(kx_ref, vx_ref, i * kchunk, kchunk) for i in range(kx_ref.shape[1] // kchunk)]
    chunks.append((kc_ref, vc_ref, 0, kc_ref.shape[1]))
    _softmax_pv(q_ref[0], chunks, s_ref, o_ref, exp2_scale)


def _attn_ctx_kernel(q_ref, kc_ref, vc_ref, o_ref, s_ref, *, exp2_scale):
    _softmax_pv(q_ref[0], [(kc_ref, vc_ref, 0, kc_ref.shape[1])], s_ref, o_ref, exp2_scale)


def _attention(q, k, v, scale, geom, name):
    H, _, dk = q.shape
    Hk, _, dve = v.shape
    grp = H // Hk
    B, T, C, RL = geom["B"], geom["T"], geom["C"], geom["RL"]
    bq = _pick_block(1024, T)
    nq = T // bq
    cb = RL // C
    return pl.pallas_call(
        functools.partial(_attn_kernel, exp2_scale=scale * math.log2(math.e), kchunk=256),
        grid=(B, H, nq),
        in_specs=[pl.BlockSpec((1, bq, dk), lambda b, h, j: (h, b * nq + j, 0)),
                  pl.BlockSpec((1, T, dk), lambda b, h, j: (h // grp, b, 0)),
                  pl.BlockSpec((1, C, dk), lambda b, h, j: (h // grp, cb + b, 0)),
                  pl.BlockSpec((1, T, dve), lambda b, h, j: (h // grp, b, 0)),
                  pl.BlockSpec((1, C, dve), lambda b, h, j: (h // grp, cb + b, 0))],
        out_specs=pl.BlockSpec((bq, LANE), lambda b, h, j: (b * nq + j, h)),
        out_shape=jax.ShapeDtypeStruct((RL, H * LANE), BF16),
        scratch_shapes=[pltpu.VMEM((bq, T + C), F32)],
        compiler_params=_params(("parallel", "parallel", "parallel"), VMEM_BIG),
        name=name,
    )(q, k, k, v, v)


def _attention_ctx(q, k, v, scale, geom, name):
    H, _, dk = q.shape
    Hk, _, dve = v.shape
    grp = H // Hk
    B, C, RL, RC = geom["B"], geom["C"], geom["RL"], geom["RC"]
    cb = RL // C
    return pl.pallas_call(
        functools.partial(_attn_ctx_kernel, exp2_scale=scale * math.log2(math.e)),
        grid=(B, H),
        in_specs=[pl.BlockSpec((1, C, dk), lambda b, h: (h, cb + b, 0)),
                  pl.BlockSpec((1, C, dk), lambda b, h: (h // grp, cb + b, 0)),
                  pl.BlockSpec((1, C, dve), lambda b, h: (h // grp, cb + b, 0))],
        out_specs=pl.BlockSpec((C, LANE), lambda b, h: (b, h)),
        out_shape=jax.ShapeDtypeStruct((RC, H * LANE), BF16),
        scratch_shapes=[pltpu.VMEM((C, C), F32)],
        compiler_params=_params(("parallel", "parallel")),
        name=name,
    )(q, k, v)


def _gdn_a_kernel(cur_ref, prev_ref, next_ref, misc_ref, wc_ref, gvec_ref, qkvn_ref, gates_ref, ext_ref,
                  *, nlat, per_seq):
    r = pl.program_id(0)
    pos = r % per_seq
    is_ctx = r >= nlat
    pf = jnp.where(jnp.logical_or(is_ctx, pos == 0), 0.0, 1.0).astype(F32)
    nf = jnp.where(jnp.logical_or(is_ctx, pos == per_seq - 1), 0.0, 1.0).astype(F32)
    nb = GDN_BLOCK
    ext_ref[0:8, :] = prev_ref[...] * pf
    ext_ref[8:8 + nb, :] = cur_ref[...]
    ext_ref[8 + nb:16 + nb, :] = next_ref[...] * nf
    pad = GDN_CONV // 2
    n_qk = 2 * GDN_HEADS
    for c in range(3 * GDN_HEADS):
        cs = slice(c * LANE, (c + 1) * LANE)
        acc = None
        for j in range(GDN_CONV):
            t = ext_ref[8 - pad + j:8 - pad + j + nb, cs] * wc_ref[j:j + 1, cs]
            acc = t if acc is None else acc + t
        y = _silu(acc)
        if c < n_qk:
            y = y * lax.rsqrt(jnp.sum(y * y, axis=-1, keepdims=True) + EPS)
        qkvn_ref[:, cs] = y

    lane = lax.broadcasted_iota(jnp.int32, (1, LANE), 1)
    raw = misc_ref[...]
    beta = _sigmoid(raw)
    z = raw + gvec_ref[1:2, :]
    softplus = jnp.maximum(z, 0.0) + jnp.log1p(jnp.exp(-jnp.abs(z)))
    g = -jnp.exp(gvec_ref[0:1, :]) * softplus
    is_beta = jnp.logical_and(lane >= MISC_BETA, lane < MISC_DEC)
    is_g = jnp.logical_and(lane >= MISC_DEC, lane < MISC_DEC + 2 * GDN_HEADS)
    gates_ref[...] = jnp.where(is_beta, beta, jnp.where(is_g, g, 0.0))


def _gdn_conv(P, w_conv, a_log, dt_bias, geom):
    R = P.shape[0]
    nb = GDN_BLOCK
    W = 3 * GDN_HEADS * LANE
    nblk = R // nb
    sub = nb // 8
    gvec = jnp.zeros((2, LANE), F32)
    gvec = gvec.at[0, MISC_DEC:MISC_DEC + 2 * GDN_HEADS].set(a_log.reshape(-1))
    gvec = gvec.at[1, MISC_DEC:MISC_DEC + 2 * GDN_HEADS].set(dt_bias.reshape(-1))
    qc = OFF_QKV // W
    return pl.pallas_call(
        functools.partial(_gdn_a_kernel, nlat=geom["RL"] // nb, per_seq=geom["T"] // nb),
        grid=(nblk,),
        in_specs=[pl.BlockSpec((nb, W), lambda r: (r, qc)),
                  pl.BlockSpec((8, W), lambda r: (jnp.maximum(r * sub - 1, 0), qc)),
                  pl.BlockSpec((8, W), lambda r: (jnp.minimum((r + 1) * sub, R // 8 - 1), qc)),
                  pl.BlockSpec((nb, LANE), lambda r: (r, OFF_MISC // LANE)),
                  pl.BlockSpec((GDN_CONV, W), lambda r: (0, 0)),
                  pl.BlockSpec((2, LANE), lambda r: (0, 0))],
        out_specs=[pl.BlockSpec((nb, W), lambda r: (r, 0)),
                   pl.BlockSpec((nb, LANE), lambda r: (r, 0))],
        out_shape=[jax.ShapeDtypeStruct((R, W), F32), jax.ShapeDtypeStruct((R, LANE), F32)],
        scratch_shapes=[pltpu.VMEM((nb + 16, W), F32)],
        compiler_params=_params(("parallel",)),
        name="gdn_conv",
    )(P, P, P, P, w_conv, gvec)


def _split3(x):
    h = x.astype(BF16)
    r = x - h.astype(F32)
    m = r.astype(BF16)
    l = (r - m.astype(F32)).astype(BF16)
    return h, m, l


def _gdn_b_kernel(qkvn_ref, gates_ref, u_ref, w_ref, qg_ref, kgt_ref, qk_ref):
    nb = GDN_BLOCK
    nh = GDN_HEADS
    ri = lax.broadcasted_iota(jnp.int32, (nb, nb), 0)
    ci = lax.broadcasted_iota(jnp.int32, (nb, nb), 1)
    same = (ri >> 6) == (ci >> 6)
    low = jnp.logical_and(same, ri >= ci)
    upp = jnp.logical_and(same, ri <= ci)
    slow = jnp.logical_and(same, ri > ci)
    supp = jnp.logical_and(same, ri < ci)
    ltri = jnp.where(low, 1.0, 0.0).astype(BF16)
    utri = jnp.where(upp, 1.0, 0.0).astype(BF16)
    eye = jnp.where(ri == ci, 1.0, 0.0)
    pair = (ri >> 1) == (ci >> 1)
    offs = [jnp.logical_and((ri >> (lv + 1)) == (ci >> (lv + 1)), (ri >> lv) != (ci >> lv))
            for lv in range(1, int(math.log2(GDN_CHUNK)))]

    gt = gates_ref[...]
    g3 = _split3(gt)
    g3t = _split3(gt.T)
    cum_c = (sum(_mm(ltri, p) for p in reversed(g3)), sum(_mm(utri, p) for p in reversed(g3)))
    cum_r = (sum(_mm(p, utri) for p in reversed(g3t)), sum(_mm(p, ltri) for p in reversed(g3t)))
    tot_c = cum_c[0] + cum_c[1] - gt

    scale = GDN_DK ** -0.5
    insts = [(d, h) for d in range(2) for h in range(nh)]
    a_all, tinv_all, rhs_all = [], [], []
    for d, h in insts:
        mask, smask = (low, slow) if d == 0 else (upp, supp)
        ib = MISC_BETA + d * nh + h
        ig = MISC_DEC + d * nh + h
        beta = gt[:, ib:ib + 1]
        gc = cum_c[d][:, ig:ig + 1]
        gr = cum_r[d][ig:ig + 1, :]
        gl = tot_c[:, ig:ig + 1]
        q = qkvn_ref[:, h * LANE:(h + 1) * LANE]
        k = qkvn_ref[:, (nh + h) * LANE:(nh + h + 1) * LANE]
        v = qkvn_ref[:, (2 * nh + h) * LANE:(2 * nh + h + 1) * LANE]
        decay = jnp.where(mask, jnp.exp(jnp.where(mask, gc - gr, 0.0)), 0.0)
        kb = k * beta
        k16 = k.astype(BF16)
        a = jnp.where(smask, _mm_nt(kb.astype(BF16), k16) * decay, 0.0)
        a_all.append(a.astype(BF16))
        tinv_all.append(eye - jnp.where(pair, a, 0.0))
        rhs_all.append(jnp.concatenate([v * beta, kb * jnp.exp(gc)], axis=1).astype(BF16))
        hs = slice(h * LANE, (h + 1) * LANE)
        qs = q * scale
        qk_ref[d, :, h * nb:(h + 1) * nb] = (_mm_nt(qs.astype(BF16), k16) * decay).astype(BF16)
        qg_ref[d, :, hs] = (qs * jnp.exp(gc)).astype(BF16)
        kgt_ref[d, hs, :] = (k * jnp.exp(gl - gc)).T.astype(BF16)

    zero16 = jnp.zeros((nb, nb), BF16)
    for off in offs:
        for i in range(len(insts)):
            t16 = tinv_all[i].astype(BF16)
            ta = _mm(t16, jnp.where(off, a_all[i], zero16))
            tinv_all[i] = tinv_all[i] - _mm(ta.astype(BF16), t16)
    for i, (d, h) in enumerate(insts):
        x = _mm(tinv_all[i].astype(BF16), rhs_all[i])
        hs = slice(h * LANE, (h + 1) * LANE)
        u_ref[d, :, hs] = x[:, :LANE]
        w_ref[d, :, hs] = x[:, LANE:].astype(BF16)


def _gdn_local(qkvn, gates):
    R, W = qkvn.shape
    nb = GDN_BLOCK
    HW = GDN_HEADS * LANE
    return pl.pallas_call(
        _gdn_b_kernel,
        grid=(R // nb,),
        in_specs=[pl.BlockSpec((nb, W), lambda r: (r, 0)),
                  pl.BlockSpec((nb, LANE), lambda r: (r, 0))],
        out_specs=[pl.BlockSpec((2, nb, HW), lambda r: (0, r, 0)),
                   pl.BlockSpec((2, nb, HW), lambda r: (0, r, 0)),
                   pl.BlockSpec((2, nb, HW), lambda r: (0, r, 0)),
                   pl.BlockSpec((2, HW, nb), lambda r: (0, 0, r)),
                   pl.BlockSpec((2, nb, GDN_HEADS * nb), lambda r: (0, r, 0))],
        out_shape=[jax.ShapeDtypeStruct((2, R, HW), F32),
                   jax.ShapeDtypeStruct((2, R, HW), BF16),
                   jax.ShapeDtypeStruct((2, R, HW), BF16),
                   jax.ShapeDtypeStruct((2, HW, R), BF16),
                   jax.ShapeDtypeStruct((2, R, GDN_HEADS * nb), BF16)],
        compiler_params=_params(("parallel",), VMEM_BIG),
        name="gdn_local",
    )(qkvn, gates)


def _gdn_c_kernel(u_ref, w_ref, qg_ref, kgt_ref, qk_ref, gates_ref, o_ref, s_ref, vn_ref, *, d):
    nb = GDN_BLOCK
    nh = GDN_HEADS

    @pl.when(pl.program_id(1) == 0)
    def _():
        s_ref[...] = jnp.zeros_like(s_ref)
        vn_ref[...] = jnp.zeros_like(vn_ref)

    rowi = lax.broadcasted_iota(jnp.int32, (nb, 1), 0)
    nchunk = nb // GDN_CHUNK
    order = range(nchunk) if d == 0 else range(nchunk - 1, -1, -1)
    for c in order:
        rs = slice(c * GDN_CHUNK, (c + 1) * GDN_CHUNK)
        decay_all = jnp.exp(jnp.sum(gates_ref[rs, :], axis=0, keepdims=True))
        cmask = jnp.logical_and(rowi >= c * GDN_CHUNK, rowi < (c + 1) * GDN_CHUNK)
        for h in range(nh):
            hs = slice(h * LANE, (h + 1) * LANE)
            ig = MISC_DEC + d * nh + h
            s = s_ref[h]
            s16 = s.astype(BF16)
            v_new = u_ref[0, rs, hs] - _mm(w_ref[0, rs, hs], s16)
            vn_ref[h, rs, :] = v_new.astype(BF16)
            vn = vn_ref[h]
            o_ref[rs, hs] = _mm(qg_ref[0, rs, hs], s16) + _mm(qk_ref[0, rs, h * nb:(h + 1) * nb], vn)
            v_cur = jnp.where(cmask, vn, jnp.zeros_like(vn))
            s_ref[h] = s * decay_all[:, ig:ig + 1] + _mm(kgt_ref[0, hs, :], v_cur)


def _gdn_scan(u, w, qg, kgt, qk, gates, geom, d):
    nb = GDN_BLOCK
    HW = GDN_HEADS * LANE
    B, RL, R = geom["B"], geom["RL"], geom["R"]
    per = geom["T"] // nb
    nlat = RL // nb

    def blk(b, j):
        lat = b * per + (j - 1 if d == 0 else per - j)
        return jnp.where(j == 0, nlat + b, lat)

    tok = lambda w_: pl.BlockSpec((1, nb, w_), lambda b, j: (d, blk(b, j), 0))
    return pl.pallas_call(
        functools.partial(_gdn_c_kernel, d=d),
        grid=(B, per + 1),
        in_specs=[tok(HW), tok(HW), tok(HW),
                  pl.BlockSpec((1, HW, nb), lambda b, j: (d, 0, blk(b, j))),
                  tok(GDN_HEADS * nb),
                  pl.BlockSpec((nb, LANE), lambda b, j: (blk(b, j), 0))],
        out_specs=pl.BlockSpec((nb, HW), lambda b, j: (blk(b, j), 0)),
        out_shape=jax.ShapeDtypeStruct((R, HW), F32),
        scratch_shapes=[pltpu.VMEM((GDN_HEADS, GDN_DK, GDN_DV), F32),
                        pltpu.VMEM((GDN_HEADS, nb, GDN_DV), BF16)],
        compiler_params=_params(("parallel", "arbitrary")),
        name="gdn_scan_fwd" if d == 0 else "gdn_scan_bwd",
    )(u, w, qg, kgt, qk, gates)


def _gdn_out_kernel(of_ref, ob_ref, og_ref, g_ref, o_ref):
    o = of_ref[...] + ob_ref[...]
    gate = _silu(og_ref[...])
    g = g_ref[...]
    for h in range(GDN_HEADS):
        hs = slice(h * LANE, (h + 1) * LANE)
        o_ref[:, hs] = (_rms(o[:, hs], g) * gate[:, hs]).astype(BF16)


def _gdn_out(o_f, o_b, P, g_out, rows):
    HW = GDN_HEADS * LANE
    bm = 512 if rows % 512 == 0 else 256
    return pl.pallas_call(
        _gdn_out_kernel,
        grid=(rows // bm,),
        in_specs=[pl.BlockSpec((bm, HW), lambda i: (i, 0)),
                  pl.BlockSpec((bm, HW), lambda i: (i, 0)),
                  pl.BlockSpec((bm, HW), lambda i: (i, OFF_OG // HW)),
                  pl.BlockSpec((1, LANE), lambda i: (0, 0))],
        out_specs=pl.BlockSpec((bm, HW), lambda i: (i, 0)),
        out_shape=jax.ShapeDtypeStruct((rows, HW), BF16),
        compiler_params=_params(("parallel",)),
        name="gdn_out",
    )(o_f, o_b, P, g_out.reshape(1, LANE))


def _merge_kernel(x_ref, mod_ref, g_ref, oa_ref, ob_ref, oc_ref, wg_ref, wua_ref, wub_ref, wuc_ref, wo_ref,
                  o_ref, hx_ref, acc_ref):
    j = pl.program_id(1)
    m = mod_ref[0]

    @pl.when(j == 0)
    def _():
        hx_ref[...] = _norm_mod(x_ref[...], g_ref[...], m[0:1], m[1:2]).astype(BF16)
        acc_ref[...] = jnp.zeros_like(acc_ref)

    hx = hx_ref[...]
    y = (_sigmoid(_mm(hx, wg_ref[0])) * _mm(oa_ref[...], wua_ref[...])
         + _sigmoid(_mm(hx, wg_ref[1])) * _mm(ob_ref[...], wub_ref[...])
         + _sigmoid(_mm(hx, wg_ref[2])) * _mm(oc_ref[...], wuc_ref[...]))
    acc_ref[...] += _mm(y.astype(BF16), wo_ref[...])

    @pl.when(j == pl.num_programs(1) - 1)
    def _():
        o_ref[...] = x_ref[...] + m[2:3] * acc_ref[...]


def _merge(xs, mod_l, g1, oa, ob, oc, wg, wua, wub, wuc, wo, geom, rows):
    D = xs.shape[1]
    bm = _pick_block(512, geom["T"], geom["RC"])
    hc = 256
    sel = geom["mod_sel"](bm)
    rowb = lambda w_: pl.BlockSpec((bm, w_), lambda i, j: (i, 0))
    return pl.pallas_call(
        _merge_kernel,
        grid=(rows // bm, D // hc),
        in_specs=[rowb(D),
                  pl.BlockSpec((1, N_MOD, D), lambda i, j: (sel(i), 0, 0)),
                  pl.BlockSpec((1, D), lambda i, j: (0, 0)),
                  rowb(oa.shape[1]), rowb(ob.shape[1]), rowb(oc.shape[1]),
                  pl.BlockSpec((3, D, hc), lambda i, j: (0, 0, j)),
                  pl.BlockSpec((wua.shape[0], hc), lambda i, j: (0, j)),
                  pl.BlockSpec((wub.shape[0], hc), lambda i, j: (0, j)),
                  pl.BlockSpec((wuc.shape[0], hc), lambda i, j: (0, j)),
                  pl.BlockSpec((hc, D), lambda i, j: (j, 0))],
        out_specs=rowb(D),
        out_shape=jax.ShapeDtypeStruct((rows, D), F32),
        scratch_shapes=[pltpu.VMEM((bm, D), BF16), pltpu.VMEM((bm, D), F32)],
        compiler_params=_params(("parallel", "arbitrary"), VMEM_BIG),
        name="merge",
    )(xs, mod_l, g1.reshape(1, D), oa, ob, oc, wg, wua, wub, wuc, wo)


def _mlp_kernel(x_ref, mod_ref, g_ref, w1_ref, w2_ref, gf_ref, o_ref, hx_ref, acc_ref, *, final):
    j = pl.program_id(1)
    m = mod_ref[0]

    @pl.when(j == 0)
    def _():
        hx_ref[...] = _norm_mod(x_ref[...], g_ref[...], m[3:4], m[4:5]).astype(BF16)
        acc_ref[...] = jnp.zeros_like(acc_ref)

    h = jnp.maximum(_mm(hx_ref[...], w1_ref[...]), 0.0)
    acc_ref[...] += _mm((h * h).astype(BF16), w2_ref[...])

    @pl.when(j == pl.num_programs(1) - 1)
    def _():
        x2 = x_ref[...] + m[5:6] * acc_ref[...]
        o_ref[...] = _rms(x2, gf_ref[...]) if final else x2


def _mlp(x1, mod_l, g2, w1, w2, g_final, geom, final):
    rows, D = x1.shape
    F = w1.shape[1]
    bm = _pick_block(512, geom["T"], geom["RC"])
    fc = 1024
    sel = geom["mod_sel"](bm)
    return pl.pallas_call(
        functools.partial(_mlp_kernel, final=final),
        grid=(rows // bm, F // fc),
        in_specs=[pl.BlockSpec((bm, D), lambda i, j: (i, 0)),
                  pl.BlockSpec((1, N_MOD, D), lambda i, j: (sel(i), 0, 0)),
                  pl.BlockSpec((1, D), lambda i, j: (0, 0)),
                  pl.BlockSpec((D, fc), lambda i, j: (0, j)),
                  pl.BlockSpec((fc, D), lambda i, j: (j, 0)),
                  pl.BlockSpec((1, D), lambda i, j: (0, 0))],
        out_specs=pl.BlockSpec((bm, D), lambda i, j: (i, 0)),
        out_shape=jax.ShapeDtypeStruct((rows, D), F32),
        scratch_shapes=[pltpu.VMEM((bm, D), BF16), pltpu.VMEM((bm, D), F32)],
        compiler_params=_params(("parallel", "arbitrary"), VMEM_BIG),
        name="mlp",
    )(x1, mod_l, g2.reshape(1, D), w1, w2, g_final.reshape(1, D))


def _rope_tables(T, bm):
    t = np.arange(T)
    row, col = t // GRID_W, t % GRID_W

    def table(n_rot):
        n = n_rot // 2
        half = n // 2
        inv = ROPE_THETA ** (-np.arange(half, dtype=np.float64) / half)
        cos = np.zeros((T + bm, LANE))
        sin = np.zeros((T + bm, LANE))
        for s, pos in enumerate((row, col)):
            ang = pos[:, None].astype(np.float32).astype(np.float64) * inv.astype(np.float32)[None, :]
            ang = ang.astype(np.float32).astype(np.float64)
            c, sn = np.cos(ang), np.sin(ang)
            cos[:T, s * n:s * n + half] = c
            cos[:T, s * n + half:(s + 1) * n] = c
            sin[:T, s * n:s * n + half] = -sn
            sin[:T, s * n + half:(s + 1) * n] = sn
        cos[T:, :] = 1.0
        return jnp.asarray(cos, F32), jnp.asarray(sin, F32)

    cg, sg = table(GQA_HD)
    cm, sm = table(MLA_ROPE)
    return cg, sg, cm, sm


def _prep_layer_weights(w_in, w_qb, w_kvb):
    D = w_in.shape[0]
    splits = (MLA_Q_RANK, MLA_KV_RANK, MLA_ROPE, GQA_HEADS * GQA_HD, GQA_KV_HEADS * GQA_HD, GQA_KV_HEADS * GQA_HD,
              GDN_HEADS * (2 * GDN_DK + GDN_DV), 2 * GDN_HEADS, 2 * GDN_HEADS, GDN_HEADS * GDN_DV, 3 * D)
    off = np.cumsum((0,) + splits)
    cq, ckv, kpe, gq, gk, gv, qkv, beta, dec, og, bg = (w_in[:, off[i]:off[i + 1]] for i in range(len(splits)))
    used = OFF_MISC + MLA_ROPE + 4 * GDN_HEADS
    wa = jnp.concatenate([gq, cq, ckv, gk, gv, og, qkv, kpe, beta, dec, jnp.zeros((D, N_SMALL - used), F32)],
                         axis=1).astype(BF16)
    wg = bg.reshape(D, 3, D).transpose(1, 0, 2).astype(BF16)
    qb = w_qb.reshape(-1, MLA_HEADS, MLA_NOPE + MLA_ROPE)
    qpe = jnp.pad(qb[:, :, MLA_NOPE:], ((0, 0), (0, 0), (0, LANE - MLA_ROPE)))
    wqb = jnp.concatenate([qb[:, :, :MLA_NOPE].reshape(-1, MLA_HEADS * LANE),
                           qpe.reshape(-1, MLA_HEADS * LANE)], axis=1).astype(BF16)
    kvb = w_kvb.reshape(-1, MLA_HEADS, MLA_NOPE + MLA_V)
    wkvb = jnp.concatenate([kvb[:, :, :MLA_NOPE].reshape(-1, MLA_HEADS * LANE),
                            kvb[:, :, MLA_NOPE:].reshape(-1, MLA_HEADS * LANE)], axis=1).astype(BF16)
    return wa, wg, wqb, wkvb


def kernel(x, c, ctx, c_ctx, w_mod, b_mod, g_norm1, w_in, g_mla_q, w_mla_qb, g_mla_kv, w_mla_kvb, g_gqa_q, g_gqa_k,
           w_conv, a_log, dt_bias, g_gdn_out, w_up_a, w_up_b, w_up_c, w_out, g_norm2, w_ff1, w_ff2, g_final):
    B, T, D = x.shape
    C = ctx.shape[1]
    L = w_mod.shape[0]
    assert C == GDN_BLOCK and T % GDN_BLOCK == 0 and T % GRID_W == 0 and B < 8
    RL, RC = B * T, B * C
    geom = dict(B=B, T=T, C=C, RL=RL, RC=RC, R=RL + RC)
    geom["mod_sel"] = lambda bm: (lambda i: jnp.where(i < RL // bm, i // (T // bm), B))

    mod = _modulation(c, c_ctx, w_mod, b_mod)
    tabs = _rope_tables(T, _pick_block(512, T, RC))
    xs = jnp.concatenate([x.reshape(RL, D), ctx.reshape(RC, D)], axis=0)
    mla_scale = (MLA_NOPE + MLA_ROPE) ** -0.5
    gqa_scale = GQA_HD ** -0.5

    for l in range(L):
        last = l == L - 1
        wa, wg, wqb, wkvb = _prep_layer_weights(w_in[l], w_mla_qb[l], w_mla_kvb[l])
        P = _in_proj(xs, mod[l], g_norm1[l], wa, geom)
        qm, km, vm, qg, kg, vg = _attn_prep(P, g_mla_q[l], g_mla_kv[l], wqb, wkvb, g_gqa_q[l], g_gqa_k[l], tabs, geom)
        oa = _attention(qm, km, vm, mla_scale, geom, "attn_mla")
        ob = _attention(qg, kg, vg, gqa_scale, geom, "attn_gqa")
        qkvn, gates = _gdn_conv(P, w_conv[l], a_log[l], dt_bias[l], geom)
        u, w, qgd, kgt, qk = _gdn_local(qkvn, gates)
        o_f = _gdn_scan(u, w, qgd, kgt, qk, gates, geom, 0)
        o_b = _gdn_scan(u, w, qgd, kgt, qk, gates, geom, 1)
        rows = RL if last else RL + RC
        oc = _gdn_out(o_f, o_b, P, g_gdn_out[l], rows)
        if not last:
            oa = jnp.concatenate([oa, _attention_ctx(qm, km, vm, mla_scale, geom, "attn_mla_ctx")], axis=0)
            ob = jnp.concatenate([ob, _attention_ctx(qg, kg, vg, gqa_scale, geom, "attn_gqa_ctx")], axis=0)
        x1 = _merge(xs, mod[l], g_norm1[l], oa, ob, oc, wg, w_up_a[l].astype(BF16), w_up_b[l].astype(BF16),
                    w_up_c[l].astype(BF16), w_out[l].astype(BF16), geom, rows)
        xs = _mlp(x1, mod[l], g_norm2[l], w_ff1[l].astype(BF16), w_ff2[l].astype(BF16), g_final, geom, last)
    return xs.reshape(B, T, D)
```

```python
import functools
import math

import numpy as np
import jax
import jax.numpy as jnp
from jax import lax
from jax.experimental import pallas as pl
from jax.experimental.pallas import tpu as pltpu

F32 = jnp.float32
BF16 = jnp.bfloat16

GRID_W = 64
EPS = 1e-6
ROPE_THETA = 10000.0
MLA_HEADS, MLA_Q_RANK, MLA_KV_RANK, MLA_NOPE, MLA_ROPE, MLA_V = 4, 512, 512, 128, 64, 128
GQA_HEADS, GQA_KV_HEADS, GQA_HD = 8, 2, 128
GDN_HEADS, GDN_DK, GDN_DV, GDN_CONV, GDN_CHUNK = 4, 128, 128, 5, 64
N_MOD = 6
LANE = 128
GDN_BLOCK = 256
HALO_ROWS = 16
VMEM_BIG = 56 * 1024 * 1024

OFF_GQ, OFF_CQ, OFF_CKV, OFF_GK, OFF_GV, OFF_OG, OFF_QKV, OFF_MISC = 0, 1024, 1536, 2048, 2304, 2560, 3072, 4608
N_SMALL = 5120
MISC_BETA, MISC_DEC = 64, 72


def _mm(a, b):
    return jnp.dot(a, b, preferred_element_type=F32)


def _mm_nt(a, b):
    return lax.dot_general(a, b, (((1,), (1,)), ((), ())), preferred_element_type=F32)


def _silu(x):
    return x / (1.0 + jnp.exp(-x))


def _sigmoid(x):
    return 1.0 / (1.0 + jnp.exp(-x))


def _rms(x, g):
    return x * lax.rsqrt(jnp.mean(x * x, axis=-1, keepdims=True) + EPS) * g


def _norm_mod(x, g, shift, scale):
    return _rms(x, g) * (1.0 + scale) + shift


def _params(sem, vmem=None):
    kw = dict(dimension_semantics=sem)
    if vmem is not None:
        kw["vmem_limit_bytes"] = vmem
    return pltpu.CompilerParams(**kw)


def _pick_block(pref, *sizes):
    b = pref
    while any(s % b for s in sizes):
        b //= 2
    return b


def _mod_kernel(s_ref, w_ref, b_ref, o_ref):
    s = _silu(s_ref[...])
    o_ref[0] = _mm(s.astype(BF16), w_ref[0].astype(BF16)) + b_ref[0]


def _modulation(c, c_ctx, w_mod, b_mod):
    L, D, N = w_mod.shape
    B = c.shape[0]
    rows = jnp.zeros((8, D), F32).at[:B].set(c).at[B].set(c_ctx)
    bn = 1024
    out = pl.pallas_call(
        _mod_kernel,
        grid=(L, N // bn),
        in_specs=[pl.BlockSpec((8, D), lambda l, j: (0, 0)),
                  pl.BlockSpec((1, D, bn), lambda l, j: (l, 0, j)),
                  pl.BlockSpec((1, 1, bn), lambda l, j: (l, 0, j))],
        out_specs=pl.BlockSpec((1, 8, bn), lambda l, j: (l, 0, j)),
        out_shape=jax.ShapeDtypeStruct((L, 8, N), F32),
        compiler_params=_params(("parallel", "parallel"), VMEM_BIG),
        name="adaln_mod",
    )(rows, w_mod, b_mod.reshape(L, 1, N))
    return out.reshape(L, 8, N_MOD, D)


NORM_ROWS = 64


def _norm_mod_rows(x_ref, g, shift, scale, hx_ref):
    def body(r, carry):
        rs = pl.ds(pl.multiple_of(r * NORM_ROWS, NORM_ROWS), NORM_ROWS)
        hx_ref[rs, :] = _norm_mod(x_ref[rs, :], g, shift, scale).astype(BF16)
        return carry

    lax.fori_loop(0, x_ref.shape[0] // NORM_ROWS, body, 0)


def _inproj_kernel(x_ref, mod_ref, g_ref, w_ref, o_ref, hx_ref):
    @pl.when(pl.program_id(1) == 0)
    def _():
        m = mod_ref[0]
        _norm_mod_rows(x_ref, g_ref[...], m[0:1], m[1:2], hx_ref)

    o_ref[...] = _mm(hx_ref[...], w_ref[...]).astype(BF16)


def _in_proj(xs, mod_l, g1, wa, geom):
    R, D = xs.shape
    N = wa.shape[1]
    bm = _pick_block(1024, geom["T"], geom["RC"])
    bn = 1024
    sel = geom["mod_sel"](bm)
    return pl.pallas_call(
        _inproj_kernel,
        grid=(R // bm, N // bn),
        in_specs=[pl.BlockSpec((bm, D), lambda i, j: (i, 0)),
                  pl.BlockSpec((1, N_MOD, D), lambda i, j: (sel(i), 0, 0)),
                  pl.BlockSpec((1, D), lambda i, j: (0, 0)),
                  pl.BlockSpec((D, bn), lambda i, j: (0, j))],
        out_specs=pl.BlockSpec((bm, bn), lambda i, j: (i, j)),
        out_shape=jax.ShapeDtypeStruct((R, N), BF16),
        scratch_shapes=[pltpu.VMEM((bm, D), BF16)],
        compiler_params=_params(("parallel", "arbitrary"), VMEM_BIG),
        name="in_proj",
    )(xs, mod_l, g1.reshape(1, D), wa)


def _prep_kernel(gq_ref, cq_ref, ckv_ref, gk_ref, gv_ref, misc_ref, gmq_ref, gmkv_ref, wqb_ref, wkvb_ref,
                 ggq_ref, ggk_ref, cosg_ref, sing_ref, cosm_ref, sinm_ref,
                 qm_ref, km_ref, vm_ref, qg_ref, kg_ref, vg_ref):
    lane = lax.broadcasted_iota(jnp.int32, (1, LANE), 1)
    cosm, sinm = cosm_ref[...], sinm_ref[...]
    cosg, sing = cosg_ref[...], sing_ref[...]
    first_m = (lane & 31) < 16
    first_g = (lane & 63) < 32
    ones_col = jnp.where(lane == 0, 1.0, 0.0).astype(BF16) + jnp.zeros((cq_ref.shape[0], LANE), BF16)

    def rope_m(x):
        partner = jnp.where(first_m, pltpu.roll(x, LANE - 16, 1), pltpu.roll(x, 16, 1))
        return x * cosm + partner * sinm

    def rope_g(x):
        partner = jnp.where(first_g, pltpu.roll(x, LANE - 32, 1), pltpu.roll(x, 32, 1))
        return x * cosg + partner * sing

    qa = _mm(_rms(cq_ref[...].astype(F32), gmq_ref[...]).astype(BF16), wqb_ref[...])
    kva = _mm(_rms(ckv_ref[...].astype(F32), gmkv_ref[...]).astype(BF16), wkvb_ref[...])
    kpe = rope_m(jnp.where(lane < MLA_ROPE, misc_ref[...].astype(F32), 0.0)).astype(BF16)
    nh = MLA_HEADS * LANE
    for h in range(MLA_HEADS):
        hs = slice(h * LANE, (h + 1) * LANE)
        ps = slice(nh + h * LANE, nh + (h + 1) * LANE)
        qm_ref[h, :, 0:LANE] = qa[:, hs].astype(BF16)
        qm_ref[h, :, LANE:2 * LANE] = rope_m(qa[:, ps]).astype(BF16)
        km_ref[h, :, 0:LANE] = kva[:, hs].astype(BF16)
        km_ref[h, :, LANE:2 * LANE] = kpe
        vm_ref[h, :, 0:LANE] = kva[:, ps].astype(BF16)
        vm_ref[h, :, LANE:2 * LANE] = ones_col

    ggq, ggk = ggq_ref[...], ggk_ref[...]
    for h in range(GQA_HEADS):
        hs = slice(h * LANE, (h + 1) * LANE)
        qg_ref[h] = rope_g(_rms(gq_ref[:, hs].astype(F32), ggq)).astype(BF16)
    for h in range(GQA_KV_HEADS):
        hs = slice(h * LANE, (h + 1) * LANE)
        kg_ref[h] = rope_g(_rms(gk_ref[:, hs].astype(F32), ggk)).astype(BF16)
        vg_ref[h, :, 0:LANE] = gv_ref[:, hs]
        vg_ref[h, :, LANE:2 * LANE] = ones_col


def _attn_prep(P, gmq, gmkv, wqb, wkvb, ggq, ggk, tabs, geom):
    R = P.shape[0]
    bm = _pick_block(512, geom["T"], geom["RC"])
    nlat = geom["RL"] // bm
    per = geom["T"] // bm
    tsel = lambda i: jnp.where(i < nlat, i % per, per)
    col = lambda w, off: pl.BlockSpec((bm, w), lambda i: (i, off // w))
    full = lambda a: pl.BlockSpec(a.shape, lambda i: (0,) * a.ndim)
    tab = pl.BlockSpec((bm, LANE), lambda i: (tsel(i), 0))
    outs = [((MLA_HEADS, R, 2 * LANE), 2 * LANE), ((MLA_HEADS, R, 2 * LANE), 2 * LANE),
            ((MLA_HEADS, R, 2 * LANE), 2 * LANE), ((GQA_HEADS, R, LANE), LANE), ((GQA_KV_HEADS, R, LANE), LANE),
            ((GQA_KV_HEADS, R, 2 * LANE), 2 * LANE)]
    gmq, gmkv, ggq, ggk = (a.reshape(1, -1) for a in (gmq, gmkv, ggq, ggk))
    return pl.pallas_call(
        _prep_kernel,
        grid=(R // bm,),
        in_specs=[col(1024, OFF_GQ), col(512, OFF_CQ), col(512, OFF_CKV), col(256, OFF_GK), col(256, OFF_GV),
                  col(LANE, OFF_MISC), full(gmq), full(gmkv), full(wqb), full(wkvb), full(ggq), full(ggk),
                  tab, tab, tab, tab],
        out_specs=[pl.BlockSpec((s[0], bm, w), lambda i: (0, i, 0)) for s, w in outs],
        out_shape=[jax.ShapeDtypeStruct(s, BF16) for s, _ in outs],
        compiler_params=_params(("parallel",), VMEM_BIG),
        name="attn_prep",
    )(P, P, P, P, P, P, gmq, gmkv, wqb, wkvb, ggq, ggk, *tabs)


def _softmax_pv(q, kv_chunks, s_ref, o_ref, exp2_scale):
    sub = min(q.shape)
    for r0 in range(0, q.shape[0], sub):
        rows = slice(r0, r0 + sub)
        qr = q[rows]
        m_acc = None
        off = 0
        for k_ref, _, st, n in kv_chunks:
            s = _mm_nt(qr, k_ref[0, st:st + n, :])
            s_ref[rows, off:off + n] = s
            for t in range(n // LANE):
                tile = s[:, t * LANE:(t + 1) * LANE]
                m_acc = tile if m_acc is None else jnp.maximum(m_acc, tile)
            off += n
        m = jnp.max(m_acc, axis=-1, keepdims=True)
        acc = None
        off = 0
        for _, v_ref, st, n in kv_chunks:
            p = jnp.exp2((s_ref[rows, off:off + n] - m) * exp2_scale).astype(BF16)
            pv = _mm(p, v_ref[0, st:st + n, :])
            acc = pv if acc is None else acc + pv
            off += n
        o_ref[rows, :] = (acc[:, :LANE] / acc[:, LANE:LANE + 1]).astype(BF16)


def _attn_kernel(q_ref, kx_ref, kc_ref, vx_ref, vc_ref, o_ref, s_ref, *, exp2_scale, kchunk):
    chunks = [(kx_ref, vx_ref, i * kchunk, kchunk) for i in range(kx_ref.shape[1] // kchunk)]
    chunks.append((kc_ref, vc_ref, 0, kc_ref.shape[1]))
    _softmax_pv(q_ref[0], chunks, s_ref, o_ref, exp2_scale)


def _attn_ctx_kernel(q_ref, kc_ref, vc_ref, o_ref, s_ref, *, exp2_scale):
    _softmax_pv(q_ref[0], [(kc_ref, vc_ref, 0, kc_ref.shape[1])], s_ref, o_ref, exp2_scale)


def _attention(q, k, v, scale, geom, name):
    H, _, dk = q.shape
    Hk, _, dve = v.shape
    grp = H // Hk
    B, T, C, RL = geom["B"], geom["T"], geom["C"], geom["RL"]
    bq = _pick_block(1024, T)
    nq = T // bq
    cb = RL // C
    return pl.pallas_call(
        functools.partial(_attn_kernel, exp2_scale=scale * math.log2(math.e), kchunk=256),
        grid=(B, H, nq),
        in_specs=[pl.BlockSpec((1, bq, dk), lambda b, h, j: (h, b * nq + j, 0)),
                  pl.BlockSpec((1, T, dk), lambda b, h, j: (h // grp, b, 0)),
                  pl.BlockSpec((1, C, dk), lambda b, h, j: (h // grp, cb + b, 0)),
                  pl.BlockSpec((1, T, dve), lambda b, h, j: (h // grp, b, 0)),
                  pl.BlockSpec((1, C, dve), lambda b, h, j: (h // grp, cb + b, 0))],
        out_specs=pl.BlockSpec((bq, LANE), lambda b, h, j: (b * nq + j, h)),
        out_shape=jax.ShapeDtypeStruct((RL, H * LANE), BF16),
        scratch_shapes=[pltpu.VMEM((bq, T + C), F32)],
        compiler_params=_params(("parallel", "parallel", "parallel"), VMEM_BIG),
        name=name,
    )(q, k, k, v, v)


def _attention_ctx(q, k, v, scale, geom, name):
    H, _, dk = q.shape
    Hk, _, dve = v.shape
    grp = H // Hk
    B, C, RL, RC = geom["B"], geom["C"], geom["RL"], geom["RC"]
    cb = RL // C
    return pl.pallas_call(
        functools.partial(_attn_ctx_kernel, exp2_scale=scale * math.log2(math.e)),
        grid=(B, H),
        in_specs=[pl.BlockSpec((1, C, dk), lambda b, h: (h, cb + b, 0)),
                  pl.BlockSpec((1, C, dk), lambda b, h: (h // grp, cb + b, 0)),
                  pl.BlockSpec((1, C, dve), lambda b, h: (h // grp, cb + b, 0))],
        out_specs=pl.BlockSpec((C, LANE), lambda b, h: (b, h)),
        out_shape=jax.ShapeDtypeStruct((RC, H * LANE), BF16),
        scratch_shapes=[pltpu.VMEM((C, C), F32)],
        compiler_params=_params(("parallel", "parallel")),
        name=name,
    )(q, k, v)


def _gdn_a_kernel(cur_ref, prev_ref, next_ref, misc_ref, wc_ref, gvec_ref, qkvn_ref, gates_ref, ext_ref,
                  *, nlat, per_seq):
    r = pl.program_id(0)
    pos = r % per_seq
    is_ctx = r >= nlat
    pf = jnp.where(jnp.logical_or(is_ctx, pos == 0), 0.0, 1.0).astype(F32)
    nf = jnp.where(jnp.logical_or(is_ctx, pos == per_seq - 1), 0.0, 1.0).astype(F32)
    nb = GDN_BLOCK
    hal = HALO_ROWS
    ext_ref[0:hal, :] = prev_ref[...].astype(F32) * pf
    ext_ref[hal:hal + nb, :] = cur_ref[...].astype(F32)
    ext_ref[hal + nb:2 * hal + nb, :] = next_ref[...].astype(F32) * nf
    pad = GDN_CONV // 2
    n_qk = 2 * GDN_HEADS
    for c in range(3 * GDN_HEADS):
        cs = slice(c * LANE, (c + 1) * LANE)
        acc = None
        for j in range(GDN_CONV):
            t = ext_ref[hal - pad + j:hal - pad + j + nb, cs] * wc_ref[j:j + 1, cs]
            acc = t if acc is None else acc + t
        y = _silu(acc)
        if c < n_qk:
            y = y * lax.rsqrt(jnp.sum(y * y, axis=-1, keepdims=True) + EPS)
        qkvn_ref[:, cs] = y

    lane = lax.broadcasted_iota(jnp.int32, (1, LANE), 1)
    raw = misc_ref[...].astype(F32)
    beta = _sigmoid(raw)
    z = raw + gvec_ref[1:2, :]
    softplus = jnp.maximum(z, 0.0) + jnp.log1p(jnp.exp(-jnp.abs(z)))
    g = -jnp.exp(gvec_ref[0:1, :]) * softplus
    is_beta = jnp.logical_and(lane >= MISC_BETA, lane < MISC_DEC)
    is_g = jnp.logical_and(lane >= MISC_DEC, lane < MISC_DEC + 2 * GDN_HEADS)
    gates_ref[...] = jnp.where(is_beta, beta, jnp.where(is_g, g, 0.0))


def _gdn_conv(P, w_conv, a_log, dt_bias, geom):
    R = P.shape[0]
    nb = GDN_BLOCK
    W = 3 * GDN_HEADS * LANE
    nblk = R // nb
    hal = HALO_ROWS
    sub = nb // hal
    gvec = jnp.zeros((2, LANE), F32)
    gvec = gvec.at[0, MISC_DEC:MISC_DEC + 2 * GDN_HEADS].set(a_log.reshape(-1))
    gvec = gvec.at[1, MISC_DEC:MISC_DEC + 2 * GDN_HEADS].set(dt_bias.reshape(-1))
    qc = OFF_QKV // W
    return pl.pallas_call(
        functools.partial(_gdn_a_kernel, nlat=geom["RL"] // nb, per_seq=geom["T"] // nb),
        grid=(nblk,),
        in_specs=[pl.BlockSpec((nb, W), lambda r: (r, qc)),
                  pl.BlockSpec((hal, W), lambda r: (jnp.maximum(r * sub - 1, 0), qc)),
                  pl.BlockSpec((hal, W), lambda r: (jnp.minimum((r + 1) * sub, R // hal - 1), qc)),
                  pl.BlockSpec((nb, LANE), lambda r: (r, OFF_MISC // LANE)),
                  pl.BlockSpec((GDN_CONV, W), lambda r: (0, 0)),
                  pl.BlockSpec((2, LANE), lambda r: (0, 0))],
        out_specs=[pl.BlockSpec((nb, W), lambda r: (r, 0)),
                   pl.BlockSpec((nb, LANE), lambda r: (r, 0))],
        out_shape=[jax.ShapeDtypeStruct((R, W), F32), jax.ShapeDtypeStruct((R, LANE), F32)],
        scratch_shapes=[pltpu.VMEM((nb + 2 * hal, W), F32)],
        compiler_params=_params(("parallel",)),
        name="gdn_conv",
    )(P, P, P, P, w_conv, gvec)


def _split3(x):
    h = x.astype(BF16)
    r = x - h.astype(F32)
    m = r.astype(BF16)
    l = (r - m.astype(F32)).astype(BF16)
    return h, m, l


def _gdn_b_kernel(qkvn_ref, gates_ref, u_ref, w_ref, qg_ref, kgt_ref, qk_ref):
    nb = GDN_BLOCK
    nh = GDN_HEADS
    ri = lax.broadcasted_iota(jnp.int32, (nb, nb), 0)
    ci = lax.broadcasted_iota(jnp.int32, (nb, nb), 1)
    same = (ri >> 6) == (ci >> 6)
    low = jnp.logical_and(same, ri >= ci)
    upp = jnp.logical_and(same, ri <= ci)
    slow = jnp.logical_and(same, ri > ci)
    supp = jnp.logical_and(same, ri < ci)
    ltri = jnp.where(low, 1.0, 0.0).astype(BF16)
    utri = jnp.where(upp, 1.0, 0.0).astype(BF16)
    eye = jnp.where(ri == ci, 1.0, 0.0)
    pair = (ri >> 1) == (ci >> 1)
    offs = [jnp.logical_and((ri >> (lv + 1)) == (ci >> (lv + 1)), (ri >> lv) != (ci >> lv))
            for lv in range(1, int(math.log2(GDN_CHUNK)))]

    gt = gates_ref[...]
    g3 = _split3(gt)
    g3t = _split3(gt.T)
    cum_c = (sum(_mm(ltri, p) for p in reversed(g3)), sum(_mm(utri, p) for p in reversed(g3)))
    cum_r = (sum(_mm(p, utri) for p in reversed(g3t)), sum(_mm(p, ltri) for p in reversed(g3t)))
    tot_c = cum_c[0] + cum_c[1] - gt

    scale = GDN_DK ** -0.5
    insts = [(d, h) for d in range(2) for h in range(nh)]
    a_all, tinv_all, rhs_all = [], [], []
    for d, h in insts:
        mask, smask = (low, slow) if d == 0 else (upp, supp)
        ib = MISC_BETA + d * nh + h
        ig = MISC_DEC + d * nh + h
        beta = gt[:, ib:ib + 1]
        gc = cum_c[d][:, ig:ig + 1]
        gr = cum_r[d][ig:ig + 1, :]
        gl = tot_c[:, ig:ig + 1]
        q = qkvn_ref[:, h * LANE:(h + 1) * LANE]
        k = qkvn_ref[:, (nh + h) * LANE:(nh + h + 1) * LANE]
        v = qkvn_ref[:, (2 * nh + h) * LANE:(2 * nh + h + 1) * LANE]
        decay = jnp.where(mask, jnp.exp(jnp.where(mask, gc - gr, 0.0)), 0.0)
        kb = k * beta
        k16 = k.astype(BF16)
        a = jnp.where(smask, _mm_nt(kb.astype(BF16), k16) * decay, 0.0)
        a_all.append(a.astype(BF16))
        tinv_all.append(eye - jnp.where(pair, a, 0.0))
        rhs_all.append(jnp.concatenate([v * beta, kb * jnp.exp(gc)], axis=1).astype(BF16))
        hs = slice(h * LANE, (h + 1) * LANE)
        qs = q * scale
        qk_ref[d, :, h * nb:(h + 1) * nb] = (_mm_nt(qs.astype(BF16), k16) * decay).astype(BF16)
        qg_ref[d, :, hs] = (qs * jnp.exp(gc)).astype(BF16)
        kgt_ref[d, hs, :] = (k * jnp.exp(gl - gc)).T.astype(BF16)

    zero16 = jnp.zeros((nb, nb), BF16)
    for off in offs:
        for i in range(len(insts)):
            t16 = tinv_all[i].astype(BF16)
            ta = _mm(t16, jnp.where(off, a_all[i], zero16))
            tinv_all[i] = tinv_all[i] - _mm(ta.astype(BF16), t16)
    for i, (d, h) in enumerate(insts):
        x = _mm(tinv_all[i].astype(BF16), rhs_all[i])
        hs = slice(h * LANE, (h + 1) * LANE)
        u_ref[d, :, hs] = x[:, :LANE]
        w_ref[d, :, hs] = x[:, LANE:].astype(BF16)


def _gdn_local(qkvn, gates):
    R, W = qkvn.shape
    nb = GDN_BLOCK
    HW = GDN_HEADS * LANE
    return pl.pallas_call(
        _gdn_b_kernel,
        grid=(R // nb,),
        in_specs=[pl.BlockSpec((nb, W), lambda r: (r, 0)),
                  pl.BlockSpec((nb, LANE), lambda r: (r, 0))],
        out_specs=[pl.BlockSpec((2, nb, HW), lambda r: (0, r, 0)),
                   pl.BlockSpec((2, nb, HW), lambda r: (0, r, 0)),
                   pl.BlockSpec((2, nb, HW), lambda r: (0, r, 0)),
                   pl.BlockSpec((2, HW, nb), lambda r: (0, 0, r)),
                   pl.BlockSpec((2, nb, GDN_HEADS * nb), lambda r: (0, r, 0))],
        out_shape=[jax.ShapeDtypeStruct((2, R, HW), F32),
                   jax.ShapeDtypeStruct((2, R, HW), BF16),
                   jax.ShapeDtypeStruct((2, R, HW), BF16),
                   jax.ShapeDtypeStruct((2, HW, R), BF16),
                   jax.ShapeDtypeStruct((2, R, GDN_HEADS * nb), BF16)],
        compiler_params=_params(("parallel",), VMEM_BIG),
        name="gdn_local",
    )(qkvn, gates)


def _gdn_c_kernel(u_ref, w_ref, qg_ref, kgt_ref, qk_ref, gates_ref, o_ref, s_ref, vn_ref, *, d):
    nb = GDN_BLOCK
    nh = GDN_HEADS

    @pl.when(pl.program_id(1) == 0)
    def _():
        s_ref[...] = jnp.zeros_like(s_ref)
        vn_ref[...] = jnp.zeros_like(vn_ref)

    rowi = lax.broadcasted_iota(jnp.int32, (nb, 1), 0)
    nchunk = nb // GDN_CHUNK
    order = range(nchunk) if d == 0 else range(nchunk - 1, -1, -1)
    for c in order:
        rs = slice(c * GDN_CHUNK, (c + 1) * GDN_CHUNK)
        decay_all = jnp.exp(jnp.sum(gates_ref[rs, :], axis=0, keepdims=True))
        cmask = jnp.logical_and(rowi >= c * GDN_CHUNK, rowi < (c + 1) * GDN_CHUNK)
        for h in range(nh):
            hs = slice(h * LANE, (h + 1) * LANE)
            ig = MISC_DEC + d * nh + h
            s = s_ref[h]
            s16 = s.astype(BF16)
            v_new = u_ref[0, rs, hs] - _mm(w_ref[0, rs, hs], s16)
            vn_ref[h, rs, :] = v_new.astype(BF16)
            vn = vn_ref[h]
            o_ref[rs, hs] = _mm(qg_ref[0, rs, hs], s16) + _mm(qk_ref[0, rs, h * nb:(h + 1) * nb], vn)
            v_cur = jnp.where(cmask, vn, jnp.zeros_like(vn))
            s_ref[h] = s * decay_all[:, ig:ig + 1] + _mm(kgt_ref[0, hs, :], v_cur)


def _gdn_scan(u, w, qg, kgt, qk, gates, geom, d):
    nb = GDN_BLOCK
    HW = GDN_HEADS * LANE
    B, RL, R = geom["B"], geom["RL"], geom["R"]
    per = geom["T"] // nb
    nlat = RL // nb

    def blk(b, j):
        lat = b * per + (j - 1 if d == 0 else per - j)
        return jnp.where(j == 0, nlat + b, lat)

    tok = lambda w_: pl.BlockSpec((1, nb, w_), lambda b, j: (d, blk(b, j), 0))
    return pl.pallas_call(
        functools.partial(_gdn_c_kernel, d=d),
        grid=(B, per + 1),
        in_specs=[tok(HW), tok(HW), tok(HW),
                  pl.BlockSpec((1, HW, nb), lambda b, j: (d, 0, blk(b, j))),
                  tok(GDN_HEADS * nb),
                  pl.BlockSpec((nb, LANE), lambda b, j: (blk(b, j), 0))],
        out_specs=pl.BlockSpec((nb, HW), lambda b, j: (blk(b, j), 0)),
        out_shape=jax.ShapeDtypeStruct((R, HW), F32),
        scratch_shapes=[pltpu.VMEM((GDN_HEADS, GDN_DK, GDN_DV), F32),
                        pltpu.VMEM((GDN_HEADS, nb, GDN_DV), BF16)],
        compiler_params=_params(("parallel", "arbitrary")),
        name="gdn_scan_fwd" if d == 0 else "gdn_scan_bwd",
    )(u, w, qg, kgt, qk, gates)


def _gdn_out_kernel(of_ref, ob_ref, og_ref, g_ref, o_ref):
    o = of_ref[...] + ob_ref[...]
    gate = _silu(og_ref[...].astype(F32))
    g = g_ref[...]
    for h in range(GDN_HEADS):
        hs = slice(h * LANE, (h + 1) * LANE)
        o_ref[:, hs] = (_rms(o[:, hs], g) * gate[:, hs]).astype(BF16)


def _gdn_out(o_f, o_b, P, g_out, rows):
    HW = GDN_HEADS * LANE
    bm = 512 if rows % 512 == 0 else 256
    return pl.pallas_call(
        _gdn_out_kernel,
        grid=(rows // bm,),
        in_specs=[pl.BlockSpec((bm, HW), lambda i: (i, 0)),
                  pl.BlockSpec((bm, HW), lambda i: (i, 0)),
                  pl.BlockSpec((bm, HW), lambda i: (i, OFF_OG // HW)),
                  pl.BlockSpec((1, LANE), lambda i: (0, 0))],
        out_specs=pl.BlockSpec((bm, HW), lambda i: (i, 0)),
        out_shape=jax.ShapeDtypeStruct((rows, HW), BF16),
        compiler_params=_params(("parallel",)),
        name="gdn_out",
    )(o_f, o_b, P, g_out.reshape(1, LANE))


def _residual_rows(x_ref, gate, o_ref, post=None):
    def body(r, carry):
        rs = pl.ds(pl.multiple_of(r * NORM_ROWS, NORM_ROWS), NORM_ROWS)
        v = x_ref[rs, :] + gate * o_ref[rs, :]
        o_ref[rs, :] = v if post is None else post(v)
        return carry

    lax.fori_loop(0, x_ref.shape[0] // NORM_ROWS, body, 0)


def _merge_kernel(x_ref, mod_ref, g_ref, oa_ref, ob_ref, oc_ref, wga_ref, wgb_ref, wgc_ref, wua_ref, wub_ref,
                  wuc_ref, wo_ref, o_ref, hx_ref):
    j = pl.program_id(1)
    m = mod_ref[0]

    @pl.when(j == 0)
    def _():
        _norm_mod_rows(x_ref, g_ref[...], m[0:1], m[1:2], hx_ref)
        o_ref[...] = jnp.zeros_like(o_ref)

    hx = hx_ref[...]
    oa, ob, oc = oa_ref[...], ob_ref[...], oc_ref[...]
    hc = wo_ref.shape[0]
    ga, gb, gc = _mm(hx, wga_ref[...]), _mm(hx, wgb_ref[...]), _mm(hx, wgc_ref[...])
    ua, ub, uc = _mm(oa, wua_ref[...]), _mm(ob, wub_ref[...]), _mm(oc, wuc_ref[...])
    acc = None
    for cs in (slice(0, hc // 2), slice(hc // 2, hc)):
        y = _sigmoid(ga[:, cs]) * ua[:, cs] + _sigmoid(gb[:, cs]) * ub[:, cs] + _sigmoid(gc[:, cs]) * uc[:, cs]
        t = _mm(y.astype(BF16), wo_ref[cs, :])
        acc = t if acc is None else acc + t
    o_ref[...] += acc

    @pl.when(j == pl.num_programs(1) - 1)
    def _():
        _residual_rows(x_ref, m[2:3], o_ref)


def _merge(xs, mod_l, g1, oa, ob, oc, wg, wua, wub, wuc, wo, geom, rows):
    D = xs.shape[1]
    bm = _pick_block(512, geom["T"], geom["RC"])
    hc = 512
    nj = D // hc
    sel = geom["mod_sel"](bm)
    rowb = lambda w_: pl.BlockSpec((bm, w_), lambda i, j: (i, 0))
    gate_w = lambda br: pl.BlockSpec((D, hc), lambda i, j: (0, br * nj + j))
    return pl.pallas_call(
        _merge_kernel,
        grid=(rows // bm, nj),
        in_specs=[rowb(D),
                  pl.BlockSpec((1, N_MOD, D), lambda i, j: (sel(i), 0, 0)),
                  pl.BlockSpec((1, D), lambda i, j: (0, 0)),
                  rowb(oa.shape[1]), rowb(ob.shape[1]), rowb(oc.shape[1]),
                  gate_w(0), gate_w(1), gate_w(2),
                  pl.BlockSpec((wua.shape[0], hc), lambda i, j: (0, j)),
                  pl.BlockSpec((wub.shape[0], hc), lambda i, j: (0, j)),
                  pl.BlockSpec((wuc.shape[0], hc), lambda i, j: (0, j)),
                  pl.BlockSpec((hc, D), lambda i, j: (j, 0))],
        out_specs=rowb(D),
        out_shape=jax.ShapeDtypeStruct((rows, D), F32),
        scratch_shapes=[pltpu.VMEM((bm, D), BF16)],
        compiler_params=_params(("parallel", "arbitrary"), VMEM_BIG),
        name="merge",
    )(xs, mod_l, g1.reshape(1, D), oa, ob, oc, wg, wg, wg, wua, wub, wuc, wo)


def _mlp_kernel(x_ref, mod_ref, g_ref, w1_ref, w2_ref, gf_ref, o_ref, hx_ref, *, final):
    j = pl.program_id(1)
    m = mod_ref[0]

    @pl.when(j == 0)
    def _():
        _norm_mod_rows(x_ref, g_ref[...], m[3:4], m[4:5], hx_ref)
        o_ref[...] = jnp.zeros_like(o_ref)

    hx = hx_ref[...]
    fc = w2_ref.shape[0]
    h1 = _mm(hx, w1_ref[...])
    acc = None
    for cs in (slice(0, fc // 2), slice(fc // 2, fc)):
        h = jnp.maximum(h1[:, cs], 0.0)
        t = _mm((h * h).astype(BF16), w2_ref[cs, :])
        acc = t if acc is None else acc + t
    o_ref[...] += acc

    @pl.when(j == pl.num_programs(1) - 1)
    def _():
        gf = gf_ref[...]
        _residual_rows(x_ref, m[5:6], o_ref, (lambda v: _rms(v, gf)) if final else None)


def _mlp(x1, mod_l, g2, w1, w2, g_final, geom, final):
    rows, D = x1.shape
    F = w1.shape[1]
    bm = _pick_block(1024, geom["T"], geom["RC"])
    fc = 512
    sel = geom["mod_sel"](bm)
    return pl.pallas_call(
        functools.partial(_mlp_kernel, final=final),
        grid=(rows // bm, F // fc),
        in_specs=[pl.BlockSpec((bm, D), lambda i, j: (i, 0)),
                  pl.BlockSpec((1, N_MOD, D), lambda i, j: (sel(i), 0, 0)),
                  pl.BlockSpec((1, D), lambda i, j: (0, 0)),
                  pl.BlockSpec((D, fc), lambda i, j: (0, j)),
                  pl.BlockSpec((fc, D), lambda i, j: (j, 0)),
                  pl.BlockSpec((1, D), lambda i, j: (0, 0))],
        out_specs=pl.BlockSpec((bm, D), lambda i, j: (i, 0)),
        out_shape=jax.ShapeDtypeStruct((rows, D), F32),
        scratch_shapes=[pltpu.VMEM((bm, D), BF16)],
        compiler_params=_params(("parallel", "arbitrary"), VMEM_BIG),
        name="mlp",
    )(x1, mod_l, g2.reshape(1, D), w1, w2, g_final.reshape(1, D))


def _rope_tables(T, bm):
    t = np.arange(T)
    row, col = t // GRID_W, t % GRID_W

    def table(n_rot):
        n = n_rot // 2
        half = n // 2
        inv = ROPE_THETA ** (-np.arange(half, dtype=np.float64) / half)
        cos = np.zeros((T + bm, LANE))
        sin = np.zeros((T + bm, LANE))
        for s, pos in enumerate((row, col)):
            ang = pos[:, None].astype(np.float32).astype(np.float64) * inv.astype(np.float32)[None, :]
            ang = ang.astype(np.float32).astype(np.float64)
            c, sn = np.cos(ang), np.sin(ang)
            cos[:T, s * n:s * n + half] = c
            cos[:T, s * n + half:(s + 1) * n] = c
            sin[:T, s * n:s * n + half] = -sn
            sin[:T, s * n + half:(s + 1) * n] = sn
        cos[T:, :] = 1.0
        return jnp.asarray(cos, F32), jnp.asarray(sin, F32)

    cg, sg = table(GQA_HD)
    cm, sm = table(MLA_ROPE)
    return cg, sg, cm, sm


def _prep_layer_weights(w_in, w_qb, w_kvb):
    D = w_in.shape[0]
    splits = (MLA_Q_RANK, MLA_KV_RANK, MLA_ROPE, GQA_HEADS * GQA_HD, GQA_KV_HEADS * GQA_HD, GQA_KV_HEADS * GQA_HD,
              GDN_HEADS * (2 * GDN_DK + GDN_DV), 2 * GDN_HEADS, 2 * GDN_HEADS, GDN_HEADS * GDN_DV, 3 * D)
    off = np.cumsum((0,) + splits)
    cq, ckv, kpe, gq, gk, gv, qkv, beta, dec, og, bg = (w_in[:, off[i]:off[i + 1]] for i in range(len(splits)))
    used = OFF_MISC + MLA_ROPE + 4 * GDN_HEADS
    wa = jnp.concatenate([a.astype(BF16) for a in (gq, cq, ckv, gk, gv, og, qkv, kpe, beta, dec)]
                         + [jnp.zeros((D, N_SMALL - used), BF16)], axis=1)
    wg = bg.astype(BF16)
    qb = w_qb.reshape(-1, MLA_HEADS, MLA_NOPE + MLA_ROPE)
    qpe = jnp.pad(qb[:, :, MLA_NOPE:], ((0, 0), (0, 0), (0, LANE - MLA_ROPE)))
    wqb = jnp.concatenate([qb[:, :, :MLA_NOPE].reshape(-1, MLA_HEADS * LANE),
                           qpe.reshape(-1, MLA_HEADS * LANE)], axis=1).astype(BF16)
    kvb = w_kvb.reshape(-1, MLA_HEADS, MLA_NOPE + MLA_V)
    wkvb = jnp.concatenate([kvb[:, :, :MLA_NOPE].reshape(-1, MLA_HEADS * LANE),
                            kvb[:, :, MLA_NOPE:].reshape(-1, MLA_HEADS * LANE)], axis=1).astype(BF16)
    return wa, wg, wqb, wkvb


def kernel(x, c, ctx, c_ctx, w_mod, b_mod, g_norm1, w_in, g_mla_q, w_mla_qb, g_mla_kv, w_mla_kvb, g_gqa_q, g_gqa_k,
           w_conv, a_log, dt_bias, g_gdn_out, w_up_a, w_up_b, w_up_c, w_out, g_norm2, w_ff1, w_ff2, g_final):
    B, T, D = x.shape
    C = ctx.shape[1]
    L = w_mod.shape[0]
    assert C == GDN_BLOCK and T % GDN_BLOCK == 0 and T % GRID_W == 0 and B < 8
    RL, RC = B * T, B * C
    geom = dict(B=B, T=T, C=C, RL=RL, RC=RC, R=RL + RC)
    geom["mod_sel"] = lambda bm: (lambda i: jnp.where(i < RL // bm, i // (T // bm), B))

    mod = _modulation(c, c_ctx, w_mod, b_mod)
    tabs = _rope_tables(T, _pick_block(512, T, RC))
    xs = jnp.concatenate([x.reshape(RL, D), ctx.reshape(RC, D)], axis=0)
    mla_scale = (MLA_NOPE + MLA_ROPE) ** -0.5
    gqa_scale = GQA_HD ** -0.5

    for l in range(L):
        last = l == L - 1
        wa, wg, wqb, wkvb = _prep_layer_weights(w_in[l], w_mla_qb[l], w_mla_kvb[l])
        P = _in_proj(xs, mod[l], g_norm1[l], wa, geom)
        qm, km, vm, qg, kg, vg = _attn_prep(P, g_mla_q[l], g_mla_kv[l], wqb, wkvb, g_gqa_q[l], g_gqa_k[l], tabs, geom)
        oa = _attention(qm, km, vm, mla_scale, geom, "attn_mla")
        ob = _attention(qg, kg, vg, gqa_scale, geom, "attn_gqa")
        qkvn, gates = _gdn_conv(P, w_conv[l], a_log[l], dt_bias[l], geom)
        u, w, qgd, kgt, qk = _gdn_local(qkvn, gates)
        o_f = _gdn_scan(u, w, qgd, kgt, qk, gates, geom, 0)
        o_b = _gdn_scan(u, w, qgd, kgt, qk, gates, geom, 1)
        rows = RL if last else RL + RC
        oc = _gdn_out(o_f, o_b, P, g_gdn_out[l], rows)
        if not last:
            oa = jnp.concatenate([oa, _attention_ctx(qm, km, vm, mla_scale, geom, "attn_mla_ctx")], axis=0)
            ob = jnp.concatenate([ob, _attention_ctx(qg, kg, vg, gqa_scale, geom, "attn_gqa_ctx")], axis=0)
        x1 = _merge(xs, mod[l], g_norm1[l], oa, ob, oc, wg, w_up_a[l].astype(BF16), w_up_b[l].astype(BF16),
                    w_up_c[l].astype(BF16), w_out[l].astype(BF16), geom, rows)
        xs = _mlp(x1, mod[l], g_norm2[l], w_ff1[l].astype(BF16), w_ff2[l].astype(BF16), g_final, geom, last)
    return xs.reshape(B, T, D)
```

```python
import functools
import math

import numpy as np
import jax
import jax.numpy as jnp
from jax import lax
from jax.experimental import pallas as pl
from jax.experimental.pallas import tpu as pltpu

F32 = jnp.float32
BF16 = jnp.bfloat16

GRID_W = 64
EPS = 1e-6
ROPE_THETA = 10000.0
MLA_HEADS, MLA_Q_RANK, MLA_KV_RANK, MLA_NOPE, MLA_ROPE, MLA_V = 4, 512, 512, 128, 64, 128
GQA_HEADS, GQA_KV_HEADS, GQA_HD = 8, 2, 128
GDN_HEADS, GDN_DK, GDN_DV, GDN_CONV, GDN_CHUNK = 4, 128, 128, 5, 64
N_MOD = 6
LANE = 128
GDN_BLOCK = 256
HALO_ROWS = 16
VMEM_BIG = 56 * 1024 * 1024

OFF_GQ, OFF_CQ, OFF_CKV, OFF_GK, OFF_GV, OFF_OG, OFF_QKV, OFF_MISC = 0, 1024, 1536, 2048, 2304, 2560, 3072, 4608
N_SMALL = 5120
MISC_BETA, MISC_DEC = 64, 72


def _mm(a, b):
    return jnp.dot(a, b, preferred_element_type=F32)


def _mm_nt(a, b):
    return lax.dot_general(a, b, (((1,), (1,)), ((), ())), preferred_element_type=F32)


def _silu(x):
    return x / (1.0 + jnp.exp(-x))


def _sigmoid(x):
    return 1.0 / (1.0 + jnp.exp(-x))


def _rms(x, g):
    return x * lax.rsqrt(jnp.mean(x * x, axis=-1, keepdims=True) + EPS) * g


def _norm_mod(x, g, shift, scale):
    return _rms(x, g) * (1.0 + scale) + shift


def _params(sem, vmem=None):
    kw = dict(dimension_semantics=sem)
    if vmem is not None:
        kw["vmem_limit_bytes"] = vmem
    return pltpu.CompilerParams(**kw)


def _pick_block(pref, *sizes):
    b = pref
    while any(s % b for s in sizes):
        b //= 2
    return b


def _mod_kernel(s_ref, w_ref, b_ref, o_ref):
    s = _silu(s_ref[...])
    o_ref[0] = _mm(s.astype(BF16), w_ref[0].astype(BF16)) + b_ref[0]


def _modulation(c, c_ctx, w_mod, b_mod):
    L, D, N = w_mod.shape
    B = c.shape[0]
    rows = jnp.zeros((8, D), F32).at[:B].set(c).at[B].set(c_ctx)
    bn = 2048
    out = pl.pallas_call(
        _mod_kernel,
        grid=(L, N // bn),
        in_specs=[pl.BlockSpec((8, D), lambda l, j: (0, 0)),
                  pl.BlockSpec((1, D, bn), lambda l, j: (l, 0, j)),
                  pl.BlockSpec((1, 1, bn), lambda l, j: (l, 0, j))],
        out_specs=pl.BlockSpec((1, 8, bn), lambda l, j: (l, 0, j)),
        out_shape=jax.ShapeDtypeStruct((L, 8, N), F32),
        compiler_params=_params(("parallel", "parallel"), VMEM_BIG),
        name="adaln_mod",
    )(rows, w_mod, b_mod.reshape(L, 1, N))
    return out.reshape(L, 8, N_MOD, D)


NORM_ROWS = 64


def _norm_mod_rows(x_ref, g, shift, scale, hx_ref):
    def body(r, carry):
        rs = pl.ds(pl.multiple_of(r * NORM_ROWS, NORM_ROWS), NORM_ROWS)
        hx_ref[rs, :] = _norm_mod(x_ref[rs, :], g, shift, scale).astype(BF16)
        return carry

    lax.fori_loop(0, x_ref.shape[0] // NORM_ROWS, body, 0)


def _row_source_specs(src, bm, D, n_latent_rows):
    nlat = n_latent_rows // bm
    off_b = src[2] // bm
    spec_a = pl.BlockSpec((bm, D), lambda i, j: (jnp.minimum(i, nlat - 1), 0))
    spec_b = pl.BlockSpec((bm, D), lambda i, j: (off_b + jnp.maximum(i - nlat, 0), 0), pipeline_mode=pl.Buffered(1))
    return nlat, spec_a, spec_b


def _for_row_source(nlat, xa_ref, xb_ref, fn):
    i = pl.program_id(0)

    @pl.when(i < nlat)
    def _():
        fn(xa_ref)

    @pl.when(i >= nlat)
    def _():
        fn(xb_ref)


def _inproj_kernel(xa_ref, xb_ref, mod_ref, g_ref, w_ref, o_ref, hx_ref, *, nlat):
    @pl.when(pl.program_id(1) == 0)
    def _():
        m = mod_ref[0]
        _for_row_source(nlat, xa_ref, xb_ref,
                        lambda x_ref: _norm_mod_rows(x_ref, g_ref[...], m[0:1], m[1:2], hx_ref))

    o_ref[...] = _mm(hx_ref[...], w_ref[...]).astype(BF16)


def _in_proj(src, mod_l, g1, wa, geom):
    R, D = geom["R"], src[0].shape[1]
    N = wa.shape[1]
    bm = _pick_block(1024, geom["T"], geom["RC"])
    bn = 1024
    sel = geom["mod_sel"](bm)
    nlat, spec_a, spec_b = _row_source_specs(src, bm, D, geom["RL"])
    return pl.pallas_call(
        functools.partial(_inproj_kernel, nlat=nlat),
        grid=(R // bm, N // bn),
        in_specs=[spec_a, spec_b,
                  pl.BlockSpec((1, N_MOD, D), lambda i, j: (sel(i), 0, 0)),
                  pl.BlockSpec((1, D), lambda i, j: (0, 0)),
                  pl.BlockSpec((D, bn), lambda i, j: (0, j))],
        out_specs=pl.BlockSpec((bm, bn), lambda i, j: (i, j)),
        out_shape=jax.ShapeDtypeStruct((R, N), BF16),
        scratch_shapes=[pltpu.VMEM((bm, D), BF16)],
        compiler_params=_params(("parallel", "arbitrary"), VMEM_BIG),
        name="in_proj",
    )(src[0], src[1], mod_l, g1.reshape(1, D), wa)


def _prep_kernel(gq_ref, cq_ref, ckv_ref, gk_ref, gv_ref, misc_ref, gmq_ref, gmkv_ref, wqb_ref, wkvb_ref,
                 ggq_ref, ggk_ref, cosg_ref, sing_ref, cosm_ref, sinm_ref,
                 qm_ref, km_ref, vm_ref, qg_ref, kg_ref, vg_ref):
    lane = lax.broadcasted_iota(jnp.int32, (1, LANE), 1)
    cosm, sinm = cosm_ref[...], sinm_ref[...]
    cosg, sing = cosg_ref[...], sing_ref[...]
    first_m = (lane & 31) < 16
    first_g = (lane & 63) < 32
    ones_col = jnp.where(lane == 0, 1.0, 0.0).astype(BF16) + jnp.zeros((cq_ref.shape[0], LANE), BF16)

    def rope_m(x):
        partner = jnp.where(first_m, pltpu.roll(x, LANE - 16, 1), pltpu.roll(x, 16, 1))
        return x * cosm + partner * sinm

    def rope_g(x):
        partner = jnp.where(first_g, pltpu.roll(x, LANE - 32, 1), pltpu.roll(x, 32, 1))
        return x * cosg + partner * sing

    qa = _mm(_rms(cq_ref[...].astype(F32), gmq_ref[...]).astype(BF16), wqb_ref[...])
    kva = _mm(_rms(ckv_ref[...].astype(F32), gmkv_ref[...]).astype(BF16), wkvb_ref[...])
    kpe = rope_m(jnp.where(lane < MLA_ROPE, misc_ref[...].astype(F32), 0.0)).astype(BF16)
    nh = MLA_HEADS * LANE
    for h in range(MLA_HEADS):
        hs = slice(h * LANE, (h + 1) * LANE)
        ps = slice(nh + h * LANE, nh + (h + 1) * LANE)
        qm_ref[h, :, 0:LANE] = qa[:, hs].astype(BF16)
        qm_ref[h, :, LANE:2 * LANE] = rope_m(qa[:, ps]).astype(BF16)
        km_ref[h, :, 0:LANE] = kva[:, hs].astype(BF16)
        km_ref[h, :, LANE:2 * LANE] = kpe
        vm_ref[h, :, 0:LANE] = kva[:, ps].astype(BF16)
        vm_ref[h, :, LANE:2 * LANE] = ones_col

    ggq, ggk = ggq_ref[...], ggk_ref[...]
    for h in range(GQA_HEADS):
        hs = slice(h * LANE, (h + 1) * LANE)
        qg_ref[h] = rope_g(_rms(gq_ref[:, hs].astype(F32), ggq)).astype(BF16)
    for h in range(GQA_KV_HEADS):
        hs = slice(h * LANE, (h + 1) * LANE)
        kg_ref[h] = rope_g(_rms(gk_ref[:, hs].astype(F32), ggk)).astype(BF16)
        vg_ref[h, :, 0:LANE] = gv_ref[:, hs]
        vg_ref[h, :, LANE:2 * LANE] = ones_col


def _attn_prep(P, gmq, gmkv, wqb, wkvb, ggq, ggk, tabs, geom):
    R = P.shape[0]
    bm = _pick_block(512, geom["T"], geom["RC"])
    nlat = geom["RL"] // bm
    per = geom["T"] // bm
    tsel = lambda i: jnp.where(i < nlat, i % per, per)
    col = lambda w, off: pl.BlockSpec((bm, w), lambda i: (i, off // w))
    full = lambda a: pl.BlockSpec(a.shape, lambda i: (0,) * a.ndim)
    tab = pl.BlockSpec((bm, LANE), lambda i: (tsel(i), 0))
    outs = [((MLA_HEADS, R, 2 * LANE), 2 * LANE), ((MLA_HEADS, R, 2 * LANE), 2 * LANE),
            ((MLA_HEADS, R, 2 * LANE), 2 * LANE), ((GQA_HEADS, R, LANE), LANE), ((GQA_KV_HEADS, R, LANE), LANE),
            ((GQA_KV_HEADS, R, 2 * LANE), 2 * LANE)]
    gmq, gmkv, ggq, ggk = (a.reshape(1, -1) for a in (gmq, gmkv, ggq, ggk))
    return pl.pallas_call(
        _prep_kernel,
        grid=(R // bm,),
        in_specs=[col(1024, OFF_GQ), col(512, OFF_CQ), col(512, OFF_CKV), col(256, OFF_GK), col(256, OFF_GV),
                  col(LANE, OFF_MISC), full(gmq), full(gmkv), full(wqb), full(wkvb), full(ggq), full(ggk),
                  tab, tab, tab, tab],
        out_specs=[pl.BlockSpec((s[0], bm, w), lambda i: (0, i, 0)) for s, w in outs],
        out_shape=[jax.ShapeDtypeStruct(s, BF16) for s, _ in outs],
        compiler_params=_params(("parallel",), VMEM_BIG),
        name="attn_prep",
    )(P, P, P, P, P, P, gmq, gmkv, wqb, wkvb, ggq, ggk, *tabs)


def _softmax_pv(q, kv_chunks, s_ref, o_ref, exp2_scale):
    sub = min(q.shape)
    for r0 in range(0, q.shape[0], sub):
        rows = slice(r0, r0 + sub)
        qr = q[rows]
        m_acc = None
        off = 0
        for k_ref, _, st, n in kv_chunks:
            s = _mm_nt(qr, k_ref[0, st:st + n, :])
            s_ref[rows, off:off + n] = s
            for t in range(n // LANE):
                tile = s[:, t * LANE:(t + 1) * LANE]
                m_acc = tile if m_acc is None else jnp.maximum(m_acc, tile)
            off += n
        m = jnp.max(m_acc, axis=-1, keepdims=True)
        acc = None
        off = 0
        for _, v_ref, st, n in kv_chunks:
            p = jnp.exp2((s_ref[rows, off:off + n] - m) * exp2_scale).astype(BF16)
            pv = _mm(p, v_ref[0, st:st + n, :])
            acc = pv if acc is None else acc + pv
            off += n
        o_ref[rows, :] = (acc[:, :LANE] / acc[:, LANE:LANE + 1]).astype(BF16)


def _attn_kernel(q_ref, kx_ref, kc_ref, vx_ref, vc_ref, o_ref, s_ref, *, exp2_scale, kchunk):
    chunks = [(kx_ref, vx_ref, i * kchunk, kchunk) for i in range(kx_ref.shape[1] // kchunk)]
    chunks.append((kc_ref, vc_ref, 0, kc_ref.shape[1]))
    _softmax_pv(q_ref[0], chunks, s_ref, o_ref, exp2_scale)


def _attn_ctx_kernel(q_ref, kc_ref, vc_ref, o_ref, s_ref, *, exp2_scale):
    _softmax_pv(q_ref[0], [(kc_ref, vc_ref, 0, kc_ref.shape[1])], s_ref, o_ref, exp2_scale)


def _attention(q, k, v, scale, geom, name):
    H, _, dk = q.shape
    Hk, _, dve = v.shape
    grp = H // Hk
    B, T, C, RL = geom["B"], geom["T"], geom["C"], geom["RL"]
    bq = _pick_block(1024, T)
    nq = T // bq
    cb = RL // C
    return pl.pallas_call(
        functools.partial(_attn_kernel, exp2_scale=scale * math.log2(math.e), kchunk=256),
        grid=(B, H, nq),
        in_specs=[pl.BlockSpec((1, bq, dk), lambda b, h, j: (h, b * nq + j, 0)),
                  pl.BlockSpec((1, T, dk), lambda b, h, j: (h // grp, b, 0)),
                  pl.BlockSpec((1, C, dk), lambda b, h, j: (h // grp, cb + b, 0)),
                  pl.BlockSpec((1, T, dve), lambda b, h, j: (h // grp, b, 0)),
                  pl.BlockSpec((1, C, dve), lambda b, h, j: (h // grp, cb + b, 0))],
        out_specs=pl.BlockSpec((bq, LANE), lambda b, h, j: (b * nq + j, h)),
        out_shape=jax.ShapeDtypeStruct((RL, H * LANE), BF16),
        scratch_shapes=[pltpu.VMEM((bq, T + C), F32)],
        compiler_params=_params(("parallel", "parallel", "parallel"), VMEM_BIG),
        name=name,
    )(q, k, k, v, v)


def _attention_ctx(q, k, v, scale, geom, name):
    H, _, dk = q.shape
    Hk, _, dve = v.shape
    grp = H // Hk
    B, C, RL, RC = geom["B"], geom["C"], geom["RL"], geom["RC"]
    cb = RL // C
    return pl.pallas_call(
        functools.partial(_attn_ctx_kernel, exp2_scale=scale * math.log2(math.e)),
        grid=(B, H),
        in_specs=[pl.BlockSpec((1, C, dk), lambda b, h: (h, cb + b, 0)),
                  pl.BlockSpec((1, C, dk), lambda b, h: (h // grp, cb + b, 0)),
                  pl.BlockSpec((1, C, dve), lambda b, h: (h // grp, cb + b, 0))],
        out_specs=pl.BlockSpec((C, LANE), lambda b, h: (b, h)),
        out_shape=jax.ShapeDtypeStruct((RC, H * LANE), BF16),
        scratch_shapes=[pltpu.VMEM((C, C), F32)],
        compiler_params=_params(("parallel", "parallel")),
        name=name,
    )(q, k, v)


def _gdn_a_kernel(cur_ref, prev_ref, next_ref, misc_ref, wc_ref, gvec_ref, qkvn_ref, gates_ref, ext_ref,
                  *, nlat, per_seq):
    r = pl.program_id(0)
    pos = r % per_seq
    is_ctx = r >= nlat
    pf = jnp.where(jnp.logical_or(is_ctx, pos == 0), 0.0, 1.0).astype(F32)
    nf = jnp.where(jnp.logical_or(is_ctx, pos == per_seq - 1), 0.0, 1.0).astype(F32)
    nb = GDN_BLOCK
    hal = HALO_ROWS
    ext_ref[0:hal, :] = prev_ref[...].astype(F32) * pf
    ext_ref[hal:hal + nb, :] = cur_ref[...].astype(F32)
    ext_ref[hal + nb:2 * hal + nb, :] = next_ref[...].astype(F32) * nf
    pad = GDN_CONV // 2
    n_qk = 2 * GDN_HEADS
    for c in range(3 * GDN_HEADS):
        cs = slice(c * LANE, (c + 1) * LANE)
        acc = None
        for j in range(GDN_CONV):
            t = ext_ref[hal - pad + j:hal - pad + j + nb, cs] * wc_ref[j:j + 1, cs]
            acc = t if acc is None else acc + t
        y = _silu(acc)
        if c < n_qk:
            y = y * lax.rsqrt(jnp.sum(y * y, axis=-1, keepdims=True) + EPS)
        qkvn_ref[:, cs] = y

    lane = lax.broadcasted_iota(jnp.int32, (1, LANE), 1)
    raw = misc_ref[...].astype(F32)
    beta = _sigmoid(raw)
    z = raw + gvec_ref[1:2, :]
    softplus = jnp.maximum(z, 0.0) + jnp.log1p(jnp.exp(-jnp.abs(z)))
    g = -jnp.exp(gvec_ref[0:1, :]) * softplus
    is_beta = jnp.logical_and(lane >= MISC_BETA, lane < MISC_DEC)
    is_g = jnp.logical_and(lane >= MISC_DEC, lane < MISC_DEC + 2 * GDN_HEADS)
    gates_ref[...] = jnp.where(is_beta, beta, jnp.where(is_g, g, 0.0))


def _gdn_conv(P, w_conv, a_log, dt_bias, geom):
    R = P.shape[0]
    nb = GDN_BLOCK
    W = 3 * GDN_HEADS * LANE
    nblk = R // nb
    hal = HALO_ROWS
    sub = nb // hal
    gvec = jnp.zeros((2, LANE), F32)
    gvec = gvec.at[0, MISC_DEC:MISC_DEC + 2 * GDN_HEADS].set(a_log.reshape(-1))
    gvec = gvec.at[1, MISC_DEC:MISC_DEC + 2 * GDN_HEADS].set(dt_bias.reshape(-1))
    qc = OFF_QKV // W
    return pl.pallas_call(
        functools.partial(_gdn_a_kernel, nlat=geom["RL"] // nb, per_seq=geom["T"] // nb),
        grid=(nblk,),
        in_specs=[pl.BlockSpec((nb, W), lambda r: (r, qc)),
                  pl.BlockSpec((hal, W), lambda r: (jnp.maximum(r * sub - 1, 0), qc)),
                  pl.BlockSpec((hal, W), lambda r: (jnp.minimum((r + 1) * sub, R // hal - 1), qc)),
                  pl.BlockSpec((nb, LANE), lambda r: (r, OFF_MISC // LANE)),
                  pl.BlockSpec((GDN_CONV, W), lambda r: (0, 0)),
                  pl.BlockSpec((2, LANE), lambda r: (0, 0))],
        out_specs=[pl.BlockSpec((nb, W), lambda r: (r, 0)),
                   pl.BlockSpec((nb, LANE), lambda r: (r, 0))],
        out_shape=[jax.ShapeDtypeStruct((R, W), F32), jax.ShapeDtypeStruct((R, LANE), F32)],
        scratch_shapes=[pltpu.VMEM((nb + 2 * hal, W), F32)],
        compiler_params=_params(("parallel",)),
        name="gdn_conv",
    )(P, P, P, P, w_conv, gvec)


def _split3(x):
    h = x.astype(BF16)
    r = x - h.astype(F32)
    m = r.astype(BF16)
    l = (r - m.astype(F32)).astype(BF16)
    return h, m, l


def _gdn_b_kernel(qkvn_ref, gates_ref, u_ref, w_ref, qg_ref, kgt_ref, qk_ref):
    nb = GDN_BLOCK
    nh = GDN_HEADS
    ri = lax.broadcasted_iota(jnp.int32, (nb, nb), 0)
    ci = lax.broadcasted_iota(jnp.int32, (nb, nb), 1)
    same = (ri >> 6) == (ci >> 6)
    low = jnp.logical_and(same, ri >= ci)
    upp = jnp.logical_and(same, ri <= ci)
    slow = jnp.logical_and(same, ri > ci)
    supp = jnp.logical_and(same, ri < ci)
    ltri = jnp.where(low, 1.0, 0.0).astype(BF16)
    utri = jnp.where(upp, 1.0, 0.0).astype(BF16)
    eye = jnp.where(ri == ci, 1.0, 0.0)
    pair = (ri >> 1) == (ci >> 1)
    offs = [jnp.logical_and((ri >> (lv + 1)) == (ci >> (lv + 1)), (ri >> lv) != (ci >> lv))
            for lv in range(1, int(math.log2(GDN_CHUNK)))]

    gt = gates_ref[...]
    g3 = _split3(gt)
    g3t = _split3(gt.T)
    cum_c = (sum(_mm(ltri, p) for p in reversed(g3)), sum(_mm(utri, p) for p in reversed(g3)))
    cum_r = (sum(_mm(p, utri) for p in reversed(g3t)), sum(_mm(p, ltri) for p in reversed(g3t)))
    tot_c = cum_c[0] + cum_c[1] - gt

    scale = GDN_DK ** -0.5
    insts = [(d, h) for d in range(2) for h in range(nh)]
    a_all, tinv_all, rhs_all = [], [], []
    for d, h in insts:
        mask, smask = (low, slow) if d == 0 else (upp, supp)
        ib = MISC_BETA + d * nh + h
        ig = MISC_DEC + d * nh + h
        beta = gt[:, ib:ib + 1]
        gc = cum_c[d][:, ig:ig + 1]
        gr = cum_r[d][ig:ig + 1, :]
        gl = tot_c[:, ig:ig + 1]
        q = qkvn_ref[:, h * LANE:(h + 1) * LANE]
        k = qkvn_ref[:, (nh + h) * LANE:(nh + h + 1) * LANE]
        v = qkvn_ref[:, (2 * nh + h) * LANE:(2 * nh + h + 1) * LANE]
        decay = jnp.where(mask, jnp.exp(jnp.where(mask, gc - gr, 0.0)), 0.0)
        kb = k * beta
        k16 = k.astype(BF16)
        a = jnp.where(smask, _mm_nt(kb.astype(BF16), k16) * decay, 0.0)
        a_all.append(a.astype(BF16))
        tinv_all.append(eye - jnp.where(pair, a, 0.0))
        rhs_all.append(jnp.concatenate([v * beta, kb * jnp.exp(gc)], axis=1).astype(BF16))
        hs = slice(h * LANE, (h + 1) * LANE)
        qs = q * scale
        qk_ref[d, :, h * nb:(h + 1) * nb] = (_mm_nt(qs.astype(BF16), k16) * decay).astype(BF16)
        qg_ref[d, :, hs] = (qs * jnp.exp(gc)).astype(BF16)
        kgt_ref[d, hs, :] = (k * jnp.exp(gl - gc)).T.astype(BF16)

    zero16 = jnp.zeros((nb, nb), BF16)
    for off in offs:
        for i in range(len(insts)):
            t16 = tinv_all[i].astype(BF16)
            ta = _mm(t16, jnp.where(off, a_all[i], zero16))
            tinv_all[i] = tinv_all[i] - _mm(ta.astype(BF16), t16)
    for i, (d, h) in enumerate(insts):
        x = _mm(tinv_all[i].astype(BF16), rhs_all[i])
        hs = slice(h * LANE, (h + 1) * LANE)
        u_ref[d, :, hs] = x[:, :LANE]
        w_ref[d, :, hs] = x[:, LANE:].astype(BF16)


def _gdn_local(qkvn, gates):
    R, W = qkvn.shape
    nb = GDN_BLOCK
    HW = GDN_HEADS * LANE
    return pl.pallas_call(
        _gdn_b_kernel,
        grid=(R // nb,),
        in_specs=[pl.BlockSpec((nb, W), lambda r: (r, 0)),
                  pl.BlockSpec((nb, LANE), lambda r: (r, 0))],
        out_specs=[pl.BlockSpec((2, nb, HW), lambda r: (0, r, 0)),
                   pl.BlockSpec((2, nb, HW), lambda r: (0, r, 0)),
                   pl.BlockSpec((2, nb, HW), lambda r: (0, r, 0)),
                   pl.BlockSpec((2, HW, nb), lambda r: (0, 0, r)),
                   pl.BlockSpec((2, nb, GDN_HEADS * nb), lambda r: (0, r, 0))],
        out_shape=[jax.ShapeDtypeStruct((2, R, HW), F32),
                   jax.ShapeDtypeStruct((2, R, HW), BF16),
                   jax.ShapeDtypeStruct((2, R, HW), BF16),
                   jax.ShapeDtypeStruct((2, HW, R), BF16),
                   jax.ShapeDtypeStruct((2, R, GDN_HEADS * nb), BF16)],
        compiler_params=_params(("parallel",), VMEM_BIG),
        name="gdn_local",
    )(qkvn, gates)


def _gdn_c_kernel(*refs):
    nb = GDN_BLOCK
    nh = GDN_HEADS
    ins, (of_ref, ob_ref, s_ref, vn_ref) = (refs[0:6], refs[6:12]), refs[12:]
    o_refs = (of_ref, ob_ref)

    @pl.when(pl.program_id(1) == 0)
    def _():
        s_ref[...] = jnp.zeros_like(s_ref)
        vn_ref[...] = jnp.zeros_like(vn_ref)

    rowi = lax.broadcasted_iota(jnp.int32, (nb, 1), 0)
    nchunk = nb // GDN_CHUNK
    for step in range(nchunk):
        for d in range(2):
            u_ref, w_ref, qg_ref, kgt_ref, qk_ref, gates_ref = ins[d]
            c = step if d == 0 else nchunk - 1 - step
            rs = slice(c * GDN_CHUNK, (c + 1) * GDN_CHUNK)
            decay_all = jnp.exp(jnp.sum(gates_ref[rs, :], axis=0, keepdims=True))
            cmask = jnp.logical_and(rowi >= c * GDN_CHUNK, rowi < (c + 1) * GDN_CHUNK)
            for h in range(nh):
                hs = slice(h * LANE, (h + 1) * LANE)
                ig = MISC_DEC + d * nh + h
                s = s_ref[d, h]
                s16 = s.astype(BF16)
                v_new = u_ref[0, rs, hs] - _mm(w_ref[0, rs, hs], s16)
                vn_ref[d, h, rs, :] = v_new.astype(BF16)
                vn = vn_ref[d, h]
                o_refs[d][rs, hs] = _mm(qg_ref[0, rs, hs], s16) + _mm(qk_ref[0, rs, h * nb:(h + 1) * nb], vn)
                v_cur = jnp.where(cmask, vn, jnp.zeros_like(vn))
                s_ref[d, h] = s * decay_all[:, ig:ig + 1] + _mm(kgt_ref[0, hs, :], v_cur)


def _gdn_scan(u, w, qg, kgt, qk, gates, geom):
    nb = GDN_BLOCK
    HW = GDN_HEADS * LANE
    B, RL, R = geom["B"], geom["RL"], geom["R"]
    per = geom["T"] // nb
    nlat = RL // nb

    def blk(d):
        return lambda b, j: jnp.where(j == 0, nlat + b, b * per + (j - 1 if d == 0 else per - j))

    def dir_specs(d):
        bk = blk(d)
        tok = lambda w_: pl.BlockSpec((1, nb, w_), lambda b, j: (d, bk(b, j), 0))
        return [tok(HW), tok(HW), tok(HW),
                pl.BlockSpec((1, HW, nb), lambda b, j: (d, 0, bk(b, j))),
                tok(GDN_HEADS * nb),
                pl.BlockSpec((nb, LANE), lambda b, j: (bk(b, j), 0))]

    args = (u, w, qg, kgt, qk, gates)
    return pl.pallas_call(
        _gdn_c_kernel,
        grid=(B, per + 1),
        in_specs=dir_specs(0) + dir_specs(1),
        out_specs=[pl.BlockSpec((nb, HW), lambda b, j: (blk(0)(b, j), 0)),
                   pl.BlockSpec((nb, HW), lambda b, j: (blk(1)(b, j), 0))],
        out_shape=[jax.ShapeDtypeStruct((R, HW), F32), jax.ShapeDtypeStruct((R, HW), F32)],
        scratch_shapes=[pltpu.VMEM((2, GDN_HEADS, GDN_DK, GDN_DV), F32),
                        pltpu.VMEM((2, GDN_HEADS, nb, GDN_DV), BF16)],
        compiler_params=_params(("parallel", "arbitrary")),
        name="gdn_scan",
    )(*args, *args)


def _gdn_out_kernel(of_ref, ob_ref, og_ref, g_ref, o_ref):
    o = of_ref[...] + ob_ref[...]
    gate = _silu(og_ref[...].astype(F32))
    g = g_ref[...]
    for h in range(GDN_HEADS):
        hs = slice(h * LANE, (h + 1) * LANE)
        o_ref[:, hs] = (_rms(o[:, hs], g) * gate[:, hs]).astype(BF16)


def _gdn_out(o_f, o_b, P, g_out, rows):
    HW = GDN_HEADS * LANE
    bm = 512 if rows % 512 == 0 else 256
    return pl.pallas_call(
        _gdn_out_kernel,
        grid=(rows // bm,),
        in_specs=[pl.BlockSpec((bm, HW), lambda i: (i, 0)),
                  pl.BlockSpec((bm, HW), lambda i: (i, 0)),
                  pl.BlockSpec((bm, HW), lambda i: (i, OFF_OG // HW)),
                  pl.BlockSpec((1, LANE), lambda i: (0, 0))],
        out_specs=pl.BlockSpec((bm, HW), lambda i: (i, 0)),
        out_shape=jax.ShapeDtypeStruct((rows, HW), BF16),
        compiler_params=_params(("parallel",)),
        name="gdn_out",
    )(o_f, o_b, P, g_out.reshape(1, LANE))


def _residual_rows(x_ref, gate, o_ref, post=None):
    def body(r, carry):
        rs = pl.ds(pl.multiple_of(r * NORM_ROWS, NORM_ROWS), NORM_ROWS)
        v = x_ref[rs, :] + gate * o_ref[rs, :]
        o_ref[rs, :] = v if post is None else post(v)
        return carry

    lax.fori_loop(0, x_ref.shape[0] // NORM_ROWS, body, 0)


def _merge_kernel(xa_ref, xb_ref, mod_ref, g_ref, oa_ref, ob_ref, oc_ref, wga_ref, wgb_ref, wgc_ref, wua_ref,
                  wub_ref, wuc_ref, wo_ref, o_ref, hx_ref, *, nlat):
    j = pl.program_id(1)
    m = mod_ref[0]

    @pl.when(j == 0)
    def _():
        _for_row_source(nlat, xa_ref, xb_ref,
                        lambda x_ref: _norm_mod_rows(x_ref, g_ref[...], m[0:1], m[1:2], hx_ref))
        o_ref[...] = jnp.zeros_like(o_ref)

    hx = hx_ref[...]
    oa, ob, oc = oa_ref[...], ob_ref[...], oc_ref[...]
    hc = wo_ref.shape[0]
    ga, gb, gc = _mm(hx, wga_ref[...]), _mm(hx, wgb_ref[...]), _mm(hx, wgc_ref[...])
    ua, ub, uc = _mm(oa, wua_ref[...]), _mm(ob, wub_ref[...]), _mm(oc, wuc_ref[...])
    acc = None
    for cs in (slice(0, hc // 2), slice(hc // 2, hc)):
        y = _sigmoid(ga[:, cs]) * ua[:, cs] + _sigmoid(gb[:, cs]) * ub[:, cs] + _sigmoid(gc[:, cs]) * uc[:, cs]
        t = _mm(y.astype(BF16), wo_ref[cs, :])
        acc = t if acc is None else acc + t
    o_ref[...] += acc

    @pl.when(j == pl.num_programs(1) - 1)
    def _():
        _for_row_source(nlat, xa_ref, xb_ref, lambda x_ref: _residual_rows(x_ref, m[2:3], o_ref))


def _merge(src, mod_l, g1, oa, ob, oc, wg, wua, wub, wuc, wo, geom, rows):
    D = src[0].shape[1]
    bm = _pick_block(512, geom["T"], geom["RC"])
    hc = 512
    nj = D // hc
    sel = geom["mod_sel"](bm)
    rowb = lambda w_: pl.BlockSpec((bm, w_), lambda i, j: (i, 0))
    gate_w = lambda br: pl.BlockSpec((D, hc), lambda i, j: (0, br * nj + j))
    nlat, spec_a, spec_b = _row_source_specs(src, bm, D, geom["RL"])
    return pl.pallas_call(
        functools.partial(_merge_kernel, nlat=nlat),
        grid=(rows // bm, nj),
        in_specs=[spec_a, spec_b,
                  pl.BlockSpec((1, N_MOD, D), lambda i, j: (sel(i), 0, 0)),
                  pl.BlockSpec((1, D), lambda i, j: (0, 0)),
                  rowb(oa.shape[1]), rowb(ob.shape[1]), rowb(oc.shape[1]),
                  gate_w(0), gate_w(1), gate_w(2),
                  pl.BlockSpec((wua.shape[0], hc), lambda i, j: (0, j)),
                  pl.BlockSpec((wub.shape[0], hc), lambda i, j: (0, j)),
                  pl.BlockSpec((wuc.shape[0], hc), lambda i, j: (0, j)),
                  pl.BlockSpec((hc, D), lambda i, j: (j, 0))],
        out_specs=rowb(D),
        out_shape=jax.ShapeDtypeStruct((rows, D), F32),
        scratch_shapes=[pltpu.VMEM((bm, D), BF16)],
        compiler_params=_params(("parallel", "arbitrary"), VMEM_BIG),
        name="merge",
    )(src[0], src[1], mod_l, g1.reshape(1, D), oa, ob, oc, wg, wg, wg, wua, wub, wuc, wo)


def _mlp_kernel(x_ref, mod_ref, g_ref, w1_ref, w2_ref, gf_ref, o_ref, hx_ref, *, final):
    j = pl.program_id(1)
    m = mod_ref[0]

    @pl.when(j == 0)
    def _():
        _norm_mod_rows(x_ref, g_ref[...], m[3:4], m[4:5], hx_ref)
        o_ref[...] = jnp.zeros_like(o_ref)

    hx = hx_ref[...]
    fc = w2_ref.shape[0]
    h1 = _mm(hx, w1_ref[...].astype(BF16))
    acc = None
    for cs in (slice(0, fc // 2), slice(fc // 2, fc)):
        h = jnp.maximum(h1[:, cs], 0.0)
        t = _mm((h * h).astype(BF16), w2_ref[cs, :].astype(BF16))
        acc = t if acc is None else acc + t
    o_ref[...] += acc

    @pl.when(j == pl.num_programs(1) - 1)
    def _():
        gf = gf_ref[...]
        _residual_rows(x_ref, m[5:6], o_ref, (lambda v: _rms(v, gf)) if final else None)


def _mlp(x1, mod_l, g2, w1, w2, g_final, geom, final):
    rows, D = x1.shape
    F = w1.shape[1]
    bm = _pick_block(1024, geom["T"], geom["RC"])
    fc = 512
    sel = geom["mod_sel"](bm)
    return pl.pallas_call(
        functools.partial(_mlp_kernel, final=final),
        grid=(rows // bm, F // fc),
        in_specs=[pl.BlockSpec((bm, D), lambda i, j: (i, 0), pipeline_mode=pl.Buffered(1)),
                  pl.BlockSpec((1, N_MOD, D), lambda i, j: (sel(i), 0, 0)),
                  pl.BlockSpec((1, D), lambda i, j: (0, 0)),
                  pl.BlockSpec((D, fc), lambda i, j: (0, j)),
                  pl.BlockSpec((fc, D), lambda i, j: (j, 0)),
                  pl.BlockSpec((1, D), lambda i, j: (0, 0))],
        out_specs=pl.BlockSpec((bm, D), lambda i, j: (i, 0)),
        out_shape=jax.ShapeDtypeStruct((rows, D), F32),
        scratch_shapes=[pltpu.VMEM((bm, D), BF16)],
        compiler_params=_params(("parallel", "arbitrary"), VMEM_BIG),
        name="mlp",
    )(x1, mod_l, g2.reshape(1, D), w1, w2, g_final.reshape(1, D))


def _rope_tables(T, bm):
    t = np.arange(T)
    row, col = t // GRID_W, t % GRID_W

    def table(n_rot):
        n = n_rot // 2
        half = n // 2
        inv = ROPE_THETA ** (-np.arange(half, dtype=np.float64) / half)
        cos = np.zeros((T + bm, LANE))
        sin = np.zeros((T + bm, LANE))
        for s, pos in enumerate((row, col)):
            ang = pos[:, None].astype(np.float32).astype(np.float64) * inv.astype(np.float32)[None, :]
            ang = ang.astype(np.float32).astype(np.float64)
            c, sn = np.cos(ang), np.sin(ang)
            cos[:T, s * n:s * n + half] = c
            cos[:T, s * n + half:(s + 1) * n] = c
            sin[:T, s * n:s * n + half] = -sn
            sin[:T, s * n + half:(s + 1) * n] = sn
        cos[T:, :] = 1.0
        return jnp.asarray(cos, F32), jnp.asarray(sin, F32)

    cg, sg = table(GQA_HD)
    cm, sm = table(MLA_ROPE)
    return cg, sg, cm, sm


def _prep_layer_weights(w_in, w_qb, w_kvb):
    D = w_in.shape[0]
    splits = (MLA_Q_RANK, MLA_KV_RANK, MLA_ROPE, GQA_HEADS * GQA_HD, GQA_KV_HEADS * GQA_HD, GQA_KV_HEADS * GQA_HD,
              GDN_HEADS * (2 * GDN_DK + GDN_DV), 2 * GDN_HEADS, 2 * GDN_HEADS, GDN_HEADS * GDN_DV, 3 * D)
    off = np.cumsum((0,) + splits)
    cq, ckv, kpe, gq, gk, gv, qkv, beta, dec, og, bg = (w_in[:, off[i]:off[i + 1]] for i in range(len(splits)))
    used = OFF_MISC + MLA_ROPE + 4 * GDN_HEADS
    wa = jnp.concatenate([a.astype(BF16) for a in (gq, cq, ckv, gk, gv, og, qkv, kpe, beta, dec)]
                         + [jnp.zeros((D, N_SMALL - used), BF16)], axis=1)
    wg = bg.astype(BF16)
    qb = w_qb.reshape(-1, MLA_HEADS, MLA_NOPE + MLA_ROPE)
    qpe = jnp.pad(qb[:, :, MLA_NOPE:], ((0, 0), (0, 0), (0, LANE - MLA_ROPE)))
    wqb = jnp.concatenate([qb[:, :, :MLA_NOPE].reshape(-1, MLA_HEADS * LANE),
                           qpe.reshape(-1, MLA_HEADS * LANE)], axis=1).astype(BF16)
    kvb = w_kvb.reshape(-1, MLA_HEADS, MLA_NOPE + MLA_V)
    wkvb = jnp.concatenate([kvb[:, :, :MLA_NOPE].reshape(-1, MLA_HEADS * LANE),
                            kvb[:, :, MLA_NOPE:].reshape(-1, MLA_HEADS * LANE)], axis=1).astype(BF16)
    return wa, wg, wqb, wkvb


def kernel(x, c, ctx, c_ctx, w_mod, b_mod, g_norm1, w_in, g_mla_q, w_mla_qb, g_mla_kv, w_mla_kvb, g_gqa_q, g_gqa_k,
           w_conv, a_log, dt_bias, g_gdn_out, w_up_a, w_up_b, w_up_c, w_out, g_norm2, w_ff1, w_ff2, g_final):
    B, T, D = x.shape
    C = ctx.shape[1]
    L = w_mod.shape[0]
    assert C == GDN_BLOCK and T % GDN_BLOCK == 0 and T % GRID_W == 0 and B < 8
    RL, RC = B * T, B * C
    geom = dict(B=B, T=T, C=C, RL=RL, RC=RC, R=RL + RC)
    geom["mod_sel"] = lambda bm: (lambda i: jnp.where(i < RL // bm, i // (T // bm), B))

    mod = _modulation(c, c_ctx, w_mod, b_mod)
    tabs = _rope_tables(T, _pick_block(512, T, RC))
    src = (x.reshape(RL, D), ctx.reshape(RC, D), 0)
    mla_scale = (MLA_NOPE + MLA_ROPE) ** -0.5
    gqa_scale = GQA_HD ** -0.5

    for l in range(L):
        last = l == L - 1
        wa, wg, wqb, wkvb = _prep_layer_weights(w_in[l], w_mla_qb[l], w_mla_kvb[l])
        P = _in_proj(src, mod[l], g_norm1[l], wa, geom)
        qm, km, vm, qg, kg, vg = _attn_prep(P, g_mla_q[l], g_mla_kv[l], wqb, wkvb, g_gqa_q[l], g_gqa_k[l], tabs, geom)
        oa = _attention(qm, km, vm, mla_scale, geom, "attn_mla")
        ob = _attention(qg, kg, vg, gqa_scale, geom, "attn_gqa")
        qkvn, gates = _gdn_conv(P, w_conv[l], a_log[l], dt_bias[l], geom)
        u, w, qgd, kgt, qk = _gdn_local(qkvn, gates)
        o_f, o_b = _gdn_scan(u, w, qgd, kgt, qk, gates, geom)
        rows = RL if last else RL + RC
        oc = _gdn_out(o_f, o_b, P, g_gdn_out[l], rows)
        if not last:
            oa = jnp.concatenate([oa, _attention_ctx(qm, km, vm, mla_scale, geom, "attn_mla_ctx")], axis=0)
            ob = jnp.concatenate([ob, _attention_ctx(qg, kg, vg, gqa_scale, geom, "attn_gqa_ctx")], axis=0)
        x1 = _merge(src, mod[l], g_norm1[l], oa, ob, oc, wg, w_up_a[l].astype(BF16), w_up_b[l].astype(BF16),
                    w_up_c[l].astype(BF16), w_out[l].astype(BF16), geom, rows)
        xs = _mlp(x1, mod[l], g_norm2[l], w_ff1[l], w_ff2[l], g_final, geom, last)
        src = (xs, xs, RL)
    return xs.reshape(B, T, D)
```

```python
import functools
import math

import numpy as np
import jax
import jax.numpy as jnp
from jax import lax
from jax.experimental import pallas as pl
from jax.experimental.pallas import tpu as pltpu

F32 = jnp.float32
BF16 = jnp.bfloat16

GRID_W = 64
EPS = 1e-6
ROPE_THETA = 10000.0
MLA_HEADS, MLA_Q_RANK, MLA_KV_RANK, MLA_NOPE, MLA_ROPE, MLA_V = 4, 512, 512, 128, 64, 128
GQA_HEADS, GQA_KV_HEADS, GQA_HD = 8, 2, 128
GDN_HEADS, GDN_DK, GDN_DV, GDN_CONV, GDN_CHUNK = 4, 128, 128, 5, 64
N_MOD = 6
LANE = 128
GDN_BLOCK = 256
HALO_ROWS = 16
VMEM_BIG = 56 * 1024 * 1024

OFF_GQ, OFF_CQ, OFF_CKV, OFF_GK, OFF_GV, OFF_OG, OFF_QKV, OFF_MISC = 0, 1024, 1536, 2048, 2304, 2560, 3072, 4608
N_SMALL = 5120
MISC_BETA, MISC_DEC = 64, 72


def _mm(a, b):
    return jnp.dot(a, b, preferred_element_type=F32)


def _mm_nt(a, b):
    return lax.dot_general(a, b, (((1,), (1,)), ((), ())), preferred_element_type=F32)


def _silu(x):
    return x / (1.0 + jnp.exp(-x))


def _sigmoid(x):
    return 1.0 / (1.0 + jnp.exp(-x))


def _rms(x, g):
    return x * lax.rsqrt(jnp.mean(x * x, axis=-1, keepdims=True) + EPS) * g


def _norm_mod(x, g, shift, scale):
    return _rms(x, g) * (1.0 + scale) + shift


def _params(sem, vmem=None):
    kw = dict(dimension_semantics=sem)
    if vmem is not None:
        kw["vmem_limit_bytes"] = vmem
    return pltpu.CompilerParams(**kw)


def _pick_block(pref, *sizes):
    b = pref
    while any(s % b for s in sizes):
        b //= 2
    return b


def _mod_kernel(s_ref, w_ref, b_ref, o_ref):
    s = _silu(s_ref[...])
    o_ref[0] = _mm(s.astype(BF16), w_ref[0].astype(BF16)) + b_ref[0]


def _modulation(c, c_ctx, w_mod, b_mod):
    L, D, N = w_mod.shape
    B = c.shape[0]
    rows = jnp.zeros((8, D), F32).at[:B].set(c).at[B].set(c_ctx)
    bn = 2048
    out = pl.pallas_call(
        _mod_kernel,
        grid=(L, N // bn),
        in_specs=[pl.BlockSpec((8, D), lambda l, j: (0, 0)),
                  pl.BlockSpec((1, D, bn), lambda l, j: (l, 0, j)),
                  pl.BlockSpec((1, 1, bn), lambda l, j: (l, 0, j))],
        out_specs=pl.BlockSpec((1, 8, bn), lambda l, j: (l, 0, j)),
        out_shape=jax.ShapeDtypeStruct((L, 8, N), F32),
        compiler_params=_params(("parallel", "parallel"), VMEM_BIG),
        name="adaln_mod",
    )(rows, w_mod, b_mod.reshape(L, 1, N))
    return out.reshape(L, 8, N_MOD, D)


NORM_ROWS = 64


def _norm_mod_rows(x_ref, g, shift, scale, hx_ref):
    def body(r, carry):
        rs = pl.ds(pl.multiple_of(r * NORM_ROWS, NORM_ROWS), NORM_ROWS)
        hx_ref[rs, :] = _norm_mod(x_ref[rs, :], g, shift, scale).astype(BF16)
        return carry

    lax.fori_loop(0, x_ref.shape[0] // NORM_ROWS, body, 0)


def _row_source_specs(src, bm, D, n_latent_rows):
    nlat = n_latent_rows // bm
    off_b = src[2] // bm
    spec_a = pl.BlockSpec((bm, D), lambda i, j: (jnp.minimum(i, nlat - 1), 0))
    spec_b = pl.BlockSpec((bm, D), lambda i, j: (off_b + jnp.maximum(i - nlat, 0), 0), pipeline_mode=pl.Buffered(1))
    return nlat, spec_a, spec_b


def _for_row_source(nlat, xa_ref, xb_ref, fn):
    i = pl.program_id(0)

    @pl.when(i < nlat)
    def _():
        fn(xa_ref)

    @pl.when(i >= nlat)
    def _():
        fn(xb_ref)


def _inproj_kernel(xa_ref, xb_ref, mod_ref, g_ref, w_ref, o_ref, hx_ref, *, nlat):
    @pl.when(pl.program_id(1) == 0)
    def _():
        m = mod_ref[0]
        _for_row_source(nlat, xa_ref, xb_ref,
                        lambda x_ref: _norm_mod_rows(x_ref, g_ref[...], m[0:1], m[1:2], hx_ref))

    o_ref[...] = _mm(hx_ref[...], w_ref[...]).astype(BF16)


def _in_proj(src, mod_l, g1, wa, geom):
    R, D = geom["R"], src[0].shape[1]
    N = wa.shape[1]
    bm = _pick_block(1024, geom["T"], geom["RC"])
    bn = 1024
    sel = geom["mod_sel"](bm)
    nlat, spec_a, spec_b = _row_source_specs(src, bm, D, geom["RL"])
    return pl.pallas_call(
        functools.partial(_inproj_kernel, nlat=nlat),
        grid=(R // bm, N // bn),
        in_specs=[spec_a, spec_b,
                  pl.BlockSpec((1, N_MOD, D), lambda i, j: (sel(i), 0, 0)),
                  pl.BlockSpec((1, D), lambda i, j: (0, 0)),
                  pl.BlockSpec((D, bn), lambda i, j: (0, j))],
        out_specs=pl.BlockSpec((bm, bn), lambda i, j: (i, j)),
        out_shape=jax.ShapeDtypeStruct((R, N), BF16),
        scratch_shapes=[pltpu.VMEM((bm, D), BF16)],
        compiler_params=_params(("parallel", "arbitrary"), VMEM_BIG),
        name="in_proj",
    )(src[0], src[1], mod_l, g1.reshape(1, D), wa)


def _prep_kernel(gq_ref, cq_ref, ckv_ref, gk_ref, gv_ref, misc_ref, gmq_ref, gmkv_ref, wqb_ref, wkvb_ref,
                 ggq_ref, ggk_ref, cosg_ref, sing_ref, cosm_ref, sinm_ref,
                 qm_ref, km_ref, vm_ref, qg_ref, kg_ref, vg_ref):
    lane = lax.broadcasted_iota(jnp.int32, (1, LANE), 1)
    cosm, sinm = cosm_ref[...], sinm_ref[...]
    cosg, sing = cosg_ref[...], sing_ref[...]
    first_m = (lane & 31) < 16
    first_g = (lane & 63) < 32
    ones_col = jnp.where(lane == 0, 1.0, 0.0).astype(BF16) + jnp.zeros((cq_ref.shape[0], LANE), BF16)

    def rope_m(x):
        partner = jnp.where(first_m, pltpu.roll(x, LANE - 16, 1), pltpu.roll(x, 16, 1))
        return x * cosm + partner * sinm

    def rope_g(x):
        partner = jnp.where(first_g, pltpu.roll(x, LANE - 32, 1), pltpu.roll(x, 32, 1))
        return x * cosg + partner * sing

    qa = _mm(_rms(cq_ref[...].astype(F32), gmq_ref[...]).astype(BF16), wqb_ref[...])
    kva = _mm(_rms(ckv_ref[...].astype(F32), gmkv_ref[...]).astype(BF16), wkvb_ref[...])
    kpe = rope_m(jnp.where(lane < MLA_ROPE, misc_ref[...].astype(F32), 0.0)).astype(BF16)
    nh = MLA_HEADS * LANE
    for h in range(MLA_HEADS):
        hs = slice(h * LANE, (h + 1) * LANE)
        ps = slice(nh + h * LANE, nh + (h + 1) * LANE)
        qm_ref[h, :, 0:LANE] = qa[:, hs].astype(BF16)
        qm_ref[h, :, LANE:2 * LANE] = rope_m(qa[:, ps]).astype(BF16)
        km_ref[h, :, 0:LANE] = kva[:, hs].astype(BF16)
        km_ref[h, :, LANE:2 * LANE] = kpe
        vm_ref[h, :, 0:LANE] = kva[:, ps].astype(BF16)
        vm_ref[h, :, LANE:2 * LANE] = ones_col

    ggq, ggk = ggq_ref[...], ggk_ref[...]
    for h in range(GQA_HEADS):
        hs = slice(h * LANE, (h + 1) * LANE)
        qg_ref[h] = rope_g(_rms(gq_ref[:, hs].astype(F32), ggq)).astype(BF16)
    for h in range(GQA_KV_HEADS):
        hs = slice(h * LANE, (h + 1) * LANE)
        kg_ref[h] = rope_g(_rms(gk_ref[:, hs].astype(F32), ggk)).astype(BF16)
        vg_ref[h, :, 0:LANE] = gv_ref[:, hs]
        vg_ref[h, :, LANE:2 * LANE] = ones_col


def _attn_prep(P, gmq, gmkv, wqb, wkvb, ggq, ggk, tabs, geom):
    R = P.shape[0]
    bm = _pick_block(512, geom["T"], geom["RC"])
    nlat = geom["RL"] // bm
    per = geom["T"] // bm
    tsel = lambda i: jnp.where(i < nlat, i % per, per)
    col = lambda w, off: pl.BlockSpec((bm, w), lambda i: (i, off // w))
    full = lambda a: pl.BlockSpec(a.shape, lambda i: (0,) * a.ndim)
    tab = pl.BlockSpec((bm, LANE), lambda i: (tsel(i), 0))
    outs = [((MLA_HEADS, R, 2 * LANE), 2 * LANE), ((MLA_HEADS, R, 2 * LANE), 2 * LANE),
            ((MLA_HEADS, R, 2 * LANE), 2 * LANE), ((GQA_HEADS, R, LANE), LANE), ((GQA_KV_HEADS, R, LANE), LANE),
            ((GQA_KV_HEADS, R, 2 * LANE), 2 * LANE)]
    gmq, gmkv, ggq, ggk = (a.reshape(1, -1) for a in (gmq, gmkv, ggq, ggk))
    return pl.pallas_call(
        _prep_kernel,
        grid=(R // bm,),
        in_specs=[col(1024, OFF_GQ), col(512, OFF_CQ), col(512, OFF_CKV), col(256, OFF_GK), col(256, OFF_GV),
                  col(LANE, OFF_MISC), full(gmq), full(gmkv), full(wqb), full(wkvb), full(ggq), full(ggk),
                  tab, tab, tab, tab],
        out_specs=[pl.BlockSpec((s[0], bm, w), lambda i: (0, i, 0)) for s, w in outs],
        out_shape=[jax.ShapeDtypeStruct(s, BF16) for s, _ in outs],
        compiler_params=_params(("parallel",), VMEM_BIG),
        name="attn_prep",
    )(P, P, P, P, P, P, gmq, gmkv, wqb, wkvb, ggq, ggk, *tabs)


def _softmax_pv(q, kv_chunks, s_ref, o_ref, exp2_scale):
    sub = min(q.shape)
    for r0 in range(0, q.shape[0], sub):
        rows = slice(r0, r0 + sub)
        qr = q[rows]
        m_acc = None
        off = 0
        for k_ref, _, st, n in kv_chunks:
            s = _mm_nt(qr, k_ref[0, st:st + n, :])
            s_ref[rows, off:off + n] = s
            for t in range(n // LANE):
                tile = s[:, t * LANE:(t + 1) * LANE]
                m_acc = tile if m_acc is None else jnp.maximum(m_acc, tile)
            off += n
        m = jnp.max(m_acc, axis=-1, keepdims=True)
        acc = None
        off = 0
        for _, v_ref, st, n in kv_chunks:
            p = jnp.exp2((s_ref[rows, off:off + n] - m) * exp2_scale).astype(BF16)
            pv = _mm(p, v_ref[0, st:st + n, :])
            acc = pv if acc is None else acc + pv
            off += n
        o_ref[rows, :] = (acc[:, :LANE] / acc[:, LANE:LANE + 1]).astype(BF16)


def _attn_kernel(q_ref, kx_ref, kc_ref, vx_ref, vc_ref, o_ref, s_ref, *, exp2_scale, kchunk):
    chunks = [(kx_ref, vx_ref, i * kchunk, kchunk) for i in range(kx_ref.shape[1] // kchunk)]
    chunks.append((kc_ref, vc_ref, 0, kc_ref.shape[1]))
    _softmax_pv(q_ref[0], chunks, s_ref, o_ref, exp2_scale)


def _attn_ctx_kernel(q_ref, kc_ref, vc_ref, o_ref, s_ref, *, exp2_scale):
    _softmax_pv(q_ref[0], [(kc_ref, vc_ref, 0, kc_ref.shape[1])], s_ref, o_ref, exp2_scale)


def _attention(q, k, v, scale, geom, name):
    H, _, dk = q.shape
    Hk, _, dve = v.shape
    grp = H // Hk
    B, T, C, RL = geom["B"], geom["T"], geom["C"], geom["RL"]
    bq = _pick_block(1024, T)
    nq = T // bq
    cb = RL // C
    return pl.pallas_call(
        functools.partial(_attn_kernel, exp2_scale=scale * math.log2(math.e), kchunk=256),
        grid=(B, H, nq),
        in_specs=[pl.BlockSpec((1, bq, dk), lambda b, h, j: (h, b * nq + j, 0)),
                  pl.BlockSpec((1, T, dk), lambda b, h, j: (h // grp, b, 0)),
                  pl.BlockSpec((1, C, dk), lambda b, h, j: (h // grp, cb + b, 0)),
                  pl.BlockSpec((1, T, dve), lambda b, h, j: (h // grp, b, 0)),
                  pl.BlockSpec((1, C, dve), lambda b, h, j: (h // grp, cb + b, 0))],
        out_specs=pl.BlockSpec((bq, LANE), lambda b, h, j: (b * nq + j, h)),
        out_shape=jax.ShapeDtypeStruct((RL, H * LANE), BF16),
        scratch_shapes=[pltpu.VMEM((bq, T + C), F32)],
        compiler_params=_params(("parallel", "parallel", "parallel"), VMEM_BIG),
        name=name,
    )(q, k, k, v, v)


def _attention_ctx(q, k, v, scale, geom, name):
    H, _, dk = q.shape
    Hk, _, dve = v.shape
    grp = H // Hk
    B, C, RL, RC = geom["B"], geom["C"], geom["RL"], geom["RC"]
    cb = RL // C
    return pl.pallas_call(
        functools.partial(_attn_ctx_kernel, exp2_scale=scale * math.log2(math.e)),
        grid=(B, H),
        in_specs=[pl.BlockSpec((1, C, dk), lambda b, h: (h, cb + b, 0)),
                  pl.BlockSpec((1, C, dk), lambda b, h: (h // grp, cb + b, 0)),
                  pl.BlockSpec((1, C, dve), lambda b, h: (h // grp, cb + b, 0))],
        out_specs=pl.BlockSpec((C, LANE), lambda b, h: (b, h)),
        out_shape=jax.ShapeDtypeStruct((RC, H * LANE), BF16),
        scratch_shapes=[pltpu.VMEM((C, C), F32)],
        compiler_params=_params(("parallel", "parallel")),
        name=name,
    )(q, k, v)


def _gdn_a_kernel(cur_ref, prev_ref, next_ref, misc_ref, wc_ref, gvec_ref, qkvn_ref, gates_ref, ext_ref,
                  *, nlat, per_seq):
    r = pl.program_id(0)
    pos = r % per_seq
    is_ctx = r >= nlat
    pf = jnp.where(jnp.logical_or(is_ctx, pos == 0), 0.0, 1.0).astype(F32)
    nf = jnp.where(jnp.logical_or(is_ctx, pos == per_seq - 1), 0.0, 1.0).astype(F32)
    nb = GDN_BLOCK
    hal = HALO_ROWS
    ext_ref[0:hal, :] = prev_ref[...].astype(F32) * pf
    ext_ref[hal:hal + nb, :] = cur_ref[...].astype(F32)
    ext_ref[hal + nb:2 * hal + nb, :] = next_ref[...].astype(F32) * nf
    pad = GDN_CONV // 2
    n_qk = 2 * GDN_HEADS
    for c in range(3 * GDN_HEADS):
        cs = slice(c * LANE, (c + 1) * LANE)
        acc = None
        for j in range(GDN_CONV):
            t = ext_ref[hal - pad + j:hal - pad + j + nb, cs] * wc_ref[j:j + 1, cs]
            acc = t if acc is None else acc + t
        y = _silu(acc)
        if c < n_qk:
            y = y * lax.rsqrt(jnp.sum(y * y, axis=-1, keepdims=True) + EPS)
        qkvn_ref[:, cs] = y

    lane = lax.broadcasted_iota(jnp.int32, (1, LANE), 1)
    raw = misc_ref[...].astype(F32)
    beta = _sigmoid(raw)
    z = raw + gvec_ref[1:2, :]
    softplus = jnp.maximum(z, 0.0) + jnp.log1p(jnp.exp(-jnp.abs(z)))
    g = -jnp.exp(gvec_ref[0:1, :]) * softplus
    is_beta = jnp.logical_and(lane >= MISC_BETA, lane < MISC_DEC)
    is_g = jnp.logical_and(lane >= MISC_DEC, lane < MISC_DEC + 2 * GDN_HEADS)
    gates_ref[...] = jnp.where(is_beta, beta, jnp.where(is_g, g, 0.0))


def _gdn_conv(P, w_conv, a_log, dt_bias, geom):
    R = P.shape[0]
    nb = GDN_BLOCK
    W = 3 * GDN_HEADS * LANE
    nblk = R // nb
    hal = HALO_ROWS
    sub = nb // hal
    gvec = jnp.zeros((2, LANE), F32)
    gvec = gvec.at[0, MISC_DEC:MISC_DEC + 2 * GDN_HEADS].set(a_log.reshape(-1))
    gvec = gvec.at[1, MISC_DEC:MISC_DEC + 2 * GDN_HEADS].set(dt_bias.reshape(-1))
    qc = OFF_QKV // W
    return pl.pallas_call(
        functools.partial(_gdn_a_kernel, nlat=geom["RL"] // nb, per_seq=geom["T"] // nb),
        grid=(nblk,),
        in_specs=[pl.BlockSpec((nb, W), lambda r: (r, qc)),
                  pl.BlockSpec((hal, W), lambda r: (jnp.maximum(r * sub - 1, 0), qc)),
                  pl.BlockSpec((hal, W), lambda r: (jnp.minimum((r + 1) * sub, R // hal - 1), qc)),
                  pl.BlockSpec((nb, LANE), lambda r: (r, OFF_MISC // LANE)),
                  pl.BlockSpec((GDN_CONV, W), lambda r: (0, 0)),
                  pl.BlockSpec((2, LANE), lambda r: (0, 0))],
        out_specs=[pl.BlockSpec((nb, W), lambda r: (r, 0)),
                   pl.BlockSpec((nb, LANE), lambda r: (r, 0))],
        out_shape=[jax.ShapeDtypeStruct((R, W), F32), jax.ShapeDtypeStruct((R, LANE), F32)],
        scratch_shapes=[pltpu.VMEM((nb + 2 * hal, W), F32)],
        compiler_params=_params(("parallel",)),
        name="gdn_conv",
    )(P, P, P, P, w_conv, gvec)


def _split3(x):
    h = x.astype(BF16)
    r = x - h.astype(F32)
    m = r.astype(BF16)
    l = (r - m.astype(F32)).astype(BF16)
    return h, m, l


def _gdn_b_kernel(qkvn_ref, gates_ref, u_ref, w_ref, qg_ref, kgt_ref, qk_ref):
    nb = GDN_BLOCK
    nh = GDN_HEADS
    ri = lax.broadcasted_iota(jnp.int32, (nb, nb), 0)
    ci = lax.broadcasted_iota(jnp.int32, (nb, nb), 1)
    same = (ri >> 6) == (ci >> 6)
    low = jnp.logical_and(same, ri >= ci)
    upp = jnp.logical_and(same, ri <= ci)
    slow = jnp.logical_and(same, ri > ci)
    supp = jnp.logical_and(same, ri < ci)
    ltri = jnp.where(low, 1.0, 0.0).astype(BF16)
    utri = jnp.where(upp, 1.0, 0.0).astype(BF16)
    eye = jnp.where(ri == ci, 1.0, 0.0)
    pair = (ri >> 1) == (ci >> 1)
    offs = [jnp.logical_and((ri >> (lv + 1)) == (ci >> (lv + 1)), (ri >> lv) != (ci >> lv))
            for lv in range(1, int(math.log2(GDN_CHUNK)))]

    gt = gates_ref[...]
    g3 = _split3(gt)
    g3t = _split3(gt.T)
    cum_c = (sum(_mm(ltri, p) for p in reversed(g3)), sum(_mm(utri, p) for p in reversed(g3)))
    cum_r = (sum(_mm(p, utri) for p in reversed(g3t)), sum(_mm(p, ltri) for p in reversed(g3t)))
    tot_c = cum_c[0] + cum_c[1] - gt

    scale = GDN_DK ** -0.5
    insts = [(d, h) for d in range(2) for h in range(nh)]
    a_all, tinv_all, rhs_all = [], [], []
    for d, h in insts:
        mask, smask = (low, slow) if d == 0 else (upp, supp)
        ib = MISC_BETA + d * nh + h
        ig = MISC_DEC + d * nh + h
        beta = gt[:, ib:ib + 1]
        gc = cum_c[d][:, ig:ig + 1]
        gr = cum_r[d][ig:ig + 1, :]
        gl = tot_c[:, ig:ig + 1]
        q = qkvn_ref[:, h * LANE:(h + 1) * LANE]
        k = qkvn_ref[:, (nh + h) * LANE:(nh + h + 1) * LANE]
        v = qkvn_ref[:, (2 * nh + h) * LANE:(2 * nh + h + 1) * LANE]
        decay = jnp.where(mask, jnp.exp(jnp.where(mask, gc - gr, 0.0)), 0.0)
        kb = k * beta
        k16 = k.astype(BF16)
        a = jnp.where(smask, _mm_nt(kb.astype(BF16), k16) * decay, 0.0)
        a_all.append(a.astype(BF16))
        tinv_all.append(eye - jnp.where(pair, a, 0.0))
        rhs_all.append(jnp.concatenate([v * beta, kb * jnp.exp(gc)], axis=1).astype(BF16))
        hs = slice(h * LANE, (h + 1) * LANE)
        qs = q * scale
        qk_ref[d, :, h * nb:(h + 1) * nb] = (_mm_nt(qs.astype(BF16), k16) * decay).astype(BF16)
        qg_ref[d, :, hs] = (qs * jnp.exp(gc)).astype(BF16)
        kgt_ref[d, hs, :] = (k * jnp.exp(gl - gc)).T.astype(BF16)

    zero16 = jnp.zeros((nb, nb), BF16)
    for off in offs:
        for i in range(len(insts)):
            t16 = tinv_all[i].astype(BF16)
            ta = _mm(t16, jnp.where(off, a_all[i], zero16))
            tinv_all[i] = tinv_all[i] - _mm(ta.astype(BF16), t16)
    for i, (d, h) in enumerate(insts):
        x = _mm(tinv_all[i].astype(BF16), rhs_all[i])
        hs = slice(h * LANE, (h + 1) * LANE)
        u_ref[d, :, hs] = x[:, :LANE]
        w_ref[d, :, hs] = x[:, LANE:].astype(BF16)


def _gdn_local(qkvn, gates):
    R, W = qkvn.shape
    nb = GDN_BLOCK
    HW = GDN_HEADS * LANE
    return pl.pallas_call(
        _gdn_b_kernel,
        grid=(R // nb,),
        in_specs=[pl.BlockSpec((nb, W), lambda r: (r, 0)),
                  pl.BlockSpec((nb, LANE), lambda r: (r, 0))],
        out_specs=[pl.BlockSpec((2, nb, HW), lambda r: (0, r, 0)),
                   pl.BlockSpec((2, nb, HW), lambda r: (0, r, 0)),
                   pl.BlockSpec((2, nb, HW), lambda r: (0, r, 0)),
                   pl.BlockSpec((2, HW, nb), lambda r: (0, 0, r)),
                   pl.BlockSpec((2, nb, GDN_HEADS * nb), lambda r: (0, r, 0))],
        out_shape=[jax.ShapeDtypeStruct((2, R, HW), F32),
                   jax.ShapeDtypeStruct((2, R, HW), BF16),
                   jax.ShapeDtypeStruct((2, R, HW), BF16),
                   jax.ShapeDtypeStruct((2, HW, R), BF16),
                   jax.ShapeDtypeStruct((2, R, GDN_HEADS * nb), BF16)],
        compiler_params=_params(("parallel",), VMEM_BIG),
        name="gdn_local",
    )(qkvn, gates)


def _gdn_c_kernel(*refs):
    nb = GDN_BLOCK
    nh = GDN_HEADS
    ins, (of_ref, ob_ref, s_ref, vn_ref) = (refs[0:6], refs[6:12]), refs[12:]
    o_refs = (of_ref, ob_ref)

    @pl.when(pl.program_id(1) == 0)
    def _():
        s_ref[...] = jnp.zeros_like(s_ref)
        vn_ref[...] = jnp.zeros_like(vn_ref)

    rowi = lax.broadcasted_iota(jnp.int32, (nb, 1), 0)
    nchunk = nb // GDN_CHUNK
    for step in range(nchunk):
        for d in range(2):
            u_ref, w_ref, qg_ref, kgt_ref, qk_ref, gates_ref = ins[d]
            c = step if d == 0 else nchunk - 1 - step
            rs = slice(c * GDN_CHUNK, (c + 1) * GDN_CHUNK)
            decay_all = jnp.exp(jnp.sum(gates_ref[rs, :], axis=0, keepdims=True))
            cmask = jnp.logical_and(rowi >= c * GDN_CHUNK, rowi < (c + 1) * GDN_CHUNK)
            for h in range(nh):
                hs = slice(h * LANE, (h + 1) * LANE)
                ig = MISC_DEC + d * nh + h
                s = s_ref[d, h]
                s16 = s.astype(BF16)
                v_new = u_ref[0, rs, hs] - _mm(w_ref[0, rs, hs], s16)
                vn_ref[d, h, rs, :] = v_new.astype(BF16)
                vn = vn_ref[d, h]
                o_refs[d][rs, hs] = _mm(qg_ref[0, rs, hs], s16) + _mm(qk_ref[0, rs, h * nb:(h + 1) * nb], vn)
                v_cur = jnp.where(cmask, vn, jnp.zeros_like(vn))
                s_ref[d, h] = s * decay_all[:, ig:ig + 1] + _mm(kgt_ref[0, hs, :], v_cur)


def _gdn_scan(u, w, qg, kgt, qk, gates, geom):
    nb = GDN_BLOCK
    HW = GDN_HEADS * LANE
    B, RL, R = geom["B"], geom["RL"], geom["R"]
    per = geom["T"] // nb
    nlat = RL // nb

    def blk(d):
        return lambda b, j: jnp.where(j == 0, nlat + b, b * per + (j - 1 if d == 0 else per - j))

    def dir_specs(d):
        bk = blk(d)
        tok = lambda w_: pl.BlockSpec((1, nb, w_), lambda b, j: (d, bk(b, j), 0))
        return [tok(HW), tok(HW), tok(HW),
                pl.BlockSpec((1, HW, nb), lambda b, j: (d, 0, bk(b, j))),
                tok(GDN_HEADS * nb),
                pl.BlockSpec((nb, LANE), lambda b, j: (bk(b, j), 0))]

    args = (u, w, qg, kgt, qk, gates)
    return pl.pallas_call(
        _gdn_c_kernel,
        grid=(B, per + 1),
        in_specs=dir_specs(0) + dir_specs(1),
        out_specs=[pl.BlockSpec((nb, HW), lambda b, j: (blk(0)(b, j), 0)),
                   pl.BlockSpec((nb, HW), lambda b, j: (blk(1)(b, j), 0))],
        out_shape=[jax.ShapeDtypeStruct((R, HW), F32), jax.ShapeDtypeStruct((R, HW), F32)],
        scratch_shapes=[pltpu.VMEM((2, GDN_HEADS, GDN_DK, GDN_DV), F32),
                        pltpu.VMEM((2, GDN_HEADS, nb, GDN_DV), BF16)],
        compiler_params=_params(("parallel", "arbitrary")),
        name="gdn_scan",
    )(*args, *args)


def _gdn_out_kernel(of_ref, ob_ref, og_ref, g_ref, o_ref):
    o = of_ref[...] + ob_ref[...]
    gate = _silu(og_ref[...].astype(F32))
    g = g_ref[...]
    for h in range(GDN_HEADS):
        hs = slice(h * LANE, (h + 1) * LANE)
        o_ref[:, hs] = (_rms(o[:, hs], g) * gate[:, hs]).astype(BF16)


def _gdn_out(o_f, o_b, P, g_out, rows):
    HW = GDN_HEADS * LANE
    bm = 512 if rows % 512 == 0 else 256
    return pl.pallas_call(
        _gdn_out_kernel,
        grid=(rows // bm,),
        in_specs=[pl.BlockSpec((bm, HW), lambda i: (i, 0)),
                  pl.BlockSpec((bm, HW), lambda i: (i, 0)),
                  pl.BlockSpec((bm, HW), lambda i: (i, OFF_OG // HW)),
                  pl.BlockSpec((1, LANE), lambda i: (0, 0))],
        out_specs=pl.BlockSpec((bm, HW), lambda i: (i, 0)),
        out_shape=jax.ShapeDtypeStruct((rows, HW), BF16),
        compiler_params=_params(("parallel",)),
        name="gdn_out",
    )(o_f, o_b, P, g_out.reshape(1, LANE))


def _residual_rows(x_ref, gate, o_ref, post=None):
    def body(r, carry):
        rs = pl.ds(pl.multiple_of(r * NORM_ROWS, NORM_ROWS), NORM_ROWS)
        v = x_ref[rs, :] + gate * o_ref[rs, :]
        o_ref[rs, :] = v if post is None else post(v)
        return carry

    lax.fori_loop(0, x_ref.shape[0] // NORM_ROWS, body, 0)


def _merge_kernel(xa_ref, xb_ref, mod_ref, g_ref, oa_ref, ob_ref, oc_ref, wga_ref, wgb_ref, wgc_ref, wua_ref,
                  wub_ref, wuc_ref, wo_ref, o_ref, hx_ref, *, nlat):
    j = pl.program_id(1)
    m = mod_ref[0]

    @pl.when(j == 0)
    def _():
        _for_row_source(nlat, xa_ref, xb_ref,
                        lambda x_ref: _norm_mod_rows(x_ref, g_ref[...], m[0:1], m[1:2], hx_ref))
        o_ref[...] = jnp.zeros_like(o_ref)

    hx = hx_ref[...]
    oa, ob, oc = oa_ref[...], ob_ref[...], oc_ref[...]
    hc = wo_ref.shape[0]
    ga, gb, gc = _mm(hx, wga_ref[...]), _mm(hx, wgb_ref[...]), _mm(hx, wgc_ref[...])
    ua, ub, uc = _mm(oa, wua_ref[...]), _mm(ob, wub_ref[...]), _mm(oc, wuc_ref[...])
    acc = None
    for cs in (slice(0, hc // 2), slice(hc // 2, hc)):
        y = _sigmoid(ga[:, cs]) * ua[:, cs] + _sigmoid(gb[:, cs]) * ub[:, cs] + _sigmoid(gc[:, cs]) * uc[:, cs]
        t = _mm(y.astype(BF16), wo_ref[cs, :])
        acc = t if acc is None else acc + t
    o_ref[...] += acc

    @pl.when(j == pl.num_programs(1) - 1)
    def _():
        _for_row_source(nlat, xa_ref, xb_ref, lambda x_ref: _residual_rows(x_ref, m[2:3], o_ref))


def _merge(src, mod_l, g1, oa, ob, oc, wg, wua, wub, wuc, wo, geom, rows):
    D = src[0].shape[1]
    bm = _pick_block(512, geom["T"], geom["RC"])
    hc = 512
    nj = D // hc
    sel = geom["mod_sel"](bm)
    rowb = lambda w_: pl.BlockSpec((bm, w_), lambda i, j: (i, 0))
    gate_w = lambda br: pl.BlockSpec((D, hc), lambda i, j: (0, br * nj + j))
    nlat, spec_a, spec_b = _row_source_specs(src, bm, D, geom["RL"])
    return pl.pallas_call(
        functools.partial(_merge_kernel, nlat=nlat),
        grid=(rows // bm, nj),
        in_specs=[spec_a, spec_b,
                  pl.BlockSpec((1, N_MOD, D), lambda i, j: (sel(i), 0, 0)),
                  pl.BlockSpec((1, D), lambda i, j: (0, 0)),
                  rowb(oa.shape[1]), rowb(ob.shape[1]), rowb(oc.shape[1]),
                  gate_w(0), gate_w(1), gate_w(2),
                  pl.BlockSpec((wua.shape[0], hc), lambda i, j: (0, j)),
                  pl.BlockSpec((wub.shape[0], hc), lambda i, j: (0, j)),
                  pl.BlockSpec((wuc.shape[0], hc), lambda i, j: (0, j)),
                  pl.BlockSpec((hc, D), lambda i, j: (j, 0))],
        out_specs=rowb(D),
        out_shape=jax.ShapeDtypeStruct((rows, D), F32),
        scratch_shapes=[pltpu.VMEM((bm, D), BF16)],
        compiler_params=_params(("parallel", "arbitrary"), VMEM_BIG),
        name="merge",
    )(src[0], src[1], mod_l, g1.reshape(1, D), oa, ob, oc, wg, wg, wg, wua, wub, wuc, wo)


def _mlp_kernel(x_ref, mod_ref, g_ref, w1_ref, w2_ref, gf_ref, o_ref, hx_ref, *, final):
    j = pl.program_id(1)
    m = mod_ref[0]

    @pl.when(j == 0)
    def _():
        _norm_mod_rows(x_ref, g_ref[...], m[3:4], m[4:5], hx_ref)
        o_ref[...] = jnp.zeros_like(o_ref)

    hx = hx_ref[...]
    fc = w2_ref.shape[1]
    h1 = _mm(hx, w1_ref[0].astype(BF16))
    acc = None
    for cs in (slice(0, fc // 2), slice(fc // 2, fc)):
        h = jnp.maximum(h1[:, cs], 0.0)
        t = _mm((h * h).astype(BF16), w2_ref[0, cs, :].astype(BF16))
        acc = t if acc is None else acc + t
    o_ref[...] += acc

    @pl.when(j == pl.num_programs(1) - 1)
    def _():
        gf = gf_ref[...]
        _residual_rows(x_ref, m[5:6], o_ref, (lambda v: _rms(v, gf)) if final else None)


def _mlp(x1, mod_l, g2, w1, w2, l, g_final, geom, final):
    rows, D = x1.shape
    F = w1.shape[2]
    bm = _pick_block(1024, geom["T"], geom["RC"])
    fc = 512
    sel = geom["mod_sel"](bm)
    return pl.pallas_call(
        functools.partial(_mlp_kernel, final=final),
        grid=(rows // bm, F // fc),
        in_specs=[pl.BlockSpec((bm, D), lambda i, j: (i, 0), pipeline_mode=pl.Buffered(1)),
                  pl.BlockSpec((1, N_MOD, D), lambda i, j: (sel(i), 0, 0)),
                  pl.BlockSpec((1, D), lambda i, j: (0, 0)),
                  pl.BlockSpec((1, D, fc), lambda i, j: (l, 0, j)),
                  pl.BlockSpec((1, fc, D), lambda i, j: (l, j, 0)),
                  pl.BlockSpec((1, D), lambda i, j: (0, 0))],
        out_specs=pl.BlockSpec((bm, D), lambda i, j: (i, 0)),
        out_shape=jax.ShapeDtypeStruct((rows, D), F32),
        scratch_shapes=[pltpu.VMEM((bm, D), BF16)],
        compiler_params=_params(("parallel", "arbitrary"), VMEM_BIG),
        name="mlp",
    )(x1, mod_l, g2.reshape(1, D), w1, w2, g_final.reshape(1, D))


def _rope_tables(T, bm):
    t = np.arange(T)
    row, col = t // GRID_W, t % GRID_W

    def table(n_rot):
        n = n_rot // 2
        half = n // 2
        inv = ROPE_THETA ** (-np.arange(half, dtype=np.float64) / half)
        cos = np.zeros((T + bm, LANE))
        sin = np.zeros((T + bm, LANE))
        for s, pos in enumerate((row, col)):
            ang = pos[:, None].astype(np.float32).astype(np.float64) * inv.astype(np.float32)[None, :]
            ang = ang.astype(np.float32).astype(np.float64)
            c, sn = np.cos(ang), np.sin(ang)
            cos[:T, s * n:s * n + half] = c
            cos[:T, s * n + half:(s + 1) * n] = c
            sin[:T, s * n:s * n + half] = -sn
            sin[:T, s * n + half:(s + 1) * n] = sn
        cos[T:, :] = 1.0
        return jnp.asarray(cos, F32), jnp.asarray(sin, F32)

    cg, sg = table(GQA_HD)
    cm, sm = table(MLA_ROPE)
    return cg, sg, cm, sm


def _relayout_kernel(w_ref, wa_ref, wg_ref, *, src_off):
    lane = lax.broadcasted_iota(jnp.int32, (1, LANE), 1)

    def tile(k):
        return w_ref[0, :, k * LANE:(k + 1) * LANE]

    def window(col):
        k, r = divmod(col, LANE)
        if r == 0:
            return tile(k)
        return jnp.where(lane < LANE - r, pltpu.roll(tile(k), LANE - r, 1), pltpu.roll(tile(k + 1), LANE - r, 1))

    for name, dst, width in (("gq", OFF_GQ, 1024), ("cq", OFF_CQ, 512), ("ckv", OFF_CKV, 512), ("gk", OFF_GK, 256),
                             ("gv", OFF_GV, 256), ("og", OFF_OG, 512), ("qkv", OFF_QKV, 1536)):
        for t in range(width // LANE):
            wa_ref[:, dst + t * LANE:dst + (t + 1) * LANE] = window(src_off[name] + t * LANE).astype(BF16)
    assert src_off["kpe"] % LANE == 0 and src_off["beta"] % LANE == MISC_BETA and src_off["dec"] % LANE == MISC_DEC
    misc = jnp.where(lane < MLA_ROPE, tile(src_off["kpe"] // LANE),
                     jnp.where(lane < MISC_DEC + 2 * GDN_HEADS, tile(src_off["beta"] // LANE), 0.0))
    wa_ref[:, OFF_MISC:OFF_MISC + LANE] = misc.astype(BF16)
    wa_ref[:, OFF_MISC + LANE:] = jnp.zeros((wa_ref.shape[0], N_SMALL - OFF_MISC - LANE), BF16)
    for t in range(wg_ref.shape[1] // LANE):
        wg_ref[:, t * LANE:(t + 1) * LANE] = window(src_off["bg"] + t * LANE).astype(BF16)


def _relayout_w_in(w_in, l):
    L, D, N = w_in.shape
    splits = (("cq", MLA_Q_RANK), ("ckv", MLA_KV_RANK), ("kpe", MLA_ROPE), ("gq", GQA_HEADS * GQA_HD),
              ("gk", GQA_KV_HEADS * GQA_HD), ("gv", GQA_KV_HEADS * GQA_HD),
              ("qkv", GDN_HEADS * (2 * GDN_DK + GDN_DV)), ("beta", 2 * GDN_HEADS), ("dec", 2 * GDN_HEADS),
              ("og", GDN_HEADS * GDN_DV), ("bg", 3 * D))
    src_off, o = {}, 0
    for name, n in splits:
        src_off[name] = o
        o += n
    assert o == N
    br = 256
    n_pad = pl.cdiv(N, LANE) * LANE
    return pl.pallas_call(
        functools.partial(_relayout_kernel, src_off=src_off),
        grid=(D // br,),
        in_specs=[pl.BlockSpec((1, br, n_pad), lambda i: (l, i, 0))],
        out_specs=[pl.BlockSpec((br, N_SMALL), lambda i: (i, 0)),
                   pl.BlockSpec((br, 3 * D), lambda i: (i, 0))],
        out_shape=[jax.ShapeDtypeStruct((D, N_SMALL), BF16), jax.ShapeDtypeStruct((D, 3 * D), BF16)],
        compiler_params=_params(("parallel",), VMEM_BIG),
        name="w_in_relayout",
    )(w_in)


def _prep_layer_weights(w_qb, w_kvb):
    qb = w_qb.reshape(-1, MLA_HEADS, MLA_NOPE + MLA_ROPE)
    qpe = jnp.pad(qb[:, :, MLA_NOPE:], ((0, 0), (0, 0), (0, LANE - MLA_ROPE)))
    wqb = jnp.concatenate([qb[:, :, :MLA_NOPE].reshape(-1, MLA_HEADS * LANE),
                           qpe.reshape(-1, MLA_HEADS * LANE)], axis=1).astype(BF16)
    kvb = w_kvb.reshape(-1, MLA_HEADS, MLA_NOPE + MLA_V)
    wkvb = jnp.concatenate([kvb[:, :, :MLA_NOPE].reshape(-1, MLA_HEADS * LANE),
                            kvb[:, :, MLA_NOPE:].reshape(-1, MLA_HEADS * LANE)], axis=1).astype(BF16)
    return wqb, wkvb


def kernel(x, c, ctx, c_ctx, w_mod, b_mod, g_norm1, w_in, g_mla_q, w_mla_qb, g_mla_kv, w_mla_kvb, g_gqa_q, g_gqa_k,
           w_conv, a_log, dt_bias, g_gdn_out, w_up_a, w_up_b, w_up_c, w_out, g_norm2, w_ff1, w_ff2, g_final):
    B, T, D = x.shape
    C = ctx.shape[1]
    L = w_mod.shape[0]
    assert C == GDN_BLOCK and T % GDN_BLOCK == 0 and T % GRID_W == 0 and B < 8
    RL, RC = B * T, B * C
    geom = dict(B=B, T=T, C=C, RL=RL, RC=RC, R=RL + RC)
    geom["mod_sel"] = lambda bm: (lambda i: jnp.where(i < RL // bm, i // (T // bm), B))

    mod = _modulation(c, c_ctx, w_mod, b_mod)
    tabs = _rope_tables(T, _pick_block(512, T, RC))
    src = (x.reshape(RL, D), ctx.reshape(RC, D), 0)
    mla_scale = (MLA_NOPE + MLA_ROPE) ** -0.5
    gqa_scale = GQA_HD ** -0.5

    for l in range(L):
        last = l == L - 1
        wa, wg = _relayout_w_in(w_in, l)
        wqb, wkvb = _prep_layer_weights(w_mla_qb[l], w_mla_kvb[l])
        P = _in_proj(src, mod[l], g_norm1[l], wa, geom)
        qm, km, vm, qg, kg, vg = _attn_prep(P, g_mla_q[l], g_mla_kv[l], wqb, wkvb, g_gqa_q[l], g_gqa_k[l], tabs, geom)
        oa = _attention(qm, km, vm, mla_scale, geom, "attn_mla")
        ob = _attention(qg, kg, vg, gqa_scale, geom, "attn_gqa")
        qkvn, gates = _gdn_conv(P, w_conv[l], a_log[l], dt_bias[l], geom)
        u, w, qgd, kgt, qk = _gdn_local(qkvn, gates)
        o_f, o_b = _gdn_scan(u, w, qgd, kgt, qk, gates, geom)
        rows = RL if last else RL + RC
        oc = _gdn_out(o_f, o_b, P, g_gdn_out[l], rows)
        if not last:
            oa = jnp.concatenate([oa, _attention_ctx(qm, km, vm, mla_scale, geom, "attn_mla_ctx")], axis=0)
            ob = jnp.concatenate([ob, _attention_ctx(qg, kg, vg, gqa_scale, geom, "attn_gqa_ctx")], axis=0)
        x1 = _merge(src, mod[l], g_norm1[l], oa, ob, oc, wg, w_up_a[l].astype(BF16), w_up_b[l].astype(BF16),
                    w_up_c[l].astype(BF16), w_out[l].astype(BF16), geom, rows)
        xs = _mlp(x1, mod[l], g_norm2[l], w_ff1, w_ff2, l, g_final, geom, last)
        src = (xs, xs, RL)
    return xs.reshape(B, T, D)
```

```python
import functools
import math

import numpy as np
import jax
import jax.numpy as jnp
from jax import lax
from jax.experimental import pallas as pl
from jax.experimental.pallas import tpu as pltpu

F32 = jnp.float32
BF16 = jnp.bfloat16

GRID_W = 64
EPS = 1e-6
ROPE_THETA = 10000.0
MLA_HEADS, MLA_Q_RANK, MLA_KV_RANK, MLA_NOPE, MLA_ROPE, MLA_V = 4, 512, 512, 128, 64, 128
GQA_HEADS, GQA_KV_HEADS, GQA_HD = 8, 2, 128
GDN_HEADS, GDN_DK, GDN_DV, GDN_CONV, GDN_CHUNK = 4, 128, 128, 5, 64
N_MOD = 6
LANE = 128
GDN_BLOCK = 256
HALO_ROWS = 16
VMEM_BIG = 56 * 1024 * 1024

OFF_GQ, OFF_CQ, OFF_CKV, OFF_GK, OFF_GV, OFF_OG, OFF_QKV, OFF_MISC = 0, 1024, 1536, 2048, 2304, 2560, 3072, 4608
N_SMALL = 5120
MISC_BETA, MISC_DEC = 64, 72


def _mm(a, b):
    return jnp.dot(a, b, preferred_element_type=F32)


def _mm_nt(a, b):
    return lax.dot_general(a, b, (((1,), (1,)), ((), ())), preferred_element_type=F32)


def _silu(x):
    return x / (1.0 + jnp.exp(-x))


def _sigmoid(x):
    return 1.0 / (1.0 + jnp.exp(-x))


def _rms(x, g):
    return x * lax.rsqrt(jnp.mean(x * x, axis=-1, keepdims=True) + EPS) * g


def _norm_mod(x, g, shift, scale):
    return _rms(x, g) * (1.0 + scale) + shift


def _params(sem, vmem=None):
    kw = dict(dimension_semantics=sem)
    if vmem is not None:
        kw["vmem_limit_bytes"] = vmem
    return pltpu.CompilerParams(**kw)


def _pick_block(pref, *sizes):
    b = pref
    while any(s % b for s in sizes):
        b //= 2
    return b


def _mod_kernel(s_ref, w_ref, b_ref, o_ref):
    s = _silu(s_ref[...])
    o_ref[0] = _mm(s.astype(BF16), w_ref[0].astype(BF16)) + b_ref[0]


def _modulation(c, c_ctx, w_mod, b_mod):
    L, D, N = w_mod.shape
    B = c.shape[0]
    rows = jnp.zeros((8, D), F32).at[:B].set(c).at[B].set(c_ctx)
    bn = 2048
    out = pl.pallas_call(
        _mod_kernel,
        grid=(L, N // bn),
        in_specs=[pl.BlockSpec((8, D), lambda l, j: (0, 0)),
                  pl.BlockSpec((1, D, bn), lambda l, j: (l, 0, j)),
                  pl.BlockSpec((1, 1, bn), lambda l, j: (l, 0, j))],
        out_specs=pl.BlockSpec((1, 8, bn), lambda l, j: (l, 0, j)),
        out_shape=jax.ShapeDtypeStruct((L, 8, N), F32),
        compiler_params=_params(("parallel", "parallel"), VMEM_BIG),
        name="adaln_mod",
    )(rows, w_mod, b_mod.reshape(L, 1, N))
    return out.reshape(L, 8, N_MOD, D)


NORM_ROWS = 64


def _norm_mod_rows(x_ref, g, shift, scale, hx_ref):
    def body(r, carry):
        rs = pl.ds(pl.multiple_of(r * NORM_ROWS, NORM_ROWS), NORM_ROWS)
        hx_ref[rs, :] = _norm_mod(x_ref[rs, :], g, shift, scale).astype(BF16)
        return carry

    lax.fori_loop(0, x_ref.shape[0] // NORM_ROWS, body, 0)


def _row_source_specs(src, bm, D, n_latent_rows):
    nlat = n_latent_rows // bm
    off_b = src[2] // bm
    spec_a = pl.BlockSpec((bm, D), lambda i, j: (jnp.minimum(i, nlat - 1), 0))
    spec_b = pl.BlockSpec((bm, D), lambda i, j: (off_b + jnp.maximum(i - nlat, 0), 0), pipeline_mode=pl.Buffered(1))
    return nlat, spec_a, spec_b


def _for_row_source(nlat, xa_ref, xb_ref, fn):
    i = pl.program_id(0)

    @pl.when(i < nlat)
    def _():
        fn(xa_ref)

    @pl.when(i >= nlat)
    def _():
        fn(xb_ref)


def _inproj_kernel(xa_ref, xb_ref, mod_ref, g_ref, w_ref, o_ref, hx_ref, *, nlat):
    @pl.when(pl.program_id(1) == 0)
    def _():
        m = mod_ref[0]
        _for_row_source(nlat, xa_ref, xb_ref,
                        lambda x_ref: _norm_mod_rows(x_ref, g_ref[...], m[0:1], m[1:2], hx_ref))

    o_ref[...] = _mm_nt(hx_ref[...], w_ref[...]).astype(BF16)


def _in_proj(src, mod_l, g1, wa, geom):
    R, D = geom["R"], src[0].shape[1]
    N = wa.shape[0]
    bm = _pick_block(1024, geom["T"], geom["RC"])
    bn = 1024
    sel = geom["mod_sel"](bm)
    nlat, spec_a, spec_b = _row_source_specs(src, bm, D, geom["RL"])
    return pl.pallas_call(
        functools.partial(_inproj_kernel, nlat=nlat),
        grid=(R // bm, N // bn),
        in_specs=[spec_a, spec_b,
                  pl.BlockSpec((1, N_MOD, D), lambda i, j: (sel(i), 0, 0)),
                  pl.BlockSpec((1, D), lambda i, j: (0, 0)),
                  pl.BlockSpec((bn, D), lambda i, j: (j, 0))],
        out_specs=pl.BlockSpec((bm, bn), lambda i, j: (i, j)),
        out_shape=jax.ShapeDtypeStruct((R, N), BF16),
        scratch_shapes=[pltpu.VMEM((bm, D), BF16)],
        compiler_params=_params(("parallel", "arbitrary"), VMEM_BIG),
        name="in_proj",
    )(src[0], src[1], mod_l, g1.reshape(1, D), wa)


def _prep_kernel(gq_ref, cq_ref, ckv_ref, gk_ref, gv_ref, misc_ref, gmq_ref, gmkv_ref, wqb_ref, wkvb_ref,
                 ggq_ref, ggk_ref, cosg_ref, sing_ref, cosm_ref, sinm_ref,
                 qm_ref, km_ref, vm_ref, qg_ref, kg_ref, vg_ref):
    lane = lax.broadcasted_iota(jnp.int32, (1, LANE), 1)
    cosm, sinm = cosm_ref[...], sinm_ref[...]
    cosg, sing = cosg_ref[...], sing_ref[...]
    first_m = (lane & 31) < 16
    first_g = (lane & 63) < 32
    ones_col = jnp.where(lane == 0, 1.0, 0.0).astype(BF16) + jnp.zeros((cq_ref.shape[0], LANE), BF16)

    def rope_m(x):
        partner = jnp.where(first_m, pltpu.roll(x, LANE - 16, 1), pltpu.roll(x, 16, 1))
        return x * cosm + partner * sinm

    def rope_g(x):
        partner = jnp.where(first_g, pltpu.roll(x, LANE - 32, 1), pltpu.roll(x, 32, 1))
        return x * cosg + partner * sing

    qa = _mm(_rms(cq_ref[...].astype(F32), gmq_ref[...]).astype(BF16), wqb_ref[...])
    kva = _mm(_rms(ckv_ref[...].astype(F32), gmkv_ref[...]).astype(BF16), wkvb_ref[...])
    kpe = rope_m(jnp.where(lane < MLA_ROPE, misc_ref[...].astype(F32), 0.0)).astype(BF16)
    nh = MLA_HEADS * LANE
    for h in range(MLA_HEADS):
        hs = slice(h * LANE, (h + 1) * LANE)
        ps = slice(nh + h * LANE, nh + (h + 1) * LANE)
        qm_ref[h, :, 0:LANE] = qa[:, hs].astype(BF16)
        qm_ref[h, :, LANE:2 * LANE] = rope_m(qa[:, ps]).astype(BF16)
        km_ref[h, :, 0:LANE] = kva[:, hs].astype(BF16)
        km_ref[h, :, LANE:2 * LANE] = kpe
        vm_ref[h, :, 0:LANE] = kva[:, ps].astype(BF16)
        vm_ref[h, :, LANE:2 * LANE] = ones_col

    ggq, ggk = ggq_ref[...], ggk_ref[...]
    for h in range(GQA_HEADS):
        hs = slice(h * LANE, (h + 1) * LANE)
        qg_ref[h] = rope_g(_rms(gq_ref[:, hs].astype(F32), ggq)).astype(BF16)
    for h in range(GQA_KV_HEADS):
        hs = slice(h * LANE, (h + 1) * LANE)
        kg_ref[h] = rope_g(_rms(gk_ref[:, hs].astype(F32), ggk)).astype(BF16)
        vg_ref[h, :, 0:LANE] = gv_ref[:, hs]
        vg_ref[h, :, LANE:2 * LANE] = ones_col


def _attn_prep(P, gmq, gmkv, wqb, wkvb, ggq, ggk, tabs, geom):
    R = P.shape[0]
    bm = _pick_block(512, geom["T"], geom["RC"])
    nlat = geom["RL"] // bm
    per = geom["T"] // bm
    tsel = lambda i: jnp.where(i < nlat, i % per, per)
    col = lambda w, off: pl.BlockSpec((bm, w), lambda i: (i, off // w))
    full = lambda a: pl.BlockSpec(a.shape, lambda i: (0,) * a.ndim)
    tab = pl.BlockSpec((bm, LANE), lambda i: (tsel(i), 0))
    outs = [((MLA_HEADS, R, 2 * LANE), 2 * LANE), ((MLA_HEADS, R, 2 * LANE), 2 * LANE),
            ((MLA_HEADS, R, 2 * LANE), 2 * LANE), ((GQA_HEADS, R, LANE), LANE), ((GQA_KV_HEADS, R, LANE), LANE),
            ((GQA_KV_HEADS, R, 2 * LANE), 2 * LANE)]
    gmq, gmkv, ggq, ggk = (a.reshape(1, -1) for a in (gmq, gmkv, ggq, ggk))
    return pl.pallas_call(
        _prep_kernel,
        grid=(R // bm,),
        in_specs=[col(1024, OFF_GQ), col(512, OFF_CQ), col(512, OFF_CKV), col(256, OFF_GK), col(256, OFF_GV),
                  col(LANE, OFF_MISC), full(gmq), full(gmkv), full(wqb), full(wkvb), full(ggq), full(ggk),
                  tab, tab, tab, tab],
        out_specs=[pl.BlockSpec((s[0], bm, w), lambda i: (0, i, 0)) for s, w in outs],
        out_shape=[jax.ShapeDtypeStruct(s, BF16) for s, _ in outs],
        compiler_params=_params(("parallel",), VMEM_BIG),
        name="attn_prep",
    )(P, P, P, P, P, P, gmq, gmkv, wqb, wkvb, ggq, ggk, *tabs)


def _softmax_pv(q, kv_chunks, s_ref, o_ref, exp2_scale):
    sub = min(q.shape)
    for r0 in range(0, q.shape[0], sub):
        rows = slice(r0, r0 + sub)
        qr = q[rows]
        m_acc = None
        off = 0
        for k_ref, _, st, n in kv_chunks:
            s = _mm_nt(qr, k_ref[0, st:st + n, :])
            s_ref[rows, off:off + n] = s
            for t in range(n // LANE):
                tile = s[:, t * LANE:(t + 1) * LANE]
                m_acc = tile if m_acc is None else jnp.maximum(m_acc, tile)
            off += n
        m = jnp.max(m_acc, axis=-1, keepdims=True)
        acc = None
        off = 0
        for _, v_ref, st, n in kv_chunks:
            p = jnp.exp2((s_ref[rows, off:off + n] - m) * exp2_scale).astype(BF16)
            pv = _mm(p, v_ref[0, st:st + n, :])
            acc = pv if acc is None else acc + pv
            off += n
        o_ref[rows, :] = (acc[:, :LANE] / acc[:, LANE:LANE + 1]).astype(BF16)


def _attn_kernel(q_ref, kx_ref, kc_ref, vx_ref, vc_ref, o_ref, s_ref, *, exp2_scale, kchunk):
    chunks = [(kx_ref, vx_ref, i * kchunk, kchunk) for i in range(kx_ref.shape[1] // kchunk)]
    chunks.append((kc_ref, vc_ref, 0, kc_ref.shape[1]))
    _softmax_pv(q_ref[0], chunks, s_ref, o_ref, exp2_scale)


def _attn_ctx_kernel(q_ref, kc_ref, vc_ref, o_ref, s_ref, *, exp2_scale):
    _softmax_pv(q_ref[0], [(kc_ref, vc_ref, 0, kc_ref.shape[1])], s_ref, o_ref, exp2_scale)


def _attention(q, k, v, scale, geom, name):
    H, _, dk = q.shape
    Hk, _, dve = v.shape
    grp = H // Hk
    B, T, C, RL = geom["B"], geom["T"], geom["C"], geom["RL"]
    bq = _pick_block(1024, T)
    nq = T // bq
    cb = RL // C
    return pl.pallas_call(
        functools.partial(_attn_kernel, exp2_scale=scale * math.log2(math.e), kchunk=256),
        grid=(B, H, nq),
        in_specs=[pl.BlockSpec((1, bq, dk), lambda b, h, j: (h, b * nq + j, 0)),
                  pl.BlockSpec((1, T, dk), lambda b, h, j: (h // grp, b, 0)),
                  pl.BlockSpec((1, C, dk), lambda b, h, j: (h // grp, cb + b, 0)),
                  pl.BlockSpec((1, T, dve), lambda b, h, j: (h // grp, b, 0)),
                  pl.BlockSpec((1, C, dve), lambda b, h, j: (h // grp, cb + b, 0))],
        out_specs=pl.BlockSpec((bq, LANE), lambda b, h, j: (b * nq + j, h)),
        out_shape=jax.ShapeDtypeStruct((RL, H * LANE), BF16),
        scratch_shapes=[pltpu.VMEM((bq, T + C), F32)],
        compiler_params=_params(("parallel", "parallel", "parallel"), VMEM_BIG),
        name=name,
    )(q, k, k, v, v)


def _attention_ctx(q, k, v, scale, geom, name):
    H, _, dk = q.shape
    Hk, _, dve = v.shape
    grp = H // Hk
    B, C, RL, RC = geom["B"], geom["C"], geom["RL"], geom["RC"]
    cb = RL // C
    return pl.pallas_call(
        functools.partial(_attn_ctx_kernel, exp2_scale=scale * math.log2(math.e)),
        grid=(B, H),
        in_specs=[pl.BlockSpec((1, C, dk), lambda b, h: (h, cb + b, 0)),
                  pl.BlockSpec((1, C, dk), lambda b, h: (h // grp, cb + b, 0)),
                  pl.BlockSpec((1, C, dve), lambda b, h: (h // grp, cb + b, 0))],
        out_specs=pl.BlockSpec((C, LANE), lambda b, h: (b, h)),
        out_shape=jax.ShapeDtypeStruct((RC, H * LANE), BF16),
        scratch_shapes=[pltpu.VMEM((C, C), F32)],
        compiler_params=_params(("parallel", "parallel")),
        name=name,
    )(q, k, v)


def _gdn_a_kernel(cur_ref, prev_ref, next_ref, misc_ref, wc_ref, gvec_ref, qkvn_ref, gates_ref, ext_ref,
                  *, nlat, per_seq):
    r = pl.program_id(0)
    pos = r % per_seq
    is_ctx = r >= nlat
    pf = jnp.where(jnp.logical_or(is_ctx, pos == 0), 0.0, 1.0).astype(F32)
    nf = jnp.where(jnp.logical_or(is_ctx, pos == per_seq - 1), 0.0, 1.0).astype(F32)
    nb = GDN_BLOCK
    hal = HALO_ROWS
    ext_ref[0:hal, :] = prev_ref[...].astype(F32) * pf
    ext_ref[hal:hal + nb, :] = cur_ref[...].astype(F32)
    ext_ref[hal + nb:2 * hal + nb, :] = next_ref[...].astype(F32) * nf
    pad = GDN_CONV // 2
    n_qk = 2 * GDN_HEADS
    for c in range(3 * GDN_HEADS):
        cs = slice(c * LANE, (c + 1) * LANE)
        acc = None
        for j in range(GDN_CONV):
            t = ext_ref[hal - pad + j:hal - pad + j + nb, cs] * wc_ref[j:j + 1, cs]
            acc = t if acc is None else acc + t
        y = _silu(acc)
        if c < n_qk:
            y = y * lax.rsqrt(jnp.sum(y * y, axis=-1, keepdims=True) + EPS)
        qkvn_ref[:, cs] = y

    lane = lax.broadcasted_iota(jnp.int32, (1, LANE), 1)
    raw = misc_ref[...].astype(F32)
    beta = _sigmoid(raw)
    z = raw + gvec_ref[1:2, :]
    softplus = jnp.maximum(z, 0.0) + jnp.log1p(jnp.exp(-jnp.abs(z)))
    g = -jnp.exp(gvec_ref[0:1, :]) * softplus
    is_beta = jnp.logical_and(lane >= MISC_BETA, lane < MISC_DEC)
    is_g = jnp.logical_and(lane >= MISC_DEC, lane < MISC_DEC + 2 * GDN_HEADS)
    gates_ref[...] = jnp.where(is_beta, beta, jnp.where(is_g, g, 0.0))


def _gdn_conv(P, w_conv, a_log, dt_bias, geom):
    R = P.shape[0]
    nb = GDN_BLOCK
    W = 3 * GDN_HEADS * LANE
    nblk = R // nb
    hal = HALO_ROWS
    sub = nb // hal
    gvec = jnp.zeros((2, LANE), F32)
    gvec = gvec.at[0, MISC_DEC:MISC_DEC + 2 * GDN_HEADS].set(a_log.reshape(-1))
    gvec = gvec.at[1, MISC_DEC:MISC_DEC + 2 * GDN_HEADS].set(dt_bias.reshape(-1))
    qc = OFF_QKV // W
    return pl.pallas_call(
        functools.partial(_gdn_a_kernel, nlat=geom["RL"] // nb, per_seq=geom["T"] // nb),
        grid=(nblk,),
        in_specs=[pl.BlockSpec((nb, W), lambda r: (r, qc)),
                  pl.BlockSpec((hal, W), lambda r: (jnp.maximum(r * sub - 1, 0), qc)),
                  pl.BlockSpec((hal, W), lambda r: (jnp.minimum((r + 1) * sub, R // hal - 1), qc)),
                  pl.BlockSpec((nb, LANE), lambda r: (r, OFF_MISC // LANE)),
                  pl.BlockSpec((GDN_CONV, W), lambda r: (0, 0)),
                  pl.BlockSpec((2, LANE), lambda r: (0, 0))],
        out_specs=[pl.BlockSpec((nb, W), lambda r: (r, 0)),
                   pl.BlockSpec((nb, LANE), lambda r: (r, 0))],
        out_shape=[jax.ShapeDtypeStruct((R, W), F32), jax.ShapeDtypeStruct((R, LANE), F32)],
        scratch_shapes=[pltpu.VMEM((nb + 2 * hal, W), F32)],
        compiler_params=_params(("parallel",)),
        name="gdn_conv",
    )(P, P, P, P, w_conv, gvec)


def _split3(x):
    h = x.astype(BF16)
    r = x - h.astype(F32)
    m = r.astype(BF16)
    l = (r - m.astype(F32)).astype(BF16)
    return h, m, l


def _gdn_b_kernel(qkvn_ref, gates_ref, u_ref, w_ref, qg_ref, kgt_ref, qk_ref):
    nb = GDN_BLOCK
    nh = GDN_HEADS
    ri = lax.broadcasted_iota(jnp.int32, (nb, nb), 0)
    ci = lax.broadcasted_iota(jnp.int32, (nb, nb), 1)
    same = (ri >> 6) == (ci >> 6)
    low = jnp.logical_and(same, ri >= ci)
    upp = jnp.logical_and(same, ri <= ci)
    slow = jnp.logical_and(same, ri > ci)
    supp = jnp.logical_and(same, ri < ci)
    ltri = jnp.where(low, 1.0, 0.0).astype(BF16)
    utri = jnp.where(upp, 1.0, 0.0).astype(BF16)
    eye = jnp.where(ri == ci, 1.0, 0.0)
    pair = (ri >> 1) == (ci >> 1)
    offs = [jnp.logical_and((ri >> (lv + 1)) == (ci >> (lv + 1)), (ri >> lv) != (ci >> lv))
            for lv in range(1, int(math.log2(GDN_CHUNK)))]

    gt = gates_ref[...]
    g3 = _split3(gt)
    g3t = _split3(gt.T)
    cum_c = (sum(_mm(ltri, p) for p in reversed(g3)), sum(_mm(utri, p) for p in reversed(g3)))
    cum_r = (sum(_mm(p, utri) for p in reversed(g3t)), sum(_mm(p, ltri) for p in reversed(g3t)))
    tot_c = cum_c[0] + cum_c[1] - gt

    scale = GDN_DK ** -0.5
    insts = [(d, h) for d in range(2) for h in range(nh)]
    a_all, tinv_all, rhs_all = [], [], []
    for d, h in insts:
        mask, smask = (low, slow) if d == 0 else (upp, supp)
        ib = MISC_BETA + d * nh + h
        ig = MISC_DEC + d * nh + h
        beta = gt[:, ib:ib + 1]
        gc = cum_c[d][:, ig:ig + 1]
        gr = cum_r[d][ig:ig + 1, :]
        gl = tot_c[:, ig:ig + 1]
        q = qkvn_ref[:, h * LANE:(h + 1) * LANE]
        k = qkvn_ref[:, (nh + h) * LANE:(nh + h + 1) * LANE]
        v = qkvn_ref[:, (2 * nh + h) * LANE:(2 * nh + h + 1) * LANE]
        decay = jnp.where(mask, jnp.exp(jnp.where(mask, gc - gr, 0.0)), 0.0)
        kb = k * beta
        k16 = k.astype(BF16)
        a = jnp.where(smask, _mm_nt(kb.astype(BF16), k16) * decay, 0.0)
        a_all.append(a.astype(BF16))
        tinv_all.append(eye - jnp.where(pair, a, 0.0))
        rhs_all.append(jnp.concatenate([v * beta, kb * jnp.exp(gc)], axis=1).astype(BF16))
        hs = slice(h * LANE, (h + 1) * LANE)
        qs = q * scale
        qk_ref[d, :, h * nb:(h + 1) * nb] = (_mm_nt(qs.astype(BF16), k16) * decay).astype(BF16)
        qg_ref[d, :, hs] = (qs * jnp.exp(gc)).astype(BF16)
        kgt_ref[d, hs, :] = (k * jnp.exp(gl - gc)).T.astype(BF16)

    zero16 = jnp.zeros((nb, nb), BF16)
    for off in offs:
        for i in range(len(insts)):
            t16 = tinv_all[i].astype(BF16)
            ta = _mm(t16, jnp.where(off, a_all[i], zero16))
            tinv_all[i] = tinv_all[i] - _mm(ta.astype(BF16), t16)
    for i, (d, h) in enumerate(insts):
        x = _mm(tinv_all[i].astype(BF16), rhs_all[i])
        hs = slice(h * LANE, (h + 1) * LANE)
        u_ref[d, :, hs] = x[:, :LANE]
        w_ref[d, :, hs] = x[:, LANE:].astype(BF16)


def _gdn_local(qkvn, gates):
    R, W = qkvn.shape
    nb = GDN_BLOCK
    HW = GDN_HEADS * LANE
    return pl.pallas_call(
        _gdn_b_kernel,
        grid=(R // nb,),
        in_specs=[pl.BlockSpec((nb, W), lambda r: (r, 0)),
                  pl.BlockSpec((nb, LANE), lambda r: (r, 0))],
        out_specs=[pl.BlockSpec((2, nb, HW), lambda r: (0, r, 0)),
                   pl.BlockSpec((2, nb, HW), lambda r: (0, r, 0)),
                   pl.BlockSpec((2, nb, HW), lambda r: (0, r, 0)),
                   pl.BlockSpec((2, HW, nb), lambda r: (0, 0, r)),
                   pl.BlockSpec((2, nb, GDN_HEADS * nb), lambda r: (0, r, 0))],
        out_shape=[jax.ShapeDtypeStruct((2, R, HW), F32),
                   jax.ShapeDtypeStruct((2, R, HW), BF16),
                   jax.ShapeDtypeStruct((2, R, HW), BF16),
                   jax.ShapeDtypeStruct((2, HW, R), BF16),
                   jax.ShapeDtypeStruct((2, R, GDN_HEADS * nb), BF16)],
        compiler_params=_params(("parallel",), VMEM_BIG),
        name="gdn_local",
    )(qkvn, gates)


def _gdn_c_kernel(*refs):
    nb = GDN_BLOCK
    nh = GDN_HEADS
    ins, (of_ref, ob_ref, s_ref, vn_ref) = (refs[0:6], refs[6:12]), refs[12:]
    o_refs = (of_ref, ob_ref)

    @pl.when(pl.program_id(1) == 0)
    def _():
        s_ref[...] = jnp.zeros_like(s_ref)
        vn_ref[...] = jnp.zeros_like(vn_ref)

    rowi = lax.broadcasted_iota(jnp.int32, (nb, 1), 0)
    nchunk = nb // GDN_CHUNK
    for step in range(nchunk):
        for d in range(2):
            u_ref, w_ref, qg_ref, kgt_ref, qk_ref, gates_ref = ins[d]
            c = step if d == 0 else nchunk - 1 - step
            rs = slice(c * GDN_CHUNK, (c + 1) * GDN_CHUNK)
            decay_all = jnp.exp(jnp.sum(gates_ref[rs, :], axis=0, keepdims=True))
            cmask = jnp.logical_and(rowi >= c * GDN_CHUNK, rowi < (c + 1) * GDN_CHUNK)
            for h in range(nh):
                hs = slice(h * LANE, (h + 1) * LANE)
                ig = MISC_DEC + d * nh + h
                s = s_ref[d, h]
                s16 = s.astype(BF16)
                v_new = u_ref[0, rs, hs] - _mm(w_ref[0, rs, hs], s16)
                vn_ref[d, h, rs, :] = v_new.astype(BF16)
                vn = vn_ref[d, h]
                o_refs[d][rs, hs] = _mm(qg_ref[0, rs, hs], s16) + _mm(qk_ref[0, rs, h * nb:(h + 1) * nb], vn)
                v_cur = jnp.where(cmask, vn, jnp.zeros_like(vn))
                s_ref[d, h] = s * decay_all[:, ig:ig + 1] + _mm(kgt_ref[0, hs, :], v_cur)


def _gdn_scan(u, w, qg, kgt, qk, gates, geom):
    nb = GDN_BLOCK
    HW = GDN_HEADS * LANE
    B, RL, R = geom["B"], geom["RL"], geom["R"]
    per = geom["T"] // nb
    nlat = RL // nb

    def blk(d):
        return lambda b, j: jnp.where(j == 0, nlat + b, b * per + (j - 1 if d == 0 else per - j))

    def dir_specs(d):
        bk = blk(d)
        tok = lambda w_: pl.BlockSpec((1, nb, w_), lambda b, j: (d, bk(b, j), 0))
        return [tok(HW), tok(HW), tok(HW),
                pl.BlockSpec((1, HW, nb), lambda b, j: (d, 0, bk(b, j))),
                tok(GDN_HEADS * nb),
                pl.BlockSpec((nb, LANE), lambda b, j: (bk(b, j), 0))]

    args = (u, w, qg, kgt, qk, gates)
    return pl.pallas_call(
        _gdn_c_kernel,
        grid=(B, per + 1),
        in_specs=dir_specs(0) + dir_specs(1),
        out_specs=[pl.BlockSpec((nb, HW), lambda b, j: (blk(0)(b, j), 0)),
                   pl.BlockSpec((nb, HW), lambda b, j: (blk(1)(b, j), 0))],
        out_shape=[jax.ShapeDtypeStruct((R, HW), F32), jax.ShapeDtypeStruct((R, HW), F32)],
        scratch_shapes=[pltpu.VMEM((2, GDN_HEADS, GDN_DK, GDN_DV), F32),
                        pltpu.VMEM((2, GDN_HEADS, nb, GDN_DV), BF16)],
        compiler_params=_params(("parallel", "arbitrary")),
        name="gdn_scan",
    )(*args, *args)


def _gdn_out_kernel(of_ref, ob_ref, og_ref, g_ref, o_ref):
    o = of_ref[...] + ob_ref[...]
    gate = _silu(og_ref[...].astype(F32))
    g = g_ref[...]
    for h in range(GDN_HEADS):
        hs = slice(h * LANE, (h + 1) * LANE)
        o_ref[:, hs] = (_rms(o[:, hs], g) * gate[:, hs]).astype(BF16)


def _gdn_out(o_f, o_b, P, g_out, rows):
    HW = GDN_HEADS * LANE
    bm = 512 if rows % 512 == 0 else 256
    return pl.pallas_call(
        _gdn_out_kernel,
        grid=(rows // bm,),
        in_specs=[pl.BlockSpec((bm, HW), lambda i: (i, 0)),
                  pl.BlockSpec((bm, HW), lambda i: (i, 0)),
                  pl.BlockSpec((bm, HW), lambda i: (i, OFF_OG // HW)),
                  pl.BlockSpec((1, LANE), lambda i: (0, 0))],
        out_specs=pl.BlockSpec((bm, HW), lambda i: (i, 0)),
        out_shape=jax.ShapeDtypeStruct((rows, HW), BF16),
        compiler_params=_params(("parallel",)),
        name="gdn_out",
    )(o_f, o_b, P, g_out.reshape(1, LANE))


def _residual_rows(x_ref, gate, o_ref, post=None):
    def body(r, carry):
        rs = pl.ds(pl.multiple_of(r * NORM_ROWS, NORM_ROWS), NORM_ROWS)
        v = x_ref[rs, :] + gate * o_ref[rs, :]
        o_ref[rs, :] = v if post is None else post(v)
        return carry

    lax.fori_loop(0, x_ref.shape[0] // NORM_ROWS, body, 0)


def _merge_kernel(xa_ref, xb_ref, mod_ref, g_ref, oa_ref, ob_ref, oc_ref, wga_ref, wgb_ref, wgc_ref, wua_ref,
                  wub_ref, wuc_ref, wo_ref, o_ref, hx_ref, *, nlat):
    j = pl.program_id(1)
    m = mod_ref[0]

    @pl.when(j == 0)
    def _():
        _for_row_source(nlat, xa_ref, xb_ref,
                        lambda x_ref: _norm_mod_rows(x_ref, g_ref[...], m[0:1], m[1:2], hx_ref))
        o_ref[...] = jnp.zeros_like(o_ref)

    hx = hx_ref[...]
    oa, ob, oc = oa_ref[...], ob_ref[...], oc_ref[...]
    hc = wo_ref.shape[0]
    ga, gb, gc = _mm_nt(hx, wga_ref[...]), _mm_nt(hx, wgb_ref[...]), _mm_nt(hx, wgc_ref[...])
    ua, ub, uc = _mm(oa, wua_ref[...]), _mm(ob, wub_ref[...]), _mm(oc, wuc_ref[...])
    acc = None
    for cs in (slice(0, hc // 2), slice(hc // 2, hc)):
        y = _sigmoid(ga[:, cs]) * ua[:, cs] + _sigmoid(gb[:, cs]) * ub[:, cs] + _sigmoid(gc[:, cs]) * uc[:, cs]
        t = _mm(y.astype(BF16), wo_ref[cs, :])
        acc = t if acc is None else acc + t
    o_ref[...] += acc

    @pl.when(j == pl.num_programs(1) - 1)
    def _():
        _for_row_source(nlat, xa_ref, xb_ref, lambda x_ref: _residual_rows(x_ref, m[2:3], o_ref))


def _merge(src, mod_l, g1, oa, ob, oc, wg, wua, wub, wuc, wo, geom, rows):
    D = src[0].shape[1]
    bm = _pick_block(512, geom["T"], geom["RC"])
    hc = 512
    nj = D // hc
    sel = geom["mod_sel"](bm)
    rowb = lambda w_: pl.BlockSpec((bm, w_), lambda i, j: (i, 0))
    gate_w = lambda br: pl.BlockSpec((hc, D), lambda i, j: (br * nj + j, 0))
    nlat, spec_a, spec_b = _row_source_specs(src, bm, D, geom["RL"])
    return pl.pallas_call(
        functools.partial(_merge_kernel, nlat=nlat),
        grid=(rows // bm, nj),
        in_specs=[spec_a, spec_b,
                  pl.BlockSpec((1, N_MOD, D), lambda i, j: (sel(i), 0, 0)),
                  pl.BlockSpec((1, D), lambda i, j: (0, 0)),
                  rowb(oa.shape[1]), rowb(ob.shape[1]), rowb(oc.shape[1]),
                  gate_w(0), gate_w(1), gate_w(2),
                  pl.BlockSpec((wua.shape[0], hc), lambda i, j: (0, j)),
                  pl.BlockSpec((wub.shape[0], hc), lambda i, j: (0, j)),
                  pl.BlockSpec((wuc.shape[0], hc), lambda i, j: (0, j)),
                  pl.BlockSpec((hc, D), lambda i, j: (j, 0))],
        out_specs=rowb(D),
        out_shape=jax.ShapeDtypeStruct((rows, D), F32),
        scratch_shapes=[pltpu.VMEM((bm, D), BF16)],
        compiler_params=_params(("parallel", "arbitrary"), VMEM_BIG),
        name="merge",
    )(src[0], src[1], mod_l, g1.reshape(1, D), oa, ob, oc, wg, wg, wg, wua, wub, wuc, wo)


def _mlp_kernel(x_ref, mod_ref, g_ref, w1_ref, w2_ref, gf_ref, o_ref, hx_ref, *, final):
    j = pl.program_id(1)
    m = mod_ref[0]

    @pl.when(j == 0)
    def _():
        _norm_mod_rows(x_ref, g_ref[...], m[3:4], m[4:5], hx_ref)
        o_ref[...] = jnp.zeros_like(o_ref)

    hx = hx_ref[...]
    fc = w2_ref.shape[1]
    h1 = _mm(hx, w1_ref[0].astype(BF16))
    acc = None
    for cs in (slice(0, fc // 2), slice(fc // 2, fc)):
        h = jnp.maximum(h1[:, cs], 0.0)
        t = _mm((h * h).astype(BF16), w2_ref[0, cs, :].astype(BF16))
        acc = t if acc is None else acc + t
    o_ref[...] += acc

    @pl.when(j == pl.num_programs(1) - 1)
    def _():
        gf = gf_ref[...]
        _residual_rows(x_ref, m[5:6], o_ref, (lambda v: _rms(v, gf)) if final else None)


def _mlp(x1, mod_l, g2, w1, w2, l, g_final, geom, final):
    rows, D = x1.shape
    F = w1.shape[2]
    bm = _pick_block(1024, geom["T"], geom["RC"])
    fc = 512
    sel = geom["mod_sel"](bm)
    return pl.pallas_call(
        functools.partial(_mlp_kernel, final=final),
        grid=(rows // bm, F // fc),
        in_specs=[pl.BlockSpec((bm, D), lambda i, j: (i, 0), pipeline_mode=pl.Buffered(1)),
                  pl.BlockSpec((1, N_MOD, D), lambda i, j: (sel(i), 0, 0)),
                  pl.BlockSpec((1, D), lambda i, j: (0, 0)),
                  pl.BlockSpec((1, D, fc), lambda i, j: (l, 0, j)),
                  pl.BlockSpec((1, fc, D), lambda i, j: (l, j, 0)),
                  pl.BlockSpec((1, D), lambda i, j: (0, 0))],
        out_specs=pl.BlockSpec((bm, D), lambda i, j: (i, 0)),
        out_shape=jax.ShapeDtypeStruct((rows, D), F32),
        scratch_shapes=[pltpu.VMEM((bm, D), BF16)],
        compiler_params=_params(("parallel", "arbitrary"), VMEM_BIG),
        name="mlp",
    )(x1, mod_l, g2.reshape(1, D), w1, w2, g_final.reshape(1, D))


def _rope_tables(T, bm):
    t = np.arange(T)
    row, col = t // GRID_W, t % GRID_W

    def table(n_rot):
        n = n_rot // 2
        half = n // 2
        inv = ROPE_THETA ** (-np.arange(half, dtype=np.float64) / half)
        cos = np.zeros((T + bm, LANE))
        sin = np.zeros((T + bm, LANE))
        for s, pos in enumerate((row, col)):
            ang = pos[:, None].astype(np.float32).astype(np.float64) * inv.astype(np.float32)[None, :]
            ang = ang.astype(np.float32).astype(np.float64)
            c, sn = np.cos(ang), np.sin(ang)
            cos[:T, s * n:s * n + half] = c
            cos[:T, s * n + half:(s + 1) * n] = c
            sin[:T, s * n:s * n + half] = -sn
            sin[:T, s * n + half:(s + 1) * n] = sn
        cos[T:, :] = 1.0
        return jnp.asarray(cos, F32), jnp.asarray(sin, F32)

    cg, sg = table(GQA_HD)
    cm, sm = table(MLA_ROPE)
    return cg, sg, cm, sm


def _relayout_kernel(w_ref, wa_ref, wg_ref, *, src_off):
    for name, dst, width in (("gq", OFF_GQ, 1024), ("cq", OFF_CQ, 512), ("ckv", OFF_CKV, 512), ("gk", OFF_GK, 256),
                             ("gv", OFF_GV, 256), ("og", OFF_OG, 512), ("qkv", OFF_QKV, 1536),
                             ("kpe", OFF_MISC, MLA_ROPE), ("beta", OFF_MISC + MISC_BETA, 4 * GDN_HEADS)):
        s = src_off[name]
        wa_ref[dst:dst + width, :] = w_ref[0, s:s + width, :].astype(BF16)
    assert src_off["dec"] == src_off["beta"] + 2 * GDN_HEADS
    used = OFF_MISC + MISC_DEC + 2 * GDN_HEADS
    wa_ref[used:, :] = jnp.zeros((N_SMALL - used, wa_ref.shape[1]), BF16)
    s = src_off["bg"]
    wg_ref[...] = w_ref[0, s:s + wg_ref.shape[0], :].astype(BF16)


def _relayout_w_in(w_in, l):
    w_t = jnp.swapaxes(w_in, 1, 2)
    L, N, D = w_t.shape
    splits = (("cq", MLA_Q_RANK), ("ckv", MLA_KV_RANK), ("kpe", MLA_ROPE), ("gq", GQA_HEADS * GQA_HD),
              ("gk", GQA_KV_HEADS * GQA_HD), ("gv", GQA_KV_HEADS * GQA_HD),
              ("qkv", GDN_HEADS * (2 * GDN_DK + GDN_DV)), ("beta", 2 * GDN_HEADS), ("dec", 2 * GDN_HEADS),
              ("og", GDN_HEADS * GDN_DV), ("bg", 3 * D))
    src_off, o = {}, 0
    for name, n in splits:
        src_off[name] = o
        o += n
    assert o == N
    kc = 256
    return pl.pallas_call(
        functools.partial(_relayout_kernel, src_off=src_off),
        grid=(D // kc,),
        in_specs=[pl.BlockSpec((1, N, kc), lambda i: (l, 0, i))],
        out_specs=[pl.BlockSpec((N_SMALL, kc), lambda i: (0, i)),
                   pl.BlockSpec((3 * D, kc), lambda i: (0, i))],
        out_shape=[jax.ShapeDtypeStruct((N_SMALL, D), BF16), jax.ShapeDtypeStruct((3 * D, D), BF16)],
        compiler_params=_params(("parallel",), VMEM_BIG),
        name="w_in_relayout",
    )(w_t)


def _prep_layer_weights(w_qb, w_kvb):
    qb = w_qb.reshape(-1, MLA_HEADS, MLA_NOPE + MLA_ROPE)
    qpe = jnp.pad(qb[:, :, MLA_NOPE:], ((0, 0), (0, 0), (0, LANE - MLA_ROPE)))
    wqb = jnp.concatenate([qb[:, :, :MLA_NOPE].reshape(-1, MLA_HEADS * LANE),
                           qpe.reshape(-1, MLA_HEADS * LANE)], axis=1).astype(BF16)
    kvb = w_kvb.reshape(-1, MLA_HEADS, MLA_NOPE + MLA_V)
    wkvb = jnp.concatenate([kvb[:, :, :MLA_NOPE].reshape(-1, MLA_HEADS * LANE),
                            kvb[:, :, MLA_NOPE:].reshape(-1, MLA_HEADS * LANE)], axis=1).astype(BF16)
    return wqb, wkvb


def kernel(x, c, ctx, c_ctx, w_mod, b_mod, g_norm1, w_in, g_mla_q, w_mla_qb, g_mla_kv, w_mla_kvb, g_gqa_q, g_gqa_k,
           w_conv, a_log, dt_bias, g_gdn_out, w_up_a, w_up_b, w_up_c, w_out, g_norm2, w_ff1, w_ff2, g_final):
    B, T, D = x.shape
    C = ctx.shape[1]
    L = w_mod.shape[0]
    assert C == GDN_BLOCK and T % GDN_BLOCK == 0 and T % GRID_W == 0 and B < 8
    RL, RC = B * T, B * C
    geom = dict(B=B, T=T, C=C, RL=RL, RC=RC, R=RL + RC)
    geom["mod_sel"] = lambda bm: (lambda i: jnp.where(i < RL // bm, i // (T // bm), B))

    mod = _modulation(c, c_ctx, w_mod, b_mod)
    tabs = _rope_tables(T, _pick_block(512, T, RC))
    src = (x.reshape(RL, D), ctx.reshape(RC, D), 0)
    mla_scale = (MLA_NOPE + MLA_ROPE) ** -0.5
    gqa_scale = GQA_HD ** -0.5

    for l in range(L):
        last = l == L - 1
        wa, wg = _relayout_w_in(w_in, l)
        wqb, wkvb = _prep_layer_weights(w_mla_qb[l], w_mla_kvb[l])
        P = _in_proj(src, mod[l], g_norm1[l], wa, geom)
        qm, km, vm, qg, kg, vg = _attn_prep(P, g_mla_q[l], g_mla_kv[l], wqb, wkvb, g_gqa_q[l], g_gqa_k[l], tabs, geom)
        oa = _attention(qm, km, vm, mla_scale, geom, "attn_mla")
        ob = _attention(qg, kg, vg, gqa_scale, geom, "attn_gqa")
        qkvn, gates = _gdn_conv(P, w_conv[l], a_log[l], dt_bias[l], geom)
        u, w, qgd, kgt, qk = _gdn_local(qkvn, gates)
        o_f, o_b = _gdn_scan(u, w, qgd, kgt, qk, gates, geom)
        rows = RL if last else RL + RC
        oc = _gdn_out(o_f, o_b, P, g_gdn_out[l], rows)
        if not last:
            oa = jnp.concatenate([oa, _attention_ctx(qm, km, vm, mla_scale, geom, "attn_mla_ctx")], axis=0)
            ob = jnp.concatenate([ob, _attention_ctx(qg, kg, vg, gqa_scale, geom, "attn_gqa_ctx")], axis=0)
        x1 = _merge(src, mod[l], g_norm1[l], oa, ob, oc, wg, w_up_a[l].astype(BF16), w_up_b[l].astype(BF16),
                    w_up_c[l].astype(BF16), w_out[l].astype(BF16), geom, rows)
        xs = _mlp(x1, mod[l], g_norm2[l], w_ff1, w_ff2, l, g_final, geom, last)
        src = (xs, xs, RL)
    return xs.reshape(B, T, D)
```

```python
import functools
import math

import numpy as np
import jax
import jax.numpy as jnp
from jax import lax
from jax.experimental import pallas as pl
from jax.experimental.pallas import tpu as pltpu

F32 = jnp.float32
BF16 = jnp.bfloat16

GRID_W = 64
EPS = 1e-6
ROPE_THETA = 10000.0
MLA_HEADS, MLA_Q_RANK, MLA_KV_RANK, MLA_NOPE, MLA_ROPE, MLA_V = 4, 512, 512, 128, 64, 128
GQA_HEADS, GQA_KV_HEADS, GQA_HD = 8, 2, 128
GDN_HEADS, GDN_DK, GDN_DV, GDN_CONV, GDN_CHUNK = 4, 128, 128, 5, 64
N_MOD = 6
LANE = 128
GDN_BLOCK = 256
HALO_ROWS = 16
VMEM_BIG = 56 * 1024 * 1024

OFF_GQ, OFF_CQ, OFF_CKV, OFF_GK, OFF_GV, OFF_OG, OFF_QKV, OFF_MISC = 0, 1024, 1536, 2048, 2304, 2560, 3072, 4608
N_SMALL = 5120
MISC_BETA, MISC_DEC = 64, 72


def _mm(a, b):
    return jnp.dot(a, b, preferred_element_type=F32)


def _mm_nt(a, b):
    return lax.dot_general(a, b, (((1,), (1,)), ((), ())), preferred_element_type=F32)


def _silu(x):
    return x / (1.0 + jnp.exp(-x))


def _sigmoid(x):
    return 1.0 / (1.0 + jnp.exp(-x))


def _rms(x, g):
    return x * lax.rsqrt(jnp.mean(x * x, axis=-1, keepdims=True) + EPS) * g


def _norm_mod(x, g, shift, scale):
    return _rms(x, g) * (1.0 + scale) + shift


def _params(sem, vmem=None):
    kw = dict(dimension_semantics=sem)
    if vmem is not None:
        kw["vmem_limit_bytes"] = vmem
    return pltpu.CompilerParams(**kw)


def _pick_block(pref, *sizes):
    b = pref
    while any(s % b for s in sizes):
        b //= 2
    return b


def _mod_kernel(s_ref, w_ref, b_ref, o_ref):
    s = _silu(s_ref[...])
    o_ref[0] = _mm(s.astype(BF16), w_ref[0].astype(BF16)) + b_ref[0]


def _modulation(c, c_ctx, w_mod, b_mod):
    L, D, N = w_mod.shape
    B = c.shape[0]
    rows = jnp.zeros((8, D), F32).at[:B].set(c).at[B].set(c_ctx)
    bn = 2048
    out = pl.pallas_call(
        _mod_kernel,
        grid=(L, N // bn),
        in_specs=[pl.BlockSpec((8, D), lambda l, j: (0, 0)),
                  pl.BlockSpec((1, D, bn), lambda l, j: (l, 0, j)),
                  pl.BlockSpec((1, 1, bn), lambda l, j: (l, 0, j))],
        out_specs=pl.BlockSpec((1, 8, bn), lambda l, j: (l, 0, j)),
        out_shape=jax.ShapeDtypeStruct((L, 8, N), F32),
        compiler_params=_params(("parallel", "parallel"), VMEM_BIG),
        name="adaln_mod",
    )(rows, w_mod, b_mod.reshape(L, 1, N))
    return out.reshape(L, 8, N_MOD, D)


NORM_ROWS = 64


def _norm_mod_rows(x_ref, g, shift, scale, hx_ref):
    def body(r, carry):
        rs = pl.ds(pl.multiple_of(r * NORM_ROWS, NORM_ROWS), NORM_ROWS)
        hx_ref[rs, :] = _norm_mod(x_ref[rs, :], g, shift, scale).astype(BF16)
        return carry

    lax.fori_loop(0, x_ref.shape[0] // NORM_ROWS, body, 0)


def _row_source_specs(src, bm, D, n_latent_rows):
    nlat = n_latent_rows // bm
    off_b = src[2] // bm
    spec_a = pl.BlockSpec((bm, D), lambda i, j: (jnp.minimum(i, nlat - 1), 0))
    spec_b = pl.BlockSpec((bm, D), lambda i, j: (off_b + jnp.maximum(i - nlat, 0), 0), pipeline_mode=pl.Buffered(1))
    return nlat, spec_a, spec_b


def _for_row_source(nlat, xa_ref, xb_ref, fn):
    i = pl.program_id(0)

    @pl.when(i < nlat)
    def _():
        fn(xa_ref)

    @pl.when(i >= nlat)
    def _():
        fn(xb_ref)


def _inproj_kernel(xa_ref, xb_ref, mod_ref, g_ref, w_ref, o_ref, hx_ref, *, nlat):
    @pl.when(pl.program_id(1) == 0)
    def _():
        m = mod_ref[0]
        _for_row_source(nlat, xa_ref, xb_ref,
                        lambda x_ref: _norm_mod_rows(x_ref, g_ref[...], m[0:1], m[1:2], hx_ref))

    o_ref[...] = _mm_nt(hx_ref[...], w_ref[...]).astype(BF16)


def _in_proj(src, mod_l, g1, wa, geom):
    R, D = geom["R"], src[0].shape[1]
    N = wa.shape[0]
    bm = _pick_block(1024, geom["T"], geom["RC"])
    bn = 1024
    sel = geom["mod_sel"](bm)
    nlat, spec_a, spec_b = _row_source_specs(src, bm, D, geom["RL"])
    return pl.pallas_call(
        functools.partial(_inproj_kernel, nlat=nlat),
        grid=(R // bm, N // bn),
        in_specs=[spec_a, spec_b,
                  pl.BlockSpec((1, N_MOD, D), lambda i, j: (sel(i), 0, 0)),
                  pl.BlockSpec((1, D), lambda i, j: (0, 0)),
                  pl.BlockSpec((bn, D), lambda i, j: (j, 0))],
        out_specs=pl.BlockSpec((bm, bn), lambda i, j: (i, j)),
        out_shape=jax.ShapeDtypeStruct((R, N), BF16),
        scratch_shapes=[pltpu.VMEM((bm, D), BF16)],
        compiler_params=_params(("parallel", "arbitrary"), VMEM_BIG),
        name="in_proj",
    )(src[0], src[1], mod_l, g1.reshape(1, D), wa)


def _prep_kernel(gq_ref, cq_ref, ckv_ref, gk_ref, gv_ref, misc_ref, gmq_ref, gmkv_ref, wqb_ref, wkvb_ref,
                 ggq_ref, ggk_ref, cosg_ref, sing_ref, cosm_ref, sinm_ref,
                 qm_ref, km_ref, vm_ref, qg_ref, kg_ref, vg_ref):
    lane = lax.broadcasted_iota(jnp.int32, (1, LANE), 1)
    cosm, sinm = cosm_ref[...], sinm_ref[...]
    cosg, sing = cosg_ref[...], sing_ref[...]
    first_m = (lane & 31) < 16
    first_g = (lane & 63) < 32
    ones_col = jnp.where(lane == 0, 1.0, 0.0).astype(BF16) + jnp.zeros((cq_ref.shape[0], LANE), BF16)

    def rope_m(x):
        partner = jnp.where(first_m, pltpu.roll(x, LANE - 16, 1), pltpu.roll(x, 16, 1))
        return x * cosm + partner * sinm

    def rope_g(x):
        partner = jnp.where(first_g, pltpu.roll(x, LANE - 32, 1), pltpu.roll(x, 32, 1))
        return x * cosg + partner * sing

    qa = _mm(_rms(cq_ref[...].astype(F32), gmq_ref[...]).astype(BF16), wqb_ref[...])
    kva = _mm(_rms(ckv_ref[...].astype(F32), gmkv_ref[...]).astype(BF16), wkvb_ref[...])
    kpe = rope_m(jnp.where(lane < MLA_ROPE, misc_ref[...].astype(F32), 0.0)).astype(BF16)
    nh = MLA_HEADS * LANE
    for h in range(MLA_HEADS):
        hs = slice(h * LANE, (h + 1) * LANE)
        ps = slice(nh + h * LANE, nh + (h + 1) * LANE)
        qm_ref[h, :, 0:LANE] = qa[:, hs].astype(BF16)
        qm_ref[h, :, LANE:2 * LANE] = rope_m(qa[:, ps]).astype(BF16)
        km_ref[h, :, 0:LANE] = kva[:, hs].astype(BF16)
        km_ref[h, :, LANE:2 * LANE] = kpe
        vm_ref[h, :, 0:LANE] = kva[:, ps].astype(BF16)
        vm_ref[h, :, LANE:2 * LANE] = ones_col

    ggq, ggk = ggq_ref[...], ggk_ref[...]
    for h in range(GQA_HEADS):
        hs = slice(h * LANE, (h + 1) * LANE)
        qg_ref[h] = rope_g(_rms(gq_ref[:, hs].astype(F32), ggq)).astype(BF16)
    for h in range(GQA_KV_HEADS):
        hs = slice(h * LANE, (h + 1) * LANE)
        kg_ref[h] = rope_g(_rms(gk_ref[:, hs].astype(F32), ggk)).astype(BF16)
        vg_ref[h, :, 0:LANE] = gv_ref[:, hs]
        vg_ref[h, :, LANE:2 * LANE] = ones_col


def _attn_prep(P, gmq, gmkv, wqb, wkvb, ggq, ggk, tabs, geom):
    R = P.shape[0]
    bm = _pick_block(512, geom["T"], geom["RC"])
    nlat = geom["RL"] // bm
    per = geom["T"] // bm
    tsel = lambda i: jnp.where(i < nlat, i % per, per)
    col = lambda w, off: pl.BlockSpec((bm, w), lambda i: (i, off // w))
    full = lambda a: pl.BlockSpec(a.shape, lambda i: (0,) * a.ndim)
    tab = pl.BlockSpec((bm, LANE), lambda i: (tsel(i), 0))
    outs = [((MLA_HEADS, R, 2 * LANE), 2 * LANE), ((MLA_HEADS, R, 2 * LANE), 2 * LANE),
            ((MLA_HEADS, R, 2 * LANE), 2 * LANE), ((GQA_HEADS, R, LANE), LANE), ((GQA_KV_HEADS, R, LANE), LANE),
            ((GQA_KV_HEADS, R, 2 * LANE), 2 * LANE)]
    gmq, gmkv, ggq, ggk = (a.reshape(1, -1) for a in (gmq, gmkv, ggq, ggk))
    return pl.pallas_call(
        _prep_kernel,
        grid=(R // bm,),
        in_specs=[col(1024, OFF_GQ), col(512, OFF_CQ), col(512, OFF_CKV), col(256, OFF_GK), col(256, OFF_GV),
                  col(LANE, OFF_MISC), full(gmq), full(gmkv), full(wqb), full(wkvb), full(ggq), full(ggk),
                  tab, tab, tab, tab],
        out_specs=[pl.BlockSpec((s[0], bm, w), lambda i: (0, i, 0)) for s, w in outs],
        out_shape=[jax.ShapeDtypeStruct(s, BF16) for s, _ in outs],
        compiler_params=_params(("parallel",), VMEM_BIG),
        name="attn_prep",
    )(P, P, P, P, P, P, gmq, gmkv, wqb, wkvb, ggq, ggk, *tabs)


def _softmax_pv(q, kv_chunks, s_ref, o_ref, exp2_scale):
    sub = min(q.shape)
    for r0 in range(0, q.shape[0], sub):
        rows = slice(r0, r0 + sub)
        qr = q[rows]
        m_acc = None
        off = 0
        for k_ref, _, st, n in kv_chunks:
            s = _mm_nt(qr, k_ref[0, st:st + n, :])
            s_ref[rows, off:off + n] = s
            for t in range(n // LANE):
                tile = s[:, t * LANE:(t + 1) * LANE]
                m_acc = tile if m_acc is None else jnp.maximum(m_acc, tile)
            off += n
        m = jnp.max(m_acc, axis=-1, keepdims=True)
        acc = None
        off = 0
        for _, v_ref, st, n in kv_chunks:
            p = jnp.exp2((s_ref[rows, off:off + n] - m) * exp2_scale).astype(BF16)
            pv = _mm(p, v_ref[0, st:st + n, :])
            acc = pv if acc is None else acc + pv
            off += n
        o_ref[rows, :] = (acc[:, :LANE] / acc[:, LANE:LANE + 1]).astype(BF16)


def _attn_kernel(q_ref, kx_ref, kc_ref, vx_ref, vc_ref, o_ref, s_ref, *, exp2_scale, kchunk):
    chunks = [(kx_ref, vx_ref, i * kchunk, kchunk) for i in range(kx_ref.shape[1] // kchunk)]
    chunks.append((kc_ref, vc_ref, 0, kc_ref.shape[1]))
    _softmax_pv(q_ref[0], chunks, s_ref, o_ref, exp2_scale)


def _attn_ctx_kernel(q_ref, kc_ref, vc_ref, dst_ref, o_ref, s_ref, *, exp2_scale):
    del dst_ref
    _softmax_pv(q_ref[0], [(kc_ref, vc_ref, 0, kc_ref.shape[1])], s_ref, o_ref, exp2_scale)


def _attention(q, k, v, scale, geom, name, rows_out):
    H, _, dk = q.shape
    Hk, _, dve = v.shape
    grp = H // Hk
    B, T, C, RL = geom["B"], geom["T"], geom["C"], geom["RL"]
    bq = _pick_block(1024, T)
    nq = T // bq
    cb = RL // C
    return pl.pallas_call(
        functools.partial(_attn_kernel, exp2_scale=scale * math.log2(math.e), kchunk=256),
        grid=(B, H, nq),
        in_specs=[pl.BlockSpec((1, bq, dk), lambda b, h, j: (h, b * nq + j, 0)),
                  pl.BlockSpec((1, T, dk), lambda b, h, j: (h // grp, b, 0)),
                  pl.BlockSpec((1, C, dk), lambda b, h, j: (h // grp, cb + b, 0)),
                  pl.BlockSpec((1, T, dve), lambda b, h, j: (h // grp, b, 0)),
                  pl.BlockSpec((1, C, dve), lambda b, h, j: (h // grp, cb + b, 0))],
        out_specs=pl.BlockSpec((bq, LANE), lambda b, h, j: (b * nq + j, h)),
        out_shape=jax.ShapeDtypeStruct((rows_out, H * LANE), BF16),
        scratch_shapes=[pltpu.VMEM((bq, T + C), F32)],
        compiler_params=_params(("parallel", "parallel", "parallel"), VMEM_BIG),
        name=name,
    )(q, k, k, v, v)


def _attention_ctx(q, k, v, scale, geom, name, o_latent):
    H, _, dk = q.shape
    Hk, _, dve = v.shape
    grp = H // Hk
    B, C, RL, RC = geom["B"], geom["C"], geom["RL"], geom["RC"]
    cb = RL // C
    return pl.pallas_call(
        functools.partial(_attn_ctx_kernel, exp2_scale=scale * math.log2(math.e)),
        grid=(B, H),
        in_specs=[pl.BlockSpec((1, C, dk), lambda b, h: (h, cb + b, 0)),
                  pl.BlockSpec((1, C, dk), lambda b, h: (h // grp, cb + b, 0)),
                  pl.BlockSpec((1, C, dve), lambda b, h: (h // grp, cb + b, 0)),
                  pl.BlockSpec(memory_space=pl.ANY)],
        out_specs=pl.BlockSpec((C, LANE), lambda b, h: (cb + b, h)),
        out_shape=jax.ShapeDtypeStruct(o_latent.shape, BF16),
        input_output_aliases={3: 0},
        scratch_shapes=[pltpu.VMEM((C, C), F32)],
        compiler_params=_params(("parallel", "parallel")),
        name=name,
    )(q, k, v, o_latent)


def _gdn_conv_gates(cur_ref, prev_ref, next_ref, misc_ref, wc_ref, gvec_ref, ext_ref, *, nlat, per_seq):
    r = pl.program_id(0)
    pos = r % per_seq
    is_ctx = r >= nlat
    pf = jnp.where(jnp.logical_or(is_ctx, pos == 0), 0.0, 1.0).astype(F32)
    nf = jnp.where(jnp.logical_or(is_ctx, pos == per_seq - 1), 0.0, 1.0).astype(F32)
    nb = GDN_BLOCK
    hal = HALO_ROWS
    ext_ref[0:hal, :] = prev_ref[...].astype(F32) * pf
    ext_ref[hal:hal + nb, :] = cur_ref[...].astype(F32)
    ext_ref[hal + nb:2 * hal + nb, :] = next_ref[...].astype(F32) * nf
    pad = GDN_CONV // 2
    n_qk = 2 * GDN_HEADS
    heads = []
    for c in range(3 * GDN_HEADS):
        cs = slice(c * LANE, (c + 1) * LANE)
        acc = None
        for j in range(GDN_CONV):
            t = ext_ref[hal - pad + j:hal - pad + j + nb, cs] * wc_ref[j:j + 1, cs]
            acc = t if acc is None else acc + t
        y = _silu(acc)
        if c < n_qk:
            y = y * lax.rsqrt(jnp.sum(y * y, axis=-1, keepdims=True) + EPS)
        heads.append(y)

    lane = lax.broadcasted_iota(jnp.int32, (1, LANE), 1)
    raw = misc_ref[...].astype(F32)
    beta = _sigmoid(raw)
    z = raw + gvec_ref[1:2, :]
    softplus = jnp.maximum(z, 0.0) + jnp.log1p(jnp.exp(-jnp.abs(z)))
    g = -jnp.exp(gvec_ref[0:1, :]) * softplus
    is_beta = jnp.logical_and(lane >= MISC_BETA, lane < MISC_DEC)
    is_g = jnp.logical_and(lane >= MISC_DEC, lane < MISC_DEC + 2 * GDN_HEADS)
    return heads, jnp.where(is_beta, beta, jnp.where(is_g, g, 0.0))


def _split3(x):
    h = x.astype(BF16)
    r = x - h.astype(F32)
    m = r.astype(BF16)
    l = (r - m.astype(F32)).astype(BF16)
    return h, m, l


def _gdn_chunk_local(heads, gt, u_ref, w_ref, qg_ref, kgt_ref, qk_ref):
    nb = GDN_BLOCK
    nh = GDN_HEADS
    ri = lax.broadcasted_iota(jnp.int32, (nb, nb), 0)
    ci = lax.broadcasted_iota(jnp.int32, (nb, nb), 1)
    same = (ri >> 6) == (ci >> 6)
    low = jnp.logical_and(same, ri >= ci)
    upp = jnp.logical_and(same, ri <= ci)
    slow = jnp.logical_and(same, ri > ci)
    supp = jnp.logical_and(same, ri < ci)
    ltri = jnp.where(low, 1.0, 0.0).astype(BF16)
    utri = jnp.where(upp, 1.0, 0.0).astype(BF16)
    eye = jnp.where(ri == ci, 1.0, 0.0)
    pair = (ri >> 1) == (ci >> 1)
    offs = [jnp.logical_and((ri >> (lv + 1)) == (ci >> (lv + 1)), (ri >> lv) != (ci >> lv))
            for lv in range(1, int(math.log2(GDN_CHUNK)))]

    g3 = _split3(gt)
    g3t = _split3(gt.T)
    cum_c = (sum(_mm(ltri, p) for p in reversed(g3)), sum(_mm(utri, p) for p in reversed(g3)))
    cum_r = (sum(_mm(p, utri) for p in reversed(g3t)), sum(_mm(p, ltri) for p in reversed(g3t)))
    tot_c = cum_c[0] + cum_c[1] - gt

    scale = GDN_DK ** -0.5
    insts = [(d, h) for d in range(2) for h in range(nh)]
    a_all, tinv_all, rhs_all = [], [], []
    for d, h in insts:
        mask, smask = (low, slow) if d == 0 else (upp, supp)
        ib = MISC_BETA + d * nh + h
        ig = MISC_DEC + d * nh + h
        beta = gt[:, ib:ib + 1]
        gc = cum_c[d][:, ig:ig + 1]
        gr = cum_r[d][ig:ig + 1, :]
        gl = tot_c[:, ig:ig + 1]
        q, k, v = heads[h], heads[nh + h], heads[2 * nh + h]
        decay = jnp.where(mask, jnp.exp(jnp.where(mask, gc - gr, 0.0)), 0.0)
        kb = k * beta
        k16 = k.astype(BF16)
        a = jnp.where(smask, _mm_nt(kb.astype(BF16), k16) * decay, 0.0)
        a_all.append(a.astype(BF16))
        tinv_all.append(eye - jnp.where(pair, a, 0.0))
        rhs_all.append(jnp.concatenate([v * beta, kb * jnp.exp(gc)], axis=1).astype(BF16))
        hs = slice(h * LANE, (h + 1) * LANE)
        qs = q * scale
        qk_ref[d, :, h * nb:(h + 1) * nb] = (_mm_nt(qs.astype(BF16), k16) * decay).astype(BF16)
        qg_ref[d, :, hs] = (qs * jnp.exp(gc)).astype(BF16)
        kgt_ref[d, hs, :] = (k * jnp.exp(gl - gc)).T.astype(BF16)

    zero16 = jnp.zeros((nb, nb), BF16)
    for off in offs:
        for i in range(len(insts)):
            t16 = tinv_all[i].astype(BF16)
            ta = _mm(t16, jnp.where(off, a_all[i], zero16))
            tinv_all[i] = tinv_all[i] - _mm(ta.astype(BF16), t16)
    for i, (d, h) in enumerate(insts):
        x = _mm(tinv_all[i].astype(BF16), rhs_all[i])
        hs = slice(h * LANE, (h + 1) * LANE)
        u_ref[d, :, hs] = x[:, :LANE]
        w_ref[d, :, hs] = x[:, LANE:].astype(BF16)


def _gdn_local_kernel(cur_ref, prev_ref, next_ref, misc_ref, wc_ref, gvec_ref,
                      gates_ref, u_ref, w_ref, qg_ref, kgt_ref, qk_ref, ext_ref, *, nlat, per_seq):
    heads, gt = _gdn_conv_gates(cur_ref, prev_ref, next_ref, misc_ref, wc_ref, gvec_ref, ext_ref,
                                nlat=nlat, per_seq=per_seq)
    gates_ref[...] = gt
    _gdn_chunk_local(heads, gt, u_ref, w_ref, qg_ref, kgt_ref, qk_ref)


def _gdn_local(P, w_conv, a_log, dt_bias, geom):
    R = P.shape[0]
    nb = GDN_BLOCK
    HW = GDN_HEADS * LANE
    W = 3 * HW
    hal = HALO_ROWS
    sub = nb // hal
    gvec = jnp.zeros((2, LANE), F32)
    gvec = gvec.at[0, MISC_DEC:MISC_DEC + 2 * GDN_HEADS].set(a_log.reshape(-1))
    gvec = gvec.at[1, MISC_DEC:MISC_DEC + 2 * GDN_HEADS].set(dt_bias.reshape(-1))
    qc = OFF_QKV // W
    return pl.pallas_call(
        functools.partial(_gdn_local_kernel, nlat=geom["RL"] // nb, per_seq=geom["T"] // nb),
        grid=(R // nb,),
        in_specs=[pl.BlockSpec((nb, W), lambda r: (r, qc)),
                  pl.BlockSpec((hal, W), lambda r: (jnp.maximum(r * sub - 1, 0), qc)),
                  pl.BlockSpec((hal, W), lambda r: (jnp.minimum((r + 1) * sub, R // hal - 1), qc)),
                  pl.BlockSpec((nb, LANE), lambda r: (r, OFF_MISC // LANE)),
                  pl.BlockSpec((GDN_CONV, W), lambda r: (0, 0)),
                  pl.BlockSpec((2, LANE), lambda r: (0, 0))],
        out_specs=[pl.BlockSpec((nb, LANE), lambda r: (r, 0)),
                   pl.BlockSpec((2, nb, HW), lambda r: (0, r, 0)),
                   pl.BlockSpec((2, nb, HW), lambda r: (0, r, 0)),
                   pl.BlockSpec((2, nb, HW), lambda r: (0, r, 0)),
                   pl.BlockSpec((2, HW, nb), lambda r: (0, 0, r)),
                   pl.BlockSpec((2, nb, GDN_HEADS * nb), lambda r: (0, r, 0))],
        out_shape=[jax.ShapeDtypeStruct((R, LANE), F32),
                   jax.ShapeDtypeStruct((2, R, HW), F32),
                   jax.ShapeDtypeStruct((2, R, HW), BF16),
                   jax.ShapeDtypeStruct((2, R, HW), BF16),
                   jax.ShapeDtypeStruct((2, HW, R), BF16),
                   jax.ShapeDtypeStruct((2, R, GDN_HEADS * nb), BF16)],
        scratch_shapes=[pltpu.VMEM((nb + 2 * hal, W), F32)],
        compiler_params=_params(("parallel",), VMEM_BIG),
        name="gdn_local",
    )(P, P, P, P, w_conv, gvec)


def _gdn_c_kernel(*refs):
    nb = GDN_BLOCK
    nh = GDN_HEADS
    ins, (of_ref, ob_ref, s_ref, vn_ref) = (refs[0:6], refs[6:12]), refs[12:]
    o_refs = (of_ref, ob_ref)

    @pl.when(pl.program_id(1) == 0)
    def _():
        s_ref[...] = jnp.zeros_like(s_ref)
        vn_ref[...] = jnp.zeros_like(vn_ref)

    rowi = lax.broadcasted_iota(jnp.int32, (nb, 1), 0)
    nchunk = nb // GDN_CHUNK
    for step in range(nchunk):
        for d in range(2):
            u_ref, w_ref, qg_ref, kgt_ref, qk_ref, gates_ref = ins[d]
            c = step if d == 0 else nchunk - 1 - step
            rs = slice(c * GDN_CHUNK, (c + 1) * GDN_CHUNK)
            decay_all = jnp.exp(jnp.sum(gates_ref[rs, :], axis=0, keepdims=True))
            cmask = jnp.logical_and(rowi >= c * GDN_CHUNK, rowi < (c + 1) * GDN_CHUNK)
            for h in range(nh):
                hs = slice(h * LANE, (h + 1) * LANE)
                ig = MISC_DEC + d * nh + h
                s = s_ref[d, h]
                s16 = s.astype(BF16)
                v_new = u_ref[0, rs, hs] - _mm(w_ref[0, rs, hs], s16)
                vn_ref[d, h, rs, :] = v_new.astype(BF16)
                vn = vn_ref[d, h]
                o_refs[d][rs, hs] = _mm(qg_ref[0, rs, hs], s16) + _mm(qk_ref[0, rs, h * nb:(h + 1) * nb], vn)
                v_cur = jnp.where(cmask, vn, jnp.zeros_like(vn))
                s_ref[d, h] = s * decay_all[:, ig:ig + 1] + _mm(kgt_ref[0, hs, :], v_cur)


def _gdn_scan(u, w, qg, kgt, qk, gates, geom):
    nb = GDN_BLOCK
    HW = GDN_HEADS * LANE
    B, RL, R = geom["B"], geom["RL"], geom["R"]
    per = geom["T"] // nb
    nlat = RL // nb

    def blk(d):
        return lambda b, j: jnp.where(j == 0, nlat + b, b * per + (j - 1 if d == 0 else per - j))

    def dir_specs(d):
        bk = blk(d)
        tok = lambda w_: pl.BlockSpec((1, nb, w_), lambda b, j: (d, bk(b, j), 0))
        return [tok(HW), tok(HW), tok(HW),
                pl.BlockSpec((1, HW, nb), lambda b, j: (d, 0, bk(b, j))),
                tok(GDN_HEADS * nb),
                pl.BlockSpec((nb, LANE), lambda b, j: (bk(b, j), 0))]

    args = (u, w, qg, kgt, qk, gates)
    return pl.pallas_call(
        _gdn_c_kernel,
        grid=(B, per + 1),
        in_specs=dir_specs(0) + dir_specs(1),
        out_specs=[pl.BlockSpec((nb, HW), lambda b, j: (blk(0)(b, j), 0)),
                   pl.BlockSpec((nb, HW), lambda b, j: (blk(1)(b, j), 0))],
        out_shape=[jax.ShapeDtypeStruct((R, HW), F32), jax.ShapeDtypeStruct((R, HW), F32)],
        scratch_shapes=[pltpu.VMEM((2, GDN_HEADS, GDN_DK, GDN_DV), F32),
                        pltpu.VMEM((2, GDN_HEADS, nb, GDN_DV), BF16)],
        compiler_params=_params(("parallel", "arbitrary")),
        name="gdn_scan",
    )(*args, *args)


def _gdn_out_kernel(of_ref, ob_ref, og_ref, g_ref, o_ref):
    o = of_ref[...] + ob_ref[...]
    gate = _silu(og_ref[...].astype(F32))
    g = g_ref[...]
    for h in range(GDN_HEADS):
        hs = slice(h * LANE, (h + 1) * LANE)
        o_ref[:, hs] = (_rms(o[:, hs], g) * gate[:, hs]).astype(BF16)


def _gdn_out(o_f, o_b, P, g_out, rows):
    HW = GDN_HEADS * LANE
    bm = 512 if rows % 512 == 0 else 256
    return pl.pallas_call(
        _gdn_out_kernel,
        grid=(rows // bm,),
        in_specs=[pl.BlockSpec((bm, HW), lambda i: (i, 0)),
                  pl.BlockSpec((bm, HW), lambda i: (i, 0)),
                  pl.BlockSpec((bm, HW), lambda i: (i, OFF_OG // HW)),
                  pl.BlockSpec((1, LANE), lambda i: (0, 0))],
        out_specs=pl.BlockSpec((bm, HW), lambda i: (i, 0)),
        out_shape=jax.ShapeDtypeStruct((rows, HW), BF16),
        compiler_params=_params(("parallel",)),
        name="gdn_out",
    )(o_f, o_b, P, g_out.reshape(1, LANE))


def _residual_rows(x_ref, gate, o_ref, post=None):
    def body(r, carry):
        rs = pl.ds(pl.multiple_of(r * NORM_ROWS, NORM_ROWS), NORM_ROWS)
        v = x_ref[rs, :] + gate * o_ref[rs, :]
        o_ref[rs, :] = v if post is None else post(v)
        return carry

    lax.fori_loop(0, x_ref.shape[0] // NORM_ROWS, body, 0)


def _merge_kernel(xa_ref, xb_ref, mod_ref, g_ref, oa_ref, ob_ref, oc_ref, wga_ref, wgb_ref, wgc_ref, wua_ref,
                  wub_ref, wuc_ref, wo_ref, o_ref, hx_ref, *, nlat):
    j = pl.program_id(1)
    m = mod_ref[0]

    @pl.when(j == 0)
    def _():
        _for_row_source(nlat, xa_ref, xb_ref,
                        lambda x_ref: _norm_mod_rows(x_ref, g_ref[...], m[0:1], m[1:2], hx_ref))
        o_ref[...] = jnp.zeros_like(o_ref)

    hx = hx_ref[...]
    oa, ob, oc = oa_ref[...], ob_ref[...], oc_ref[...]
    hc = wo_ref.shape[0]
    ga, gb, gc = _mm_nt(hx, wga_ref[...]), _mm_nt(hx, wgb_ref[...]), _mm_nt(hx, wgc_ref[...])
    ua, ub, uc = _mm(oa, wua_ref[...]), _mm(ob, wub_ref[...]), _mm(oc, wuc_ref[...])
    acc = None
    for cs in (slice(0, hc // 2), slice(hc // 2, hc)):
        y = _sigmoid(ga[:, cs]) * ua[:, cs] + _sigmoid(gb[:, cs]) * ub[:, cs] + _sigmoid(gc[:, cs]) * uc[:, cs]
        t = _mm(y.astype(BF16), wo_ref[cs, :])
        acc = t if acc is None else acc + t
    o_ref[...] += acc

    @pl.when(j == pl.num_programs(1) - 1)
    def _():
        _for_row_source(nlat, xa_ref, xb_ref, lambda x_ref: _residual_rows(x_ref, m[2:3], o_ref))


def _merge(src, mod_l, g1, oa, ob, oc, wg, wua, wub, wuc, wo, geom, rows):
    D = src[0].shape[1]
    bm = _pick_block(512, geom["T"], geom["RC"])
    hc = 512
    nj = D // hc
    sel = geom["mod_sel"](bm)
    rowb = lambda w_: pl.BlockSpec((bm, w_), lambda i, j: (i, 0))
    gate_w = lambda br: pl.BlockSpec((hc, D), lambda i, j: (br * nj + j, 0))
    nlat, spec_a, spec_b = _row_source_specs(src, bm, D, geom["RL"])
    return pl.pallas_call(
        functools.partial(_merge_kernel, nlat=nlat),
        grid=(rows // bm, nj),
        in_specs=[spec_a, spec_b,
                  pl.BlockSpec((1, N_MOD, D), lambda i, j: (sel(i), 0, 0)),
                  pl.BlockSpec((1, D), lambda i, j: (0, 0)),
                  rowb(oa.shape[1]), rowb(ob.shape[1]), rowb(oc.shape[1]),
                  gate_w(0), gate_w(1), gate_w(2),
                  pl.BlockSpec((wua.shape[0], hc), lambda i, j: (0, j)),
                  pl.BlockSpec((wub.shape[0], hc), lambda i, j: (0, j)),
                  pl.BlockSpec((wuc.shape[0], hc), lambda i, j: (0, j)),
                  pl.BlockSpec((hc, D), lambda i, j: (j, 0))],
        out_specs=rowb(D),
        out_shape=jax.ShapeDtypeStruct((rows, D), F32),
        scratch_shapes=[pltpu.VMEM((bm, D), BF16)],
        compiler_params=_params(("parallel", "arbitrary"), VMEM_BIG),
        name="merge",
    )(src[0], src[1], mod_l, g1.reshape(1, D), oa, ob, oc, wg, wg, wg, wua, wub, wuc, wo)


def _mlp_kernel(x_ref, mod_ref, g_ref, w1_ref, w2_ref, gf_ref, o_ref, hx_ref, *, final):
    j = pl.program_id(1)
    m = mod_ref[0]

    @pl.when(j == 0)
    def _():
        _norm_mod_rows(x_ref, g_ref[...], m[3:4], m[4:5], hx_ref)
        o_ref[...] = jnp.zeros_like(o_ref)

    hx = hx_ref[...]
    fc = w2_ref.shape[1]
    h1 = _mm(hx, w1_ref[0].astype(BF16))
    acc = None
    for cs in (slice(0, fc // 2), slice(fc // 2, fc)):
        h = jnp.maximum(h1[:, cs], 0.0)
        t = _mm((h * h).astype(BF16), w2_ref[0, cs, :].astype(BF16))
        acc = t if acc is None else acc + t
    o_ref[...] += acc

    @pl.when(j == pl.num_programs(1) - 1)
    def _():
        gf = gf_ref[...]
        _residual_rows(x_ref, m[5:6], o_ref, (lambda v: _rms(v, gf)) if final else None)


def _mlp(x1, mod_l, g2, w1, w2, l, g_final, geom, final):
    rows, D = x1.shape
    F = w1.shape[2]
    bm = _pick_block(1024, geom["T"], geom["RC"])
    fc = 512
    sel = geom["mod_sel"](bm)
    return pl.pallas_call(
        functools.partial(_mlp_kernel, final=final),
        grid=(rows // bm, F // fc),
        in_specs=[pl.BlockSpec((bm, D), lambda i, j: (i, 0), pipeline_mode=pl.Buffered(1)),
                  pl.BlockSpec((1, N_MOD, D), lambda i, j: (sel(i), 0, 0)),
                  pl.BlockSpec((1, D), lambda i, j: (0, 0)),
                  pl.BlockSpec((1, D, fc), lambda i, j: (l, 0, j)),
                  pl.BlockSpec((1, fc, D), lambda i, j: (l, j, 0)),
                  pl.BlockSpec((1, D), lambda i, j: (0, 0))],
        out_specs=pl.BlockSpec((bm, D), lambda i, j: (i, 0)),
        out_shape=jax.ShapeDtypeStruct((rows, D), F32),
        scratch_shapes=[pltpu.VMEM((bm, D), BF16)],
        compiler_params=_params(("parallel", "arbitrary"), VMEM_BIG),
        name="mlp",
    )(x1, mod_l, g2.reshape(1, D), w1, w2, g_final.reshape(1, D))


def _rope_tables(T, bm):
    t = np.arange(T)
    row, col = t // GRID_W, t % GRID_W

    def table(n_rot):
        n = n_rot // 2
        half = n // 2
        inv = ROPE_THETA ** (-np.arange(half, dtype=np.float64) / half)
        cos = np.zeros((T + bm, LANE))
        sin = np.zeros((T + bm, LANE))
        for s, pos in enumerate((row, col)):
            ang = pos[:, None].astype(np.float32).astype(np.float64) * inv.astype(np.float32)[None, :]
            ang = ang.astype(np.float32).astype(np.float64)
            c, sn = np.cos(ang), np.sin(ang)
            cos[:T, s * n:s * n + half] = c
            cos[:T, s * n + half:(s + 1) * n] = c
            sin[:T, s * n:s * n + half] = -sn
            sin[:T, s * n + half:(s + 1) * n] = sn
        cos[T:, :] = 1.0
        return jnp.asarray(cos, F32), jnp.asarray(sin, F32)

    cg, sg = table(GQA_HD)
    cm, sm = table(MLA_ROPE)
    return cg, sg, cm, sm


def _relayout_kernel(w_ref, wa_ref, wg_ref, *, src_off):
    for name, dst, width in (("gq", OFF_GQ, 1024), ("cq", OFF_CQ, 512), ("ckv", OFF_CKV, 512), ("gk", OFF_GK, 256),
                             ("gv", OFF_GV, 256), ("og", OFF_OG, 512), ("qkv", OFF_QKV, 1536),
                             ("kpe", OFF_MISC, MLA_ROPE), ("beta", OFF_MISC + MISC_BETA, 4 * GDN_HEADS)):
        s = src_off[name]
        wa_ref[dst:dst + width, :] = w_ref[0, s:s + width, :].astype(BF16)
    assert src_off["dec"] == src_off["beta"] + 2 * GDN_HEADS
    used = OFF_MISC + MISC_DEC + 2 * GDN_HEADS
    wa_ref[used:, :] = jnp.zeros((N_SMALL - used, wa_ref.shape[1]), BF16)
    s = src_off["bg"]
    wg_ref[...] = w_ref[0, s:s + wg_ref.shape[0], :].astype(BF16)


def _relayout_w_in(w_in, l):
    w_t = jnp.swapaxes(w_in, 1, 2)
    L, N, D = w_t.shape
    splits = (("cq", MLA_Q_RANK), ("ckv", MLA_KV_RANK), ("kpe", MLA_ROPE), ("gq", GQA_HEADS * GQA_HD),
              ("gk", GQA_KV_HEADS * GQA_HD), ("gv", GQA_KV_HEADS * GQA_HD),
              ("qkv", GDN_HEADS * (2 * GDN_DK + GDN_DV)), ("beta", 2 * GDN_HEADS), ("dec", 2 * GDN_HEADS),
              ("og", GDN_HEADS * GDN_DV), ("bg", 3 * D))
    src_off, o = {}, 0
    for name, n in splits:
        src_off[name] = o
        o += n
    assert o == N
    kc = 256
    return pl.pallas_call(
        functools.partial(_relayout_kernel, src_off=src_off),
        grid=(D // kc,),
        in_specs=[pl.BlockSpec((1, N, kc), lambda i: (l, 0, i))],
        out_specs=[pl.BlockSpec((N_SMALL, kc), lambda i: (0, i)),
                   pl.BlockSpec((3 * D, kc), lambda i: (0, i))],
        out_shape=[jax.ShapeDtypeStruct((N_SMALL, D), BF16), jax.ShapeDtypeStruct((3 * D, D), BF16)],
        compiler_params=_params(("parallel",), VMEM_BIG),
        name="w_in_relayout",
    )(w_t)


def _prep_layer_weights(w_qb, w_kvb):
    qb = w_qb.reshape(-1, MLA_HEADS, MLA_NOPE + MLA_ROPE)
    qpe = jnp.pad(qb[:, :, MLA_NOPE:], ((0, 0), (0, 0), (0, LANE - MLA_ROPE)))
    wqb = jnp.concatenate([qb[:, :, :MLA_NOPE].reshape(-1, MLA_HEADS * LANE),
                           qpe.reshape(-1, MLA_HEADS * LANE)], axis=1).astype(BF16)
    kvb = w_kvb.reshape(-1, MLA_HEADS, MLA_NOPE + MLA_V)
    wkvb = jnp.concatenate([kvb[:, :, :MLA_NOPE].reshape(-1, MLA_HEADS * LANE),
                            kvb[:, :, MLA_NOPE:].reshape(-1, MLA_HEADS * LANE)], axis=1).astype(BF16)
    return wqb, wkvb


def kernel(x, c, ctx, c_ctx, w_mod, b_mod, g_norm1, w_in, g_mla_q, w_mla_qb, g_mla_kv, w_mla_kvb, g_gqa_q, g_gqa_k,
           w_conv, a_log, dt_bias, g_gdn_out, w_up_a, w_up_b, w_up_c, w_out, g_norm2, w_ff1, w_ff2, g_final):
    B, T, D = x.shape
    C = ctx.shape[1]
    L = w_mod.shape[0]
    assert C == GDN_BLOCK and T % GDN_BLOCK == 0 and T % GRID_W == 0 and B < 8
    RL, RC = B * T, B * C
    geom = dict(B=B, T=T, C=C, RL=RL, RC=RC, R=RL + RC)
    geom["mod_sel"] = lambda bm: (lambda i: jnp.where(i < RL // bm, i // (T // bm), B))

    mod = _modulation(c, c_ctx, w_mod, b_mod)
    tabs = _rope_tables(T, _pick_block(512, T, RC))
    src = (x.reshape(RL, D), ctx.reshape(RC, D), 0)
    mla_scale = (MLA_NOPE + MLA_ROPE) ** -0.5
    gqa_scale = GQA_HD ** -0.5

    for l in range(L):
        last = l == L - 1
        wa, wg = _relayout_w_in(w_in, l)
        wqb, wkvb = _prep_layer_weights(w_mla_qb[l], w_mla_kvb[l])
        P = _in_proj(src, mod[l], g_norm1[l], wa, geom)
        qm, km, vm, qg, kg, vg = _attn_prep(P, g_mla_q[l], g_mla_kv[l], wqb, wkvb, g_gqa_q[l], g_gqa_k[l], tabs, geom)
        rows = RL if last else RL + RC
        oa = _attention(qm, km, vm, mla_scale, geom, "attn_mla", rows)
        ob = _attention(qg, kg, vg, gqa_scale, geom, "attn_gqa", rows)
        gates, u, w, qgd, kgt, qk = _gdn_local(P, w_conv[l], a_log[l], dt_bias[l], geom)
        o_f, o_b = _gdn_scan(u, w, qgd, kgt, qk, gates, geom)
        oc = _gdn_out(o_f, o_b, P, g_gdn_out[l], rows)
        if not last:
            oa = _attention_ctx(qm, km, vm, mla_scale, geom, "attn_mla_ctx", oa)
            ob = _attention_ctx(qg, kg, vg, gqa_scale, geom, "attn_gqa_ctx", ob)
        x1 = _merge(src, mod[l], g_norm1[l], oa, ob, oc, wg, w_up_a[l].astype(BF16), w_up_b[l].astype(BF16),
                    w_up_c[l].astype(BF16), w_out[l].astype(BF16), geom, rows)
        xs = _mlp(x1, mod[l], g_norm2[l], w_ff1, w_ff2, l, g_final, geom, last)
        src = (xs, xs, RL)
    return xs.reshape(B, T, D)
```

```python
import functools
import math

import numpy as np
import jax
import jax.numpy as jnp
from jax import lax
from jax.experimental import pallas as pl
from jax.experimental.pallas import tpu as pltpu

F32 = jnp.float32
BF16 = jnp.bfloat16

GRID_W = 64
EPS = 1e-6
ROPE_THETA = 10000.0
MLA_HEADS, MLA_Q_RANK, MLA_KV_RANK, MLA_NOPE, MLA_ROPE, MLA_V = 4, 512, 512, 128, 64, 128
GQA_HEADS, GQA_KV_HEADS, GQA_HD = 8, 2, 128
GDN_HEADS, GDN_DK, GDN_DV, GDN_CONV, GDN_CHUNK = 4, 128, 128, 5, 64
N_MOD = 6
LANE = 128
GDN_BLOCK = 256
HALO_ROWS = 16
V7X_VMEM_BYTES = 64 * 1024 * 1024
VMEM_BIG = V7X_VMEM_BYTES * 7 // 8

TILE = dict(
    adaln_cols=2048,
    inproj_rows=1024, inproj_cols=1024,
    prep_rows=512,
    attn_rows=2048, attn_keys=256,
    gdn_out_rows=512,
    merge_rows=512, merge_cols=512,
    mlp_rows=1024, mlp_cols=512,
    relayout_k=256,
)

OFF_GQ, OFF_CQ, OFF_CKV, OFF_GK, OFF_GV, OFF_OG, OFF_QKV, OFF_MISC = 0, 1024, 1536, 2048, 2304, 2560, 3072, 4608
N_SMALL = 5120
MISC_BETA, MISC_DEC = 64, 72


def _mm(a, b):
    return jnp.dot(a, b, preferred_element_type=F32)


def _mm_nt(a, b):
    return lax.dot_general(a, b, (((1,), (1,)), ((), ())), preferred_element_type=F32)


def _silu(x):
    return x / (1.0 + jnp.exp(-x))


def _sigmoid(x):
    return 1.0 / (1.0 + jnp.exp(-x))


def _rms(x, g):
    return x * lax.rsqrt(jnp.mean(x * x, axis=-1, keepdims=True) + EPS) * g


def _norm_mod(x, g, shift, scale):
    return _rms(x, g) * (1.0 + scale) + shift


def _params(sem, vmem=None):
    kw = dict(dimension_semantics=sem)
    if vmem is not None:
        kw["vmem_limit_bytes"] = vmem
    return pltpu.CompilerParams(**kw)


def _pick_block(pref, *sizes):
    b = pref
    while any(s % b for s in sizes):
        b //= 2
    return b


def _mod_kernel(s_ref, w_ref, b_ref, o_ref):
    s = _silu(s_ref[...])
    o_ref[0] = _mm(s.astype(BF16), w_ref[0].astype(BF16)) + b_ref[0]


def _modulation(c, c_ctx, w_mod, b_mod):
    L, D, N = w_mod.shape
    B = c.shape[0]
    rows = jnp.zeros((8, D), F32).at[:B].set(c).at[B].set(c_ctx)
    bn = TILE["adaln_cols"]
    out = pl.pallas_call(
        _mod_kernel,
        grid=(L, N // bn),
        in_specs=[pl.BlockSpec((8, D), lambda l, j: (0, 0)),
                  pl.BlockSpec((1, D, bn), lambda l, j: (l, 0, j)),
                  pl.BlockSpec((1, 1, bn), lambda l, j: (l, 0, j))],
        out_specs=pl.BlockSpec((1, 8, bn), lambda l, j: (l, 0, j)),
        out_shape=jax.ShapeDtypeStruct((L, 8, N), F32),
        compiler_params=_params(("parallel", "parallel"), VMEM_BIG),
        name="adaln_mod",
    )(rows, w_mod, b_mod.reshape(L, 1, N))
    return out.reshape(L, 8, N_MOD, D)


NORM_ROWS = 64


def _norm_mod_rows(x_ref, g, shift, scale, hx_ref):
    def body(r, carry):
        rs = pl.ds(pl.multiple_of(r * NORM_ROWS, NORM_ROWS), NORM_ROWS)
        hx_ref[rs, :] = _norm_mod(x_ref[rs, :], g, shift, scale).astype(BF16)
        return carry

    lax.fori_loop(0, x_ref.shape[0] // NORM_ROWS, body, 0)


def _row_source_specs(src, bm, D, n_latent_rows):
    nlat = n_latent_rows // bm
    off_b = src[2] // bm
    spec_a = pl.BlockSpec((bm, D), lambda i, j: (jnp.minimum(i, nlat - 1), 0))
    spec_b = pl.BlockSpec((bm, D), lambda i, j: (off_b + jnp.maximum(i - nlat, 0), 0), pipeline_mode=pl.Buffered(1))
    return nlat, spec_a, spec_b


def _for_row_source(nlat, xa_ref, xb_ref, fn):
    i = pl.program_id(0)

    @pl.when(i < nlat)
    def _():
        fn(xa_ref)

    @pl.when(i >= nlat)
    def _():
        fn(xb_ref)


def _inproj_kernel(xa_ref, xb_ref, mod_ref, g_ref, w_ref, o_ref, hx_ref, *, nlat):
    @pl.when(pl.program_id(1) == 0)
    def _():
        m = mod_ref[0]
        _for_row_source(nlat, xa_ref, xb_ref,
                        lambda x_ref: _norm_mod_rows(x_ref, g_ref[...], m[0:1], m[1:2], hx_ref))

    o_ref[...] = _mm_nt(hx_ref[...], w_ref[...]).astype(BF16)


def _in_proj(src, mod_l, g1, wa, geom):
    R, D = geom["R"], src[0].shape[1]
    N = wa.shape[0]
    bm = _pick_block(TILE["inproj_rows"], geom["T"], geom["RC"])
    bn = TILE["inproj_cols"]
    sel = geom["mod_sel"](bm)
    nlat, spec_a, spec_b = _row_source_specs(src, bm, D, geom["RL"])
    return pl.pallas_call(
        functools.partial(_inproj_kernel, nlat=nlat),
        grid=(R // bm, N // bn),
        in_specs=[spec_a, spec_b,
                  pl.BlockSpec((1, N_MOD, D), lambda i, j: (sel(i), 0, 0)),
                  pl.BlockSpec((1, D), lambda i, j: (0, 0)),
                  pl.BlockSpec((bn, D), lambda i, j: (j, 0))],
        out_specs=pl.BlockSpec((bm, bn), lambda i, j: (i, j)),
        out_shape=jax.ShapeDtypeStruct((R, N), BF16),
        scratch_shapes=[pltpu.VMEM((bm, D), BF16)],
        compiler_params=_params(("parallel", "arbitrary"), VMEM_BIG),
        name="in_proj",
    )(src[0], src[1], mod_l, g1.reshape(1, D), wa)


def _prep_kernel(gq_ref, cq_ref, ckv_ref, gk_ref, gv_ref, misc_ref, gmq_ref, gmkv_ref, wqb_ref, wkvb_ref,
                 ggq_ref, ggk_ref, cosg_ref, sing_ref, cosm_ref, sinm_ref,
                 qm_ref, km_ref, vm_ref, qg_ref, kg_ref, vg_ref):
    lane = lax.broadcasted_iota(jnp.int32, (1, LANE), 1)
    cosm, sinm = cosm_ref[...], sinm_ref[...]
    cosg, sing = cosg_ref[...], sing_ref[...]
    first_m = (lane & 31) < 16
    first_g = (lane & 63) < 32
    ones_col = jnp.where(lane == 0, 1.0, 0.0).astype(BF16) + jnp.zeros((cq_ref.shape[0], LANE), BF16)

    def rope_m(x):
        partner = jnp.where(first_m, pltpu.roll(x, LANE - 16, 1), pltpu.roll(x, 16, 1))
        return x * cosm + partner * sinm

    def rope_g(x):
        partner = jnp.where(first_g, pltpu.roll(x, LANE - 32, 1), pltpu.roll(x, 32, 1))
        return x * cosg + partner * sing

    qa = _mm(_rms(cq_ref[...].astype(F32), gmq_ref[...]).astype(BF16), wqb_ref[...])
    kva = _mm(_rms(ckv_ref[...].astype(F32), gmkv_ref[...]).astype(BF16), wkvb_ref[...])
    kpe = rope_m(jnp.where(lane < MLA_ROPE, misc_ref[...].astype(F32), 0.0)).astype(BF16)
    nh = MLA_HEADS * LANE
    for h in range(MLA_HEADS):
        hs = slice(h * LANE, (h + 1) * LANE)
        ps = slice(nh + h * LANE, nh + (h + 1) * LANE)
        qm_ref[h, :, 0:LANE] = qa[:, hs].astype(BF16)
        qm_ref[h, :, LANE:2 * LANE] = rope_m(qa[:, ps]).astype(BF16)
        km_ref[h, :, 0:LANE] = kva[:, hs].astype(BF16)
        km_ref[h, :, LANE:2 * LANE] = kpe
        vm_ref[h, :, 0:LANE] = kva[:, ps].astype(BF16)
        vm_ref[h, :, LANE:2 * LANE] = ones_col

    ggq, ggk = ggq_ref[...], ggk_ref[...]
    for h in range(GQA_HEADS):
        hs = slice(h * LANE, (h + 1) * LANE)
        qg_ref[h] = rope_g(_rms(gq_ref[:, hs].astype(F32), ggq)).astype(BF16)
    for h in range(GQA_KV_HEADS):
        hs = slice(h * LANE, (h + 1) * LANE)
        kg_ref[h] = rope_g(_rms(gk_ref[:, hs].astype(F32), ggk)).astype(BF16)
        vg_ref[h, :, 0:LANE] = gv_ref[:, hs]
        vg_ref[h, :, LANE:2 * LANE] = ones_col


def _attn_prep(P, gmq, gmkv, wqb, wkvb, ggq, ggk, tabs, geom):
    R = P.shape[0]
    bm = _pick_block(TILE["prep_rows"], geom["T"], geom["RC"])
    nlat = geom["RL"] // bm
    per = geom["T"] // bm
    tsel = lambda i: jnp.where(i < nlat, i % per, per)
    col = lambda w, off: pl.BlockSpec((bm, w), lambda i: (i, off // w))
    full = lambda a: pl.BlockSpec(a.shape, lambda i: (0,) * a.ndim)
    tab = pl.BlockSpec((bm, LANE), lambda i: (tsel(i), 0))
    outs = [((MLA_HEADS, R, 2 * LANE), 2 * LANE), ((MLA_HEADS, R, 2 * LANE), 2 * LANE),
            ((MLA_HEADS, R, 2 * LANE), 2 * LANE), ((GQA_HEADS, R, LANE), LANE), ((GQA_KV_HEADS, R, LANE), LANE),
            ((GQA_KV_HEADS, R, 2 * LANE), 2 * LANE)]
    gmq, gmkv, ggq, ggk = (a.reshape(1, -1) for a in (gmq, gmkv, ggq, ggk))
    return pl.pallas_call(
        _prep_kernel,
        grid=(R // bm,),
        in_specs=[col(1024, OFF_GQ), col(512, OFF_CQ), col(512, OFF_CKV), col(256, OFF_GK), col(256, OFF_GV),
                  col(LANE, OFF_MISC), full(gmq), full(gmkv), full(wqb), full(wkvb), full(ggq), full(ggk),
                  tab, tab, tab, tab],
        out_specs=[pl.BlockSpec((s[0], bm, w), lambda i: (0, i, 0)) for s, w in outs],
        out_shape=[jax.ShapeDtypeStruct(s, BF16) for s, _ in outs],
        compiler_params=_params(("parallel",), VMEM_BIG),
        name="attn_prep",
    )(P, P, P, P, P, P, gmq, gmkv, wqb, wkvb, ggq, ggk, *tabs)


def _softmax_pv(q, kv_chunks, s_ref, o_ref, exp2_scale):
    sub = min(q.shape)
    for r0 in range(0, q.shape[0], sub):
        rows = slice(r0, r0 + sub)
        qr = q[rows]
        m_acc = None
        off = 0
        for k_ref, _, st, n in kv_chunks:
            s = _mm_nt(qr, k_ref[0, st:st + n, :])
            s_ref[rows, off:off + n] = s
            for t in range(n // LANE):
                tile = s[:, t * LANE:(t + 1) * LANE]
                m_acc = tile if m_acc is None else jnp.maximum(m_acc, tile)
            off += n
        m = jnp.max(m_acc, axis=-1, keepdims=True)
        acc = None
        off = 0
        for _, v_ref, st, n in kv_chunks:
            p = jnp.exp2((s_ref[rows, off:off + n] - m) * exp2_scale).astype(BF16)
            pv = _mm(p, v_ref[0, st:st + n, :])
            acc = pv if acc is None else acc + pv
            off += n
        o_ref[rows, :] = (acc[:, :LANE] / acc[:, LANE:LANE + 1]).astype(BF16)


def _attn_kernel(q_ref, kx_ref, kc_ref, vx_ref, vc_ref, o_ref, s_ref, *, exp2_scale, kchunk):
    chunks = [(kx_ref, vx_ref, i * kchunk, kchunk) for i in range(kx_ref.shape[1] // kchunk)]
    chunks.append((kc_ref, vc_ref, 0, kc_ref.shape[1]))
    _softmax_pv(q_ref[0], chunks, s_ref, o_ref, exp2_scale)


def _attn_ctx_kernel(q_ref, kc_ref, vc_ref, dst_ref, o_ref, s_ref, *, exp2_scale):
    del dst_ref
    _softmax_pv(q_ref[0], [(kc_ref, vc_ref, 0, kc_ref.shape[1])], s_ref, o_ref, exp2_scale)


def _attention(q, k, v, scale, geom, name, rows_out):
    H, _, dk = q.shape
    Hk, _, dve = v.shape
    grp = H // Hk
    B, T, C, RL = geom["B"], geom["T"], geom["C"], geom["RL"]
    bq = _pick_block(TILE["attn_rows"], T)
    nq = T // bq
    cb = RL // C
    return pl.pallas_call(
        functools.partial(_attn_kernel, exp2_scale=scale * math.log2(math.e),
                          kchunk=TILE["attn_keys"]),
        grid=(B, H, nq),
        in_specs=[pl.BlockSpec((1, bq, dk), lambda b, h, j: (h, b * nq + j, 0)),
                  pl.BlockSpec((1, T, dk), lambda b, h, j: (h // grp, b, 0)),
                  pl.BlockSpec((1, C, dk), lambda b, h, j: (h // grp, cb + b, 0)),
                  pl.BlockSpec((1, T, dve), lambda b, h, j: (h // grp, b, 0)),
                  pl.BlockSpec((1, C, dve), lambda b, h, j: (h // grp, cb + b, 0))],
        out_specs=pl.BlockSpec((bq, LANE), lambda b, h, j: (b * nq + j, h)),
        out_shape=jax.ShapeDtypeStruct((rows_out, H * LANE), BF16),
        scratch_shapes=[pltpu.VMEM((bq, T + C), F32)],
        compiler_params=_params(("parallel", "parallel", "parallel"), VMEM_BIG),
        name=name,
    )(q, k, k, v, v)


def _attention_ctx(q, k, v, scale, geom, name, o_latent):
    H, _, dk = q.shape
    Hk, _, dve = v.shape
    grp = H // Hk
    B, C, RL, RC = geom["B"], geom["C"], geom["RL"], geom["RC"]
    cb = RL // C
    return pl.pallas_call(
        functools.partial(_attn_ctx_kernel, exp2_scale=scale * math.log2(math.e)),
        grid=(B, H),
        in_specs=[pl.BlockSpec((1, C, dk), lambda b, h: (h, cb + b, 0)),
                  pl.BlockSpec((1, C, dk), lambda b, h: (h // grp, cb + b, 0)),
                  pl.BlockSpec((1, C, dve), lambda b, h: (h // grp, cb + b, 0)),
                  pl.BlockSpec(memory_space=pl.ANY)],
        out_specs=pl.BlockSpec((C, LANE), lambda b, h: (cb + b, h)),
        out_shape=jax.ShapeDtypeStruct(o_latent.shape, BF16),
        input_output_aliases={3: 0},
        scratch_shapes=[pltpu.VMEM((C, C), F32)],
        compiler_params=_params(("parallel", "parallel")),
        name=name,
    )(q, k, v, o_latent)


def _gdn_conv_gates(cur_ref, prev_ref, next_ref, misc_ref, wc_ref, gvec_ref, ext_ref, *, nlat, per_seq, nblk):
    r = jnp.minimum(pl.program_id(0), nblk - 1)
    pos = r % per_seq
    is_ctx = r >= nlat
    pf = jnp.where(jnp.logical_or(is_ctx, pos == 0), 0.0, 1.0).astype(F32)
    nf = jnp.where(jnp.logical_or(is_ctx, pos == per_seq - 1), 0.0, 1.0).astype(F32)
    nb = GDN_BLOCK
    hal = HALO_ROWS
    ext_ref[0:hal, :] = prev_ref[...].astype(F32) * pf
    ext_ref[hal:hal + nb, :] = cur_ref[...].astype(F32)
    ext_ref[hal + nb:2 * hal + nb, :] = next_ref[...].astype(F32) * nf
    pad = GDN_CONV // 2
    n_qk = 2 * GDN_HEADS
    heads = []
    for c in range(3 * GDN_HEADS):
        cs = slice(c * LANE, (c + 1) * LANE)
        acc = None
        for j in range(GDN_CONV):
            t = ext_ref[hal - pad + j:hal - pad + j + nb, cs] * wc_ref[j:j + 1, cs]
            acc = t if acc is None else acc + t
        y = _silu(acc)
        if c < n_qk:
            y = y * lax.rsqrt(jnp.sum(y * y, axis=-1, keepdims=True) + EPS)
        heads.append(y)

    lane = lax.broadcasted_iota(jnp.int32, (1, LANE), 1)
    raw = misc_ref[...].astype(F32)
    beta = _sigmoid(raw)
    z = raw + gvec_ref[1:2, :]
    softplus = jnp.maximum(z, 0.0) + jnp.log1p(jnp.exp(-jnp.abs(z)))
    g = -jnp.exp(gvec_ref[0:1, :]) * softplus
    is_beta = jnp.logical_and(lane >= MISC_BETA, lane < MISC_DEC)
    is_g = jnp.logical_and(lane >= MISC_DEC, lane < MISC_DEC + 2 * GDN_HEADS)
    return heads, jnp.where(is_beta, beta, jnp.where(is_g, g, 0.0))


def _split3(x):
    h = x.astype(BF16)
    r = x - h.astype(F32)
    m = r.astype(BF16)
    l = (r - m.astype(F32)).astype(BF16)
    return h, m, l


def _gdn_chunk_local(heads, gt, u_ref, w_ref, qg_ref, kgt_ref, qk_ref):
    nb = GDN_BLOCK
    nh = GDN_HEADS
    ri = lax.broadcasted_iota(jnp.int32, (nb, nb), 0)
    ci = lax.broadcasted_iota(jnp.int32, (nb, nb), 1)
    same = (ri >> 6) == (ci >> 6)
    low = jnp.logical_and(same, ri >= ci)
    upp = jnp.logical_and(same, ri <= ci)
    slow = jnp.logical_and(same, ri > ci)
    supp = jnp.logical_and(same, ri < ci)
    ltri = jnp.where(low, 1.0, 0.0).astype(BF16)
    utri = jnp.where(upp, 1.0, 0.0).astype(BF16)
    eye = jnp.where(ri == ci, 1.0, 0.0)
    pair = (ri >> 1) == (ci >> 1)
    offs = [jnp.logical_and((ri >> (lv + 1)) == (ci >> (lv + 1)), (ri >> lv) != (ci >> lv))
            for lv in range(1, int(math.log2(GDN_CHUNK)))]

    g3 = _split3(gt)
    g3t = _split3(gt.T)
    cum_c = (sum(_mm(ltri, p) for p in reversed(g3)), sum(_mm(utri, p) for p in reversed(g3)))
    cum_r = (sum(_mm(p, utri) for p in reversed(g3t)), sum(_mm(p, ltri) for p in reversed(g3t)))
    tot_c = cum_c[0] + cum_c[1] - gt

    scale = GDN_DK ** -0.5
    insts = [(d, h) for d in range(2) for h in range(nh)]
    a_all, tinv_all, rhs_all = [], [], []

    def prep(d, h):
        mask, smask = (low, slow) if d == 0 else (upp, supp)
        ib = MISC_BETA + d * nh + h
        ig = MISC_DEC + d * nh + h
        beta = gt[:, ib:ib + 1]
        gc = cum_c[d][:, ig:ig + 1]
        gr = cum_r[d][ig:ig + 1, :]
        gl = tot_c[:, ig:ig + 1]
        q, k, v = heads[h], heads[nh + h], heads[2 * nh + h]
        decay = jnp.where(mask, jnp.exp(jnp.where(mask, gc - gr, 0.0)), 0.0)
        kb = k * beta
        k16 = k.astype(BF16)
        a = jnp.where(smask, _mm_nt(kb.astype(BF16), k16) * decay, 0.0)
        a_all.append(a.astype(BF16))
        tinv_all.append(eye - jnp.where(pair, a, 0.0))
        rhs_all.append(jnp.concatenate([v * beta, kb * jnp.exp(gc)], axis=1).astype(BF16))
        hs = slice(h * LANE, (h + 1) * LANE)
        qs = q * scale
        qk_ref[d, :, h * nb:(h + 1) * nb] = (_mm_nt(qs.astype(BF16), k16) * decay).astype(BF16)
        qg_ref[d, :, hs] = (qs * jnp.exp(gc)).astype(BF16)
        kgt_ref[d, hs, :] = (k * jnp.exp(gl - gc)).T.astype(BF16)

    zero16 = jnp.zeros((nb, nb), BF16)

    def level(i, off):
        t16 = tinv_all[i].astype(BF16)
        ta = _mm(t16, jnp.where(off, a_all[i], zero16))
        tinv_all[i] = tinv_all[i] - _mm(ta.astype(BF16), t16)

    def finish(i):
        d, h = insts[i]
        x = _mm(tinv_all[i].astype(BF16), rhs_all[i])
        hs = slice(h * LANE, (h + 1) * LANE)
        u_ref[d, :, hs] = x[:, :LANE]
        w_ref[d, :, hs] = x[:, LANE:].astype(BF16)

    for d, h in insts:
        prep(d, h)
    for off in offs:
        for i in range(len(insts)):
            level(i, off)
    for i in range(len(insts)):
        finish(i)


def _gdn_local_kernel(cur_ref, prev_ref, next_ref, misc_ref, wc_ref, gvec_ref,
                      gates_ref, u_ref, w_ref, qg_ref, kgt_ref, qk_ref, ext_ref, *, nlat, per_seq, nblk):
    heads, gt = _gdn_conv_gates(cur_ref, prev_ref, next_ref, misc_ref, wc_ref, gvec_ref, ext_ref,
                                nlat=nlat, per_seq=per_seq, nblk=nblk)
    gates_ref[...] = gt
    _gdn_chunk_local(heads, gt, u_ref, w_ref, qg_ref, kgt_ref, qk_ref)


def _gdn_local(P, w_conv, a_log, dt_bias, geom):
    R = P.shape[0]
    nb = GDN_BLOCK
    HW = GDN_HEADS * LANE
    W = 3 * HW
    hal = HALO_ROWS
    sub = nb // hal
    gvec = jnp.zeros((2, LANE), F32)
    gvec = gvec.at[0, MISC_DEC:MISC_DEC + 2 * GDN_HEADS].set(a_log.reshape(-1))
    gvec = gvec.at[1, MISC_DEC:MISC_DEC + 2 * GDN_HEADS].set(dt_bias.reshape(-1))
    qc = OFF_QKV // W
    nblk = R // nb
    cb = ob = lambda t: t
    return pl.pallas_call(
        functools.partial(_gdn_local_kernel, nlat=geom["RL"] // nb, per_seq=geom["T"] // nb, nblk=nblk),
        grid=(nblk,),
        in_specs=[pl.BlockSpec((nb, W), lambda t: (cb(t), qc)),
                  pl.BlockSpec((hal, W), lambda t: (jnp.maximum(cb(t) * sub - 1, 0), qc)),
                  pl.BlockSpec((hal, W), lambda t: (jnp.minimum((cb(t) + 1) * sub, R // hal - 1), qc)),
                  pl.BlockSpec((nb, LANE), lambda t: (cb(t), OFF_MISC // LANE)),
                  pl.BlockSpec((GDN_CONV, W), lambda t: (0, 0)),
                  pl.BlockSpec((2, LANE), lambda t: (0, 0))],
        out_specs=[pl.BlockSpec((nb, LANE), lambda t: (ob(t), 0)),
                   pl.BlockSpec((2, nb, HW), lambda t: (0, ob(t), 0)),
                   pl.BlockSpec((2, nb, HW), lambda t: (0, ob(t), 0)),
                   pl.BlockSpec((2, nb, HW), lambda t: (0, ob(t), 0)),
                   pl.BlockSpec((2, HW, nb), lambda t: (0, 0, ob(t))),
                   pl.BlockSpec((2, nb, GDN_HEADS * nb), lambda t: (0, ob(t), 0))],
        out_shape=[jax.ShapeDtypeStruct((R, LANE), F32),
                   jax.ShapeDtypeStruct((2, R, HW), F32),
                   jax.ShapeDtypeStruct((2, R, HW), BF16),
                   jax.ShapeDtypeStruct((2, R, HW), BF16),
                   jax.ShapeDtypeStruct((2, HW, R), BF16),
                   jax.ShapeDtypeStruct((2, R, GDN_HEADS * nb), BF16)],
        scratch_shapes=[pltpu.VMEM((nb + 2 * hal, W), F32)],
        compiler_params=_params(("parallel",), VMEM_BIG),
        name="gdn_local",
    )(P, P, P, P, w_conv, gvec)


def _gdn_c_kernel(*refs):
    nb = GDN_BLOCK
    nh = GDN_HEADS
    ins, (of_ref, ob_ref, s_ref, vn_ref) = (refs[0:6], refs[6:12]), refs[12:]
    o_refs = (of_ref, ob_ref)

    @pl.when(pl.program_id(1) == 0)
    def _():
        s_ref[...] = jnp.zeros_like(s_ref)
        vn_ref[...] = jnp.zeros_like(vn_ref)

    rowi = lax.broadcasted_iota(jnp.int32, (nb, 1), 0)
    nchunk = nb // GDN_CHUNK
    for step in range(nchunk):
        for d in range(2):
            u_ref, w_ref, qg_ref, kgt_ref, qk_ref, gates_ref = ins[d]
            c = step if d == 0 else nchunk - 1 - step
            rs = slice(c * GDN_CHUNK, (c + 1) * GDN_CHUNK)
            decay_all = jnp.exp(jnp.sum(gates_ref[rs, :], axis=0, keepdims=True))
            cmask = jnp.logical_and(rowi >= c * GDN_CHUNK, rowi < (c + 1) * GDN_CHUNK)
            for h in range(nh):
                hs = slice(h * LANE, (h + 1) * LANE)
                ig = MISC_DEC + d * nh + h
                s = s_ref[d, h]
                s16 = s.astype(BF16)
                v_new = u_ref[0, rs, hs] - _mm(w_ref[0, rs, hs], s16)
                vn_ref[d, h, rs, :] = v_new.astype(BF16)
                vn = vn_ref[d, h]
                o_refs[d][rs, hs] = _mm(qg_ref[0, rs, hs], s16) + _mm(qk_ref[0, rs, h * nb:(h + 1) * nb], vn)
                v_cur = jnp.where(cmask, vn, jnp.zeros_like(vn))
                s_ref[d, h] = s * decay_all[:, ig:ig + 1] + _mm(kgt_ref[0, hs, :], v_cur)


def _gdn_scan(u, w, qg, kgt, qk, gates, geom):
    nb = GDN_BLOCK
    HW = GDN_HEADS * LANE
    B, RL, R = geom["B"], geom["RL"], geom["R"]
    per = geom["T"] // nb
    nlat = RL // nb

    def blk(d):
        return lambda b, j: jnp.where(j == 0, nlat + b, b * per + (j - 1 if d == 0 else per - j))

    def dir_specs(d):
        bk = blk(d)
        tok = lambda w_: pl.BlockSpec((1, nb, w_), lambda b, j: (d, bk(b, j), 0))
        return [tok(HW), tok(HW), tok(HW),
                pl.BlockSpec((1, HW, nb), lambda b, j: (d, 0, bk(b, j))),
                tok(GDN_HEADS * nb),
                pl.BlockSpec((nb, LANE), lambda b, j: (bk(b, j), 0))]

    args = (u, w, qg, kgt, qk, gates)
    return pl.pallas_call(
        _gdn_c_kernel,
        grid=(B, per + 1),
        in_specs=dir_specs(0) + dir_specs(1),
        out_specs=[pl.BlockSpec((nb, HW), lambda b, j: (blk(0)(b, j), 0)),
                   pl.BlockSpec((nb, HW), lambda b, j: (blk(1)(b, j), 0))],
        out_shape=[jax.ShapeDtypeStruct((R, HW), F32), jax.ShapeDtypeStruct((R, HW), F32)],
        scratch_shapes=[pltpu.VMEM((2, GDN_HEADS, GDN_DK, GDN_DV), F32),
                        pltpu.VMEM((2, GDN_HEADS, nb, GDN_DV), BF16)],
        compiler_params=_params(("parallel", "arbitrary")),
        name="gdn_scan",
    )(*args, *args)


def _gdn_out_kernel(of_ref, ob_ref, og_ref, g_ref, o_ref):
    o = of_ref[...] + ob_ref[...]
    gate = _silu(og_ref[...].astype(F32))
    g = g_ref[...]
    for h in range(GDN_HEADS):
        hs = slice(h * LANE, (h + 1) * LANE)
        o_ref[:, hs] = (_rms(o[:, hs], g) * gate[:, hs]).astype(BF16)


def _gdn_out(o_f, o_b, P, g_out, rows):
    HW = GDN_HEADS * LANE
    bm = _pick_block(TILE["gdn_out_rows"], rows)
    return pl.pallas_call(
        _gdn_out_kernel,
        grid=(rows // bm,),
        in_specs=[pl.BlockSpec((bm, HW), lambda i: (i, 0)),
                  pl.BlockSpec((bm, HW), lambda i: (i, 0)),
                  pl.BlockSpec((bm, HW), lambda i: (i, OFF_OG // HW)),
                  pl.BlockSpec((1, LANE), lambda i: (0, 0))],
        out_specs=pl.BlockSpec((bm, HW), lambda i: (i, 0)),
        out_shape=jax.ShapeDtypeStruct((rows, HW), BF16),
        compiler_params=_params(("parallel",)),
        name="gdn_out",
    )(o_f, o_b, P, g_out.reshape(1, LANE))


def _residual_rows(x_ref, gate, o_ref, post=None):
    def body(r, carry):
        rs = pl.ds(pl.multiple_of(r * NORM_ROWS, NORM_ROWS), NORM_ROWS)
        v = x_ref[rs, :] + gate * o_ref[rs, :]
        o_ref[rs, :] = v if post is None else post(v)
        return carry

    lax.fori_loop(0, x_ref.shape[0] // NORM_ROWS, body, 0)


def _merge_kernel(xa_ref, xb_ref, mod_ref, g_ref, oa_ref, ob_ref, oc_ref, wga_ref, wgb_ref, wgc_ref, wua_ref,
                  wub_ref, wuc_ref, wo_ref, o_ref, hx_ref, *, nlat):
    j = pl.program_id(1)
    m = mod_ref[0]

    @pl.when(j == 0)
    def _():
        _for_row_source(nlat, xa_ref, xb_ref,
                        lambda x_ref: _norm_mod_rows(x_ref, g_ref[...], m[0:1], m[1:2], hx_ref))
        o_ref[...] = jnp.zeros_like(o_ref)

    hx = hx_ref[...]
    oa, ob, oc = oa_ref[...], ob_ref[...], oc_ref[...]
    hc = wo_ref.shape[0]
    ga, gb, gc = _mm_nt(hx, wga_ref[...]), _mm_nt(hx, wgb_ref[...]), _mm_nt(hx, wgc_ref[...])
    ua, ub, uc = _mm(oa, wua_ref[...]), _mm(ob, wub_ref[...]), _mm(oc, wuc_ref[...])
    acc = None
    for cs in (slice(0, hc // 2), slice(hc // 2, hc)):
        y = _sigmoid(ga[:, cs]) * ua[:, cs] + _sigmoid(gb[:, cs]) * ub[:, cs] + _sigmoid(gc[:, cs]) * uc[:, cs]
        t = _mm(y.astype(BF16), wo_ref[cs, :])
        acc = t if acc is None else acc + t
    o_ref[...] += acc

    @pl.when(j == pl.num_programs(1) - 1)
    def _():
        _for_row_source(nlat, xa_ref, xb_ref, lambda x_ref: _residual_rows(x_ref, m[2:3], o_ref))


def _merge(src, mod_l, g1, oa, ob, oc, wg, wua, wub, wuc, wo, geom, rows):
    D = src[0].shape[1]
    bm = _pick_block(TILE["merge_rows"], geom["T"], geom["RC"])
    hc = TILE["merge_cols"]
    nj = D // hc
    sel = geom["mod_sel"](bm)
    rowb = lambda w_: pl.BlockSpec((bm, w_), lambda i, j: (i, 0))
    gate_w = lambda br: pl.BlockSpec((hc, D), lambda i, j: (br * nj + j, 0))
    nlat, spec_a, spec_b = _row_source_specs(src, bm, D, geom["RL"])
    return pl.pallas_call(
        functools.partial(_merge_kernel, nlat=nlat),
        grid=(rows // bm, nj),
        in_specs=[spec_a, spec_b,
                  pl.BlockSpec((1, N_MOD, D), lambda i, j: (sel(i), 0, 0)),
                  pl.BlockSpec((1, D), lambda i, j: (0, 0)),
                  rowb(oa.shape[1]), rowb(ob.shape[1]), rowb(oc.shape[1]),
                  gate_w(0), gate_w(1), gate_w(2),
                  pl.BlockSpec((wua.shape[0], hc), lambda i, j: (0, j)),
                  pl.BlockSpec((wub.shape[0], hc), lambda i, j: (0, j)),
                  pl.BlockSpec((wuc.shape[0], hc), lambda i, j: (0, j)),
                  pl.BlockSpec((hc, D), lambda i, j: (j, 0))],
        out_specs=rowb(D),
        out_shape=jax.ShapeDtypeStruct((rows, D), F32),
        scratch_shapes=[pltpu.VMEM((bm, D), BF16)],
        compiler_params=_params(("parallel", "arbitrary"), VMEM_BIG),
        name="merge",
    )(src[0], src[1], mod_l, g1.reshape(1, D), oa, ob, oc, wg, wg, wg, wua, wub, wuc, wo)


def _mlp_kernel(x_ref, mod_ref, g_ref, w1_ref, w2_ref, gf_ref, o_ref, hx_ref, *, final):
    j = pl.program_id(1)
    m = mod_ref[0]

    @pl.when(j == 0)
    def _():
        _norm_mod_rows(x_ref, g_ref[...], m[3:4], m[4:5], hx_ref)
        o_ref[...] = jnp.zeros_like(o_ref)

    hx = hx_ref[...]
    fc = w2_ref.shape[1]
    h1 = _mm(hx, w1_ref[0].astype(BF16))
    acc = None
    for cs in (slice(0, fc // 2), slice(fc // 2, fc)):
        h = jnp.maximum(h1[:, cs], 0.0)
        t = _mm((h * h).astype(BF16), w2_ref[0, cs, :].astype(BF16))
        acc = t if acc is None else acc + t
    o_ref[...] += acc

    @pl.when(j == pl.num_programs(1) - 1)
    def _():
        gf = gf_ref[...]
        _residual_rows(x_ref, m[5:6], o_ref, (lambda v: _rms(v, gf)) if final else None)


def _mlp(x1, mod_l, g2, w1, w2, l, g_final, geom, final):
    rows, D = x1.shape
    F = w1.shape[2]
    bm = _pick_block(TILE["mlp_rows"], geom["T"], geom["RC"])
    fc = TILE["mlp_cols"]
    sel = geom["mod_sel"](bm)
    return pl.pallas_call(
        functools.partial(_mlp_kernel, final=final),
        grid=(rows // bm, F // fc),
        in_specs=[pl.BlockSpec((bm, D), lambda i, j: (i, 0), pipeline_mode=pl.Buffered(1)),
                  pl.BlockSpec((1, N_MOD, D), lambda i, j: (sel(i), 0, 0)),
                  pl.BlockSpec((1, D), lambda i, j: (0, 0)),
                  pl.BlockSpec((1, D, fc), lambda i, j: (l, 0, j)),
                  pl.BlockSpec((1, fc, D), lambda i, j: (l, j, 0)),
                  pl.BlockSpec((1, D), lambda i, j: (0, 0))],
        out_specs=pl.BlockSpec((bm, D), lambda i, j: (i, 0)),
        out_shape=jax.ShapeDtypeStruct((rows, D), F32),
        scratch_shapes=[pltpu.VMEM((bm, D), BF16)],
        compiler_params=_params(("parallel", "arbitrary"), VMEM_BIG),
        name="mlp",
    )(x1, mod_l, g2.reshape(1, D), w1, w2, g_final.reshape(1, D))


def _rope_tables(T, bm):
    t = np.arange(T)
    row, col = t // GRID_W, t % GRID_W

    def table(n_rot):
        n = n_rot // 2
        half = n // 2
        inv = ROPE_THETA ** (-np.arange(half, dtype=np.float64) / half)
        cos = np.zeros((T + bm, LANE))
        sin = np.zeros((T + bm, LANE))
        for s, pos in enumerate((row, col)):
            ang = pos[:, None].astype(np.float32).astype(np.float64) * inv.astype(np.float32)[None, :]
            ang = ang.astype(np.float32).astype(np.float64)
            c, sn = np.cos(ang), np.sin(ang)
            cos[:T, s * n:s * n + half] = c
            cos[:T, s * n + half:(s + 1) * n] = c
            sin[:T, s * n:s * n + half] = -sn
            sin[:T, s * n + half:(s + 1) * n] = sn
        cos[T:, :] = 1.0
        return jnp.asarray(cos, F32), jnp.asarray(sin, F32)

    cg, sg = table(GQA_HD)
    cm, sm = table(MLA_ROPE)
    return cg, sg, cm, sm


def _relayout_kernel(w_ref, wa_ref, wg_ref, *, src_off):
    for name, dst, width in (("gq", OFF_GQ, 1024), ("cq", OFF_CQ, 512), ("ckv", OFF_CKV, 512), ("gk", OFF_GK, 256),
                             ("gv", OFF_GV, 256), ("og", OFF_OG, 512), ("qkv", OFF_QKV, 1536),
                             ("kpe", OFF_MISC, MLA_ROPE), ("beta", OFF_MISC + MISC_BETA, 4 * GDN_HEADS)):
        s = src_off[name]
        wa_ref[dst:dst + width, :] = w_ref[0, s:s + width, :].astype(BF16)
    assert src_off["dec"] == src_off["beta"] + 2 * GDN_HEADS
    used = OFF_MISC + MISC_DEC + 2 * GDN_HEADS
    wa_ref[used:, :] = jnp.zeros((N_SMALL - used, wa_ref.shape[1]), BF16)
    s = src_off["bg"]
    wg_ref[...] = w_ref[0, s:s + wg_ref.shape[0], :].astype(BF16)


def _relayout_w_in(w_in, l):
    w_t = jnp.swapaxes(w_in, 1, 2)
    L, N, D = w_t.shape
    splits = (("cq", MLA_Q_RANK), ("ckv", MLA_KV_RANK), ("kpe", MLA_ROPE), ("gq", GQA_HEADS * GQA_HD),
              ("gk", GQA_KV_HEADS * GQA_HD), ("gv", GQA_KV_HEADS * GQA_HD),
              ("qkv", GDN_HEADS * (2 * GDN_DK + GDN_DV)), ("beta", 2 * GDN_HEADS), ("dec", 2 * GDN_HEADS),
              ("og", GDN_HEADS * GDN_DV), ("bg", 3 * D))
    src_off, o = {}, 0
    for name, n in splits:
        src_off[name] = o
        o += n
    assert o == N
    kc = TILE["relayout_k"]
    return pl.pallas_call(
        functools.partial(_relayout_kernel, src_off=src_off),
        grid=(D // kc,),
        in_specs=[pl.BlockSpec((1, N, kc), lambda i: (l, 0, i))],
        out_specs=[pl.BlockSpec((N_SMALL, kc), lambda i: (0, i)),
                   pl.BlockSpec((3 * D, kc), lambda i: (0, i))],
        out_shape=[jax.ShapeDtypeStruct((N_SMALL, D), BF16), jax.ShapeDtypeStruct((3 * D, D), BF16)],
        compiler_params=_params(("parallel",), VMEM_BIG),
        name="w_in_relayout",
    )(w_t)


def _prep_layer_weights(w_qb, w_kvb):
    qb = w_qb.reshape(-1, MLA_HEADS, MLA_NOPE + MLA_ROPE)
    qpe = jnp.pad(qb[:, :, MLA_NOPE:], ((0, 0), (0, 0), (0, LANE - MLA_ROPE)))
    wqb = jnp.concatenate([qb[:, :, :MLA_NOPE].reshape(-1, MLA_HEADS * LANE),
                           qpe.reshape(-1, MLA_HEADS * LANE)], axis=1).astype(BF16)
    kvb = w_kvb.reshape(-1, MLA_HEADS, MLA_NOPE + MLA_V)
    wkvb = jnp.concatenate([kvb[:, :, :MLA_NOPE].reshape(-1, MLA_HEADS * LANE),
                            kvb[:, :, MLA_NOPE:].reshape(-1, MLA_HEADS * LANE)], axis=1).astype(BF16)
    return wqb, wkvb


def kernel(x, c, ctx, c_ctx, w_mod, b_mod, g_norm1, w_in, g_mla_q, w_mla_qb, g_mla_kv, w_mla_kvb, g_gqa_q, g_gqa_k,
           w_conv, a_log, dt_bias, g_gdn_out, w_up_a, w_up_b, w_up_c, w_out, g_norm2, w_ff1, w_ff2, g_final):
    B, T, D = x.shape
    C = ctx.shape[1]
    L = w_mod.shape[0]
    assert C == GDN_BLOCK and T % GDN_BLOCK == 0 and T % GRID_W == 0 and B < 8
    RL, RC = B * T, B * C
    geom = dict(B=B, T=T, C=C, RL=RL, RC=RC, R=RL + RC)
    geom["mod_sel"] = lambda bm: (lambda i: jnp.where(i < RL // bm, i // (T // bm), B))

    mod = _modulation(c, c_ctx, w_mod, b_mod)
    tabs = _rope_tables(T, _pick_block(TILE["prep_rows"], T, RC))
    src = (x.reshape(RL, D), ctx.reshape(RC, D), 0)
    mla_scale = (MLA_NOPE + MLA_ROPE) ** -0.5
    gqa_scale = GQA_HD ** -0.5

    for l in range(L):
        last = l == L - 1
        wa, wg = _relayout_w_in(w_in, l)
        wqb, wkvb = _prep_layer_weights(w_mla_qb[l], w_mla_kvb[l])
        P = _in_proj(src, mod[l], g_norm1[l], wa, geom)
        qm, km, vm, qg, kg, vg = _attn_prep(P, g_mla_q[l], g_mla_kv[l], wqb, wkvb, g_gqa_q[l], g_gqa_k[l], tabs, geom)
        rows = RL if last else RL + RC
        oa = _attention(qm, km, vm, mla_scale, geom, "attn_mla", rows)
        ob = _attention(qg, kg, vg, gqa_scale, geom, "attn_gqa", rows)
        gates, u, w, qgd, kgt, qk = _gdn_local(P, w_conv[l], a_log[l], dt_bias[l], geom)
        o_f, o_b = _gdn_scan(u, w, qgd, kgt, qk, gates, geom)
        oc = _gdn_out(o_f, o_b, P, g_gdn_out[l], rows)
        if not last:
            oa = _attention_ctx(qm, km, vm, mla_scale, geom, "attn_mla_ctx", oa)
            ob = _attention_ctx(qg, kg, vg, gqa_scale, geom, "attn_gqa_ctx", ob)
        x1 = _merge(src, mod[l], g_norm1[l], oa, ob, oc, wg, w_up_a[l].astype(BF16), w_up_b[l].astype(BF16),
                    w_up_c[l].astype(BF16), w_out[l].astype(BF16), geom, rows)
        xs = _mlp(x1, mod[l], g_norm2[l], w_ff1, w_ff2, l, g_final, geom, last)
        src = (xs, xs, RL)
    return xs.reshape(B, T, D)
```

```python
import functools
import math

import numpy as np
import jax
import jax.numpy as jnp
from jax import lax
from jax.experimental import pallas as pl
from jax.experimental.pallas import tpu as pltpu

F32 = jnp.float32
BF16 = jnp.bfloat16

GRID_W = 64
EPS = 1e-6
ROPE_THETA = 10000.0
MLA_HEADS, MLA_Q_RANK, MLA_KV_RANK, MLA_NOPE, MLA_ROPE, MLA_V = 4, 512, 512, 128, 64, 128
GQA_HEADS, GQA_KV_HEADS, GQA_HD = 8, 2, 128
GDN_HEADS, GDN_DK, GDN_DV, GDN_CONV, GDN_CHUNK = 4, 128, 128, 5, 64
N_MOD = 6
LANE = 128
GDN_BLOCK = 256
HALO_ROWS = 16
V7X_VMEM_BYTES = 64 * 1024 * 1024
VMEM_BIG = V7X_VMEM_BYTES * 7 // 8

TILE = dict(
    adaln_cols=2048,
    inproj_rows=1024, inproj_cols=1024,
    prep_rows=512,
    attn_rows=2048, attn_keys=256,
    gdn_out_rows=512,
    merge_rows=512, merge_cols=512,
    mlp_rows=1024, mlp_cols=512,
    relayout_k=256,
)

OFF_GQ, OFF_CQ, OFF_CKV, OFF_GK, OFF_GV, OFF_OG, OFF_QKV, OFF_MISC = 0, 1024, 1536, 2048, 2304, 2560, 3072, 4608
N_SMALL = 5120
MISC_BETA, MISC_DEC = 64, 72


def _mm(a, b):
    return jnp.dot(a, b, preferred_element_type=F32)


def _mm_nt(a, b):
    return lax.dot_general(a, b, (((1,), (1,)), ((), ())), preferred_element_type=F32)


def _silu(x):
    return x / (1.0 + jnp.exp(-x))


def _sigmoid(x):
    return 1.0 / (1.0 + jnp.exp(-x))


def _rms(x, g):
    return x * lax.rsqrt(jnp.mean(x * x, axis=-1, keepdims=True) + EPS) * g


def _norm_mod(x, g, shift, scale):
    return _rms(x, g) * (1.0 + scale) + shift


def _params(sem, vmem=None):
    kw = dict(dimension_semantics=sem)
    if vmem is not None:
        kw["vmem_limit_bytes"] = vmem
    return pltpu.CompilerParams(**kw)


def _pick_block(pref, *sizes):
    b = pref
    while any(s % b for s in sizes):
        b //= 2
    return b


def _mod_kernel(s_ref, w_ref, b_ref, o_ref):
    s = _silu(s_ref[...])
    o_ref[0] = _mm(s.astype(BF16), w_ref[0].astype(BF16)) + b_ref[0]


def _modulation(c, c_ctx, w_mod, b_mod):
    L, D, N = w_mod.shape
    B = c.shape[0]
    rows = jnp.zeros((8, D), F32).at[:B].set(c).at[B].set(c_ctx)
    bn = TILE["adaln_cols"]
    out = pl.pallas_call(
        _mod_kernel,
        grid=(L, N // bn),
        in_specs=[pl.BlockSpec((8, D), lambda l, j: (0, 0)),
                  pl.BlockSpec((1, D, bn), lambda l, j: (l, 0, j)),
                  pl.BlockSpec((1, 1, bn), lambda l, j: (l, 0, j))],
        out_specs=pl.BlockSpec((1, 8, bn), lambda l, j: (l, 0, j)),
        out_shape=jax.ShapeDtypeStruct((L, 8, N), F32),
        compiler_params=_params(("parallel", "parallel"), VMEM_BIG),
        name="adaln_mod",
    )(rows, w_mod, b_mod.reshape(L, 1, N))
    return out.reshape(L, 8, N_MOD, D)


NORM_ROWS = 64


def _norm_mod_rows(x_ref, g, shift, scale, hx_ref):
    def body(r, carry):
        rs = pl.ds(pl.multiple_of(r * NORM_ROWS, NORM_ROWS), NORM_ROWS)
        hx_ref[rs, :] = _norm_mod(x_ref[rs, :], g, shift, scale).astype(BF16)
        return carry

    lax.fori_loop(0, x_ref.shape[0] // NORM_ROWS, body, 0)


def _row_source_specs(src, bm, D, n_latent_rows):
    nlat = n_latent_rows // bm
    off_b = src[2] // bm
    spec_a = pl.BlockSpec((bm, D), lambda i, j: (jnp.minimum(i, nlat - 1), 0))
    spec_b = pl.BlockSpec((bm, D), lambda i, j: (off_b + jnp.maximum(i - nlat, 0), 0), pipeline_mode=pl.Buffered(1))
    return nlat, spec_a, spec_b


def _for_row_source(nlat, xa_ref, xb_ref, fn):
    i = pl.program_id(0)

    @pl.when(i < nlat)
    def _():
        fn(xa_ref)

    @pl.when(i >= nlat)
    def _():
        fn(xb_ref)


def _inproj_kernel(xa_ref, xb_ref, mod_ref, g_ref, w_ref, o_ref, hx_ref, *, nlat):
    @pl.when(pl.program_id(1) == 0)
    def _():
        m = mod_ref[0]
        _for_row_source(nlat, xa_ref, xb_ref,
                        lambda x_ref: _norm_mod_rows(x_ref, g_ref[...], m[0:1], m[1:2], hx_ref))

    o_ref[...] = _mm_nt(hx_ref[...], w_ref[...]).astype(BF16)


def _in_proj(src, mod_l, g1, wa, geom):
    R, D = geom["R"], src[0].shape[1]
    N = wa.shape[0]
    bm = _pick_block(TILE["inproj_rows"], geom["T"], geom["RC"])
    bn = TILE["inproj_cols"]
    sel = geom["mod_sel"](bm)
    nlat, spec_a, spec_b = _row_source_specs(src, bm, D, geom["RL"])
    return pl.pallas_call(
        functools.partial(_inproj_kernel, nlat=nlat),
        grid=(R // bm, N // bn),
        in_specs=[spec_a, spec_b,
                  pl.BlockSpec((1, N_MOD, D), lambda i, j: (sel(i), 0, 0)),
                  pl.BlockSpec((1, D), lambda i, j: (0, 0)),
                  pl.BlockSpec((bn, D), lambda i, j: (j, 0))],
        out_specs=pl.BlockSpec((bm, bn), lambda i, j: (i, j)),
        out_shape=jax.ShapeDtypeStruct((R, N), BF16),
        scratch_shapes=[pltpu.VMEM((bm, D), BF16)],
        compiler_params=_params(("parallel", "arbitrary"), VMEM_BIG),
        name="in_proj",
    )(src[0], src[1], mod_l, g1.reshape(1, D), wa)


def _prep_kernel(gq_ref, cq_ref, ckv_ref, gk_ref, gv_ref, misc_ref, gmq_ref, gmkv_ref, wqb_ref, wkvb_ref,
                 ggq_ref, ggk_ref, cosg_ref, sing_ref, cosm_ref, sinm_ref,
                 qm_ref, km_ref, vm_ref, qg_ref, kg_ref, vg_ref):
    lane = lax.broadcasted_iota(jnp.int32, (1, LANE), 1)
    cosm, sinm = cosm_ref[...], sinm_ref[...]
    cosg, sing = cosg_ref[...], sing_ref[...]
    first_m = (lane & 31) < 16
    first_g = (lane & 63) < 32
    ones_col = jnp.where(lane == 0, 1.0, 0.0).astype(BF16) + jnp.zeros((cq_ref.shape[0], LANE), BF16)

    def rope_m(x):
        partner = jnp.where(first_m, pltpu.roll(x, LANE - 16, 1), pltpu.roll(x, 16, 1))
        return x * cosm + partner * sinm

    def rope_g(x):
        partner = jnp.where(first_g, pltpu.roll(x, LANE - 32, 1), pltpu.roll(x, 32, 1))
        return x * cosg + partner * sing

    qa = _mm(_rms(cq_ref[...].astype(F32), gmq_ref[...]).astype(BF16), wqb_ref[...])
    kva = _mm(_rms(ckv_ref[...].astype(F32), gmkv_ref[...]).astype(BF16), wkvb_ref[...])
    kpe = rope_m(jnp.where(lane < MLA_ROPE, misc_ref[...].astype(F32), 0.0)).astype(BF16)
    nh = MLA_HEADS * LANE
    for h in range(MLA_HEADS):
        hs = slice(h * LANE, (h + 1) * LANE)
        ps = slice(nh + h * LANE, nh + (h + 1) * LANE)
        qm_ref[h, :, 0:LANE] = qa[:, hs].astype(BF16)
        qm_ref[h, :, LANE:2 * LANE] = rope_m(qa[:, ps]).astype(BF16)
        km_ref[h, :, 0:LANE] = kva[:, hs].astype(BF16)
        km_ref[h, :, LANE:2 * LANE] = kpe
        vm_ref[h, :, 0:LANE] = kva[:, ps].astype(BF16)
        vm_ref[h, :, LANE:2 * LANE] = ones_col

    ggq, ggk = ggq_ref[...], ggk_ref[...]
    for h in range(GQA_HEADS):
        hs = slice(h * LANE, (h + 1) * LANE)
        qg_ref[h] = rope_g(_rms(gq_ref[:, hs].astype(F32), ggq)).astype(BF16)
    for h in range(GQA_KV_HEADS):
        hs = slice(h * LANE, (h + 1) * LANE)
        kg_ref[h] = rope_g(_rms(gk_ref[:, hs].astype(F32), ggk)).astype(BF16)
        vg_ref[h, :, 0:LANE] = gv_ref[:, hs]
        vg_ref[h, :, LANE:2 * LANE] = ones_col


def _attn_prep(P, gmq, gmkv, wqb, wkvb, ggq, ggk, tabs, geom):
    R = P.shape[0]
    bm = _pick_block(TILE["prep_rows"], geom["T"], geom["RC"])
    nlat = geom["RL"] // bm
    per = geom["T"] // bm
    tsel = lambda i: jnp.where(i < nlat, i % per, per)
    col = lambda w, off: pl.BlockSpec((bm, w), lambda i: (i, off // w))
    full = lambda a: pl.BlockSpec(a.shape, lambda i: (0,) * a.ndim)
    tab = pl.BlockSpec((bm, LANE), lambda i: (tsel(i), 0))
    outs = [((MLA_HEADS, R, 2 * LANE), 2 * LANE), ((MLA_HEADS, R, 2 * LANE), 2 * LANE),
            ((MLA_HEADS, R, 2 * LANE), 2 * LANE), ((GQA_HEADS, R, LANE), LANE), ((GQA_KV_HEADS, R, LANE), LANE),
            ((GQA_KV_HEADS, R, 2 * LANE), 2 * LANE)]
    gmq, gmkv, ggq, ggk = (a.reshape(1, -1) for a in (gmq, gmkv, ggq, ggk))
    return pl.pallas_call(
        _prep_kernel,
        grid=(R // bm,),
        in_specs=[col(1024, OFF_GQ), col(512, OFF_CQ), col(512, OFF_CKV), col(256, OFF_GK), col(256, OFF_GV),
                  col(LANE, OFF_MISC), full(gmq), full(gmkv), full(wqb), full(wkvb), full(ggq), full(ggk),
                  tab, tab, tab, tab],
        out_specs=[pl.BlockSpec((s[0], bm, w), lambda i: (0, i, 0)) for s, w in outs],
        out_shape=[jax.ShapeDtypeStruct(s, BF16) for s, _ in outs],
        compiler_params=_params(("parallel",), VMEM_BIG),
        name="attn_prep",
    )(P, P, P, P, P, P, gmq, gmkv, wqb, wkvb, ggq, ggk, *tabs)


def _softmax_pv(q, kv_chunks, s_ref, o_ref, exp2_scale):
    sub = min(q.shape)
    for r0 in range(0, q.shape[0], sub):
        rows = slice(r0, r0 + sub)
        qr = q[rows]
        m_acc = None
        off = 0
        for k_ref, _, st, n in kv_chunks:
            s = _mm_nt(qr, k_ref[0, st:st + n, :])
            s_ref[rows, off:off + n] = s
            for t in range(n // LANE):
                tile = s[:, t * LANE:(t + 1) * LANE]
                m_acc = tile if m_acc is None else jnp.maximum(m_acc, tile)
            off += n
        m = jnp.max(m_acc, axis=-1, keepdims=True)
        acc = None
        off = 0
        for _, v_ref, st, n in kv_chunks:
            p = jnp.exp2((s_ref[rows, off:off + n] - m) * exp2_scale).astype(BF16)
            pv = _mm(p, v_ref[0, st:st + n, :])
            acc = pv if acc is None else acc + pv
            off += n
        o_ref[rows, :] = (acc[:, :LANE] / acc[:, LANE:LANE + 1]).astype(BF16)


def _attn_kernel(q_ref, kx_ref, kc_ref, vx_ref, vc_ref, o_ref, s_ref, *, exp2_scale, kchunk):
    chunks = [(kx_ref, vx_ref, i * kchunk, kchunk) for i in range(kx_ref.shape[1] // kchunk)]
    chunks.append((kc_ref, vc_ref, 0, kc_ref.shape[1]))
    _softmax_pv(q_ref[0], chunks, s_ref, o_ref, exp2_scale)


def _attn_ctx_kernel(q_ref, kc_ref, vc_ref, o_ref, s_ref, *, exp2_scale):
    _softmax_pv(q_ref[0], [(kc_ref, vc_ref, 0, kc_ref.shape[1])], s_ref, o_ref, exp2_scale)


def _attention(q, k, v, scale, geom, name):
    H, _, dk = q.shape
    Hk, _, dve = v.shape
    grp = H // Hk
    B, T, C, RL = geom["B"], geom["T"], geom["C"], geom["RL"]
    bq = _pick_block(TILE["attn_rows"], T)
    nq = T // bq
    cb = RL // C
    return pl.pallas_call(
        functools.partial(_attn_kernel, exp2_scale=scale * math.log2(math.e),
                          kchunk=TILE["attn_keys"]),
        grid=(B, H, nq),
        in_specs=[pl.BlockSpec((1, bq, dk), lambda b, h, j: (h, b * nq + j, 0)),
                  pl.BlockSpec((1, T, dk), lambda b, h, j: (h // grp, b, 0)),
                  pl.BlockSpec((1, C, dk), lambda b, h, j: (h // grp, cb + b, 0)),
                  pl.BlockSpec((1, T, dve), lambda b, h, j: (h // grp, b, 0)),
                  pl.BlockSpec((1, C, dve), lambda b, h, j: (h // grp, cb + b, 0))],
        out_specs=pl.BlockSpec((bq, LANE), lambda b, h, j: (b * nq + j, h)),
        out_shape=jax.ShapeDtypeStruct((RL, H * LANE), BF16),
        scratch_shapes=[pltpu.VMEM((bq, T + C), F32)],
        compiler_params=_params(("parallel", "parallel", "parallel"), VMEM_BIG),
        name=name,
    )(q, k, k, v, v)


def _attention_ctx(q, k, v, scale, geom, name):
    H, _, dk = q.shape
    Hk, _, dve = v.shape
    grp = H // Hk
    B, C, RL, RC = geom["B"], geom["C"], geom["RL"], geom["RC"]
    cb = RL // C
    return pl.pallas_call(
        functools.partial(_attn_ctx_kernel, exp2_scale=scale * math.log2(math.e)),
        grid=(B, H),
        in_specs=[pl.BlockSpec((1, C, dk), lambda b, h: (h, cb + b, 0)),
                  pl.BlockSpec((1, C, dk), lambda b, h: (h // grp, cb + b, 0)),
                  pl.BlockSpec((1, C, dve), lambda b, h: (h // grp, cb + b, 0))],
        out_specs=pl.BlockSpec((C, LANE), lambda b, h: (b, h)),
        out_shape=jax.ShapeDtypeStruct((RC, H * LANE), BF16),
        scratch_shapes=[pltpu.VMEM((C, C), F32)],
        compiler_params=_params(("parallel", "parallel")),
        name=name,
    )(q, k, v)


def _gdn_conv_gates(cur_ref, prev_ref, next_ref, misc_ref, wc_ref, gvec_ref, ext_ref, *, nlat, per_seq, nblk):
    r = jnp.minimum(pl.program_id(0), nblk - 1)
    pos = r % per_seq
    is_ctx = r >= nlat
    pf = jnp.where(jnp.logical_or(is_ctx, pos == 0), 0.0, 1.0).astype(F32)
    nf = jnp.where(jnp.logical_or(is_ctx, pos == per_seq - 1), 0.0, 1.0).astype(F32)
    nb = GDN_BLOCK
    hal = HALO_ROWS
    ext_ref[0:hal, :] = prev_ref[...].astype(F32) * pf
    ext_ref[hal:hal + nb, :] = cur_ref[...].astype(F32)
    ext_ref[hal + nb:2 * hal + nb, :] = next_ref[...].astype(F32) * nf
    pad = GDN_CONV // 2
    n_qk = 2 * GDN_HEADS
    heads = []
    for c in range(3 * GDN_HEADS):
        cs = slice(c * LANE, (c + 1) * LANE)
        acc = None
        for j in range(GDN_CONV):
            t = ext_ref[hal - pad + j:hal - pad + j + nb, cs] * wc_ref[j:j + 1, cs]
            acc = t if acc is None else acc + t
        y = _silu(acc)
        if c < n_qk:
            y = y * lax.rsqrt(jnp.sum(y * y, axis=-1, keepdims=True) + EPS)
        heads.append(y)

    lane = lax.broadcasted_iota(jnp.int32, (1, LANE), 1)
    raw = misc_ref[...].astype(F32)
    beta = _sigmoid(raw)
    z = raw + gvec_ref[1:2, :]
    softplus = jnp.maximum(z, 0.0) + jnp.log1p(jnp.exp(-jnp.abs(z)))
    g = -jnp.exp(gvec_ref[0:1, :]) * softplus
    is_beta = jnp.logical_and(lane >= MISC_BETA, lane < MISC_DEC)
    is_g = jnp.logical_and(lane >= MISC_DEC, lane < MISC_DEC + 2 * GDN_HEADS)
    return heads, jnp.where(is_beta, beta, jnp.where(is_g, g, 0.0))


def _split3(x):
    h = x.astype(BF16)
    r = x - h.astype(F32)
    m = r.astype(BF16)
    l = (r - m.astype(F32)).astype(BF16)
    return h, m, l


def _gdn_chunk_local(heads, gt, u_ref, w_ref, qg_ref, kgt_ref, qk_ref):
    nb = GDN_BLOCK
    nh = GDN_HEADS
    ri = lax.broadcasted_iota(jnp.int32, (nb, nb), 0)
    ci = lax.broadcasted_iota(jnp.int32, (nb, nb), 1)
    same = (ri >> 6) == (ci >> 6)
    low = jnp.logical_and(same, ri >= ci)
    upp = jnp.logical_and(same, ri <= ci)
    slow = jnp.logical_and(same, ri > ci)
    supp = jnp.logical_and(same, ri < ci)
    ltri = jnp.where(low, 1.0, 0.0).astype(BF16)
    utri = jnp.where(upp, 1.0, 0.0).astype(BF16)
    eye = jnp.where(ri == ci, 1.0, 0.0)
    pair = (ri >> 1) == (ci >> 1)
    offs = [jnp.logical_and((ri >> (lv + 1)) == (ci >> (lv + 1)), (ri >> lv) != (ci >> lv))
            for lv in range(1, int(math.log2(GDN_CHUNK)))]

    g3 = _split3(gt)
    g3t = _split3(gt.T)
    cum_c = (sum(_mm(ltri, p) for p in reversed(g3)), sum(_mm(utri, p) for p in reversed(g3)))
    cum_r = (sum(_mm(p, utri) for p in reversed(g3t)), sum(_mm(p, ltri) for p in reversed(g3t)))
    tot_c = cum_c[0] + cum_c[1] - gt

    scale = GDN_DK ** -0.5
    insts = [(d, h) for d in range(2) for h in range(nh)]
    a_all, tinv_all, rhs_all = [], [], []

    def prep(d, h):
        mask, smask = (low, slow) if d == 0 else (upp, supp)
        ib = MISC_BETA + d * nh + h
        ig = MISC_DEC + d * nh + h
        beta = gt[:, ib:ib + 1]
        gc = cum_c[d][:, ig:ig + 1]
        gr = cum_r[d][ig:ig + 1, :]
        gl = tot_c[:, ig:ig + 1]
        q, k, v = heads[h], heads[nh + h], heads[2 * nh + h]
        decay = jnp.where(mask, jnp.exp(jnp.where(mask, gc - gr, 0.0)), 0.0)
        kb = k * beta
        k16 = k.astype(BF16)
        a = jnp.where(smask, _mm_nt(kb.astype(BF16), k16) * decay, 0.0)
        a_all.append(a.astype(BF16))
        tinv_all.append(eye - jnp.where(pair, a, 0.0))
        rhs_all.append(jnp.concatenate([v * beta, kb * jnp.exp(gc)], axis=1).astype(BF16))
        hs = slice(h * LANE, (h + 1) * LANE)
        qs = q * scale
        qk_ref[d, :, h * nb:(h + 1) * nb] = (_mm_nt(qs.astype(BF16), k16) * decay).astype(BF16)
        qg_ref[d, :, hs] = (qs * jnp.exp(gc)).astype(BF16)
        kgt_ref[d, hs, :] = (k * jnp.exp(gl - gc)).T.astype(BF16)

    zero16 = jnp.zeros((nb, nb), BF16)

    def level(i, off):
        t16 = tinv_all[i].astype(BF16)
        ta = _mm(t16, jnp.where(off, a_all[i], zero16))
        tinv_all[i] = tinv_all[i] - _mm(ta.astype(BF16), t16)

    def finish(i):
        d, h = insts[i]
        x = _mm(tinv_all[i].astype(BF16), rhs_all[i])
        hs = slice(h * LANE, (h + 1) * LANE)
        u_ref[d, :, hs] = x[:, :LANE]
        w_ref[d, :, hs] = x[:, LANE:].astype(BF16)

    for d, h in insts:
        prep(d, h)
    for off in offs:
        for i in range(len(insts)):
            level(i, off)
    for i in range(len(insts)):
        finish(i)


def _gdn_local_kernel(cur_ref, prev_ref, next_ref, misc_ref, wc_ref, gvec_ref,
                      gates_ref, u_ref, w_ref, qg_ref, kgt_ref, qk_ref, ext_ref, *, nlat, per_seq, nblk):
    heads, gt = _gdn_conv_gates(cur_ref, prev_ref, next_ref, misc_ref, wc_ref, gvec_ref, ext_ref,
                                nlat=nlat, per_seq=per_seq, nblk=nblk)
    gates_ref[...] = gt
    _gdn_chunk_local(heads, gt, u_ref, w_ref, qg_ref, kgt_ref, qk_ref)


def _gdn_local(P, w_conv, a_log, dt_bias, geom):
    R = P.shape[0]
    nb = GDN_BLOCK
    HW = GDN_HEADS * LANE
    W = 3 * HW
    hal = HALO_ROWS
    sub = nb // hal
    gvec = jnp.zeros((2, LANE), F32)
    gvec = gvec.at[0, MISC_DEC:MISC_DEC + 2 * GDN_HEADS].set(a_log.reshape(-1))
    gvec = gvec.at[1, MISC_DEC:MISC_DEC + 2 * GDN_HEADS].set(dt_bias.reshape(-1))
    qc = OFF_QKV // W
    nblk = R // nb
    cb = ob = lambda t: t
    return pl.pallas_call(
        functools.partial(_gdn_local_kernel, nlat=geom["RL"] // nb, per_seq=geom["T"] // nb, nblk=nblk),
        grid=(nblk,),
        in_specs=[pl.BlockSpec((nb, W), lambda t: (cb(t), qc)),
                  pl.BlockSpec((hal, W), lambda t: (jnp.maximum(cb(t) * sub - 1, 0), qc)),
                  pl.BlockSpec((hal, W), lambda t: (jnp.minimum((cb(t) + 1) * sub, R // hal - 1), qc)),
                  pl.BlockSpec((nb, LANE), lambda t: (cb(t), OFF_MISC // LANE)),
                  pl.BlockSpec((GDN_CONV, W), lambda t: (0, 0)),
                  pl.BlockSpec((2, LANE), lambda t: (0, 0))],
        out_specs=[pl.BlockSpec((nb, LANE), lambda t: (ob(t), 0)),
                   pl.BlockSpec((2, nb, HW), lambda t: (0, ob(t), 0)),
                   pl.BlockSpec((2, nb, HW), lambda t: (0, ob(t), 0)),
                   pl.BlockSpec((2, nb, HW), lambda t: (0, ob(t), 0)),
                   pl.BlockSpec((2, HW, nb), lambda t: (0, 0, ob(t))),
                   pl.BlockSpec((2, nb, GDN_HEADS * nb), lambda t: (0, ob(t), 0))],
        out_shape=[jax.ShapeDtypeStruct((R, LANE), F32),
                   jax.ShapeDtypeStruct((2, R, HW), F32),
                   jax.ShapeDtypeStruct((2, R, HW), BF16),
                   jax.ShapeDtypeStruct((2, R, HW), BF16),
                   jax.ShapeDtypeStruct((2, HW, R), BF16),
                   jax.ShapeDtypeStruct((2, R, GDN_HEADS * nb), BF16)],
        scratch_shapes=[pltpu.VMEM((nb + 2 * hal, W), F32)],
        compiler_params=_params(("parallel",), VMEM_BIG),
        name="gdn_local",
    )(P, P, P, P, w_conv, gvec)


def _gdn_c_kernel(*refs):
    nb = GDN_BLOCK
    nh = GDN_HEADS
    ins, (of_ref, ob_ref, s_ref, vn_ref) = (refs[0:6], refs[6:12]), refs[12:]
    o_refs = (of_ref, ob_ref)

    @pl.when(pl.program_id(1) == 0)
    def _():
        s_ref[...] = jnp.zeros_like(s_ref)
        vn_ref[...] = jnp.zeros_like(vn_ref)

    rowi = lax.broadcasted_iota(jnp.int32, (nb, 1), 0)
    nchunk = nb // GDN_CHUNK
    for step in range(nchunk):
        for d in range(2):
            u_ref, w_ref, qg_ref, kgt_ref, qk_ref, gates_ref = ins[d]
            c = step if d == 0 else nchunk - 1 - step
            rs = slice(c * GDN_CHUNK, (c + 1) * GDN_CHUNK)
            decay_all = jnp.exp(jnp.sum(gates_ref[rs, :], axis=0, keepdims=True))
            cmask = jnp.logical_and(rowi >= c * GDN_CHUNK, rowi < (c + 1) * GDN_CHUNK)
            for h in range(nh):
                hs = slice(h * LANE, (h + 1) * LANE)
                ig = MISC_DEC + d * nh + h
                s = s_ref[d, h]
                s16 = s.astype(BF16)
                v_new = u_ref[0, rs, hs] - _mm(w_ref[0, rs, hs], s16)
                vn_ref[d, h, rs, :] = v_new.astype(BF16)
                vn = vn_ref[d, h]
                o_refs[d][rs, hs] = _mm(qg_ref[0, rs, hs], s16) + _mm(qk_ref[0, rs, h * nb:(h + 1) * nb], vn)
                v_cur = jnp.where(cmask, vn, jnp.zeros_like(vn))
                s_ref[d, h] = s * decay_all[:, ig:ig + 1] + _mm(kgt_ref[0, hs, :], v_cur)


def _gdn_scan(u, w, qg, kgt, qk, gates, geom):
    nb = GDN_BLOCK
    HW = GDN_HEADS * LANE
    B, RL, R = geom["B"], geom["RL"], geom["R"]
    per = geom["T"] // nb
    nlat = RL // nb

    def blk(d):
        return lambda b, j: jnp.where(j == 0, nlat + b, b * per + (j - 1 if d == 0 else per - j))

    def dir_specs(d):
        bk = blk(d)
        tok = lambda w_: pl.BlockSpec((1, nb, w_), lambda b, j: (d, bk(b, j), 0))
        return [tok(HW), tok(HW), tok(HW),
                pl.BlockSpec((1, HW, nb), lambda b, j: (d, 0, bk(b, j))),
                tok(GDN_HEADS * nb),
                pl.BlockSpec((nb, LANE), lambda b, j: (bk(b, j), 0))]

    args = (u, w, qg, kgt, qk, gates)
    return pl.pallas_call(
        _gdn_c_kernel,
        grid=(B, per + 1),
        in_specs=dir_specs(0) + dir_specs(1),
        out_specs=[pl.BlockSpec((nb, HW), lambda b, j: (blk(0)(b, j), 0)),
                   pl.BlockSpec((nb, HW), lambda b, j: (blk(1)(b, j), 0))],
        out_shape=[jax.ShapeDtypeStruct((R, HW), F32), jax.ShapeDtypeStruct((R, HW), F32)],
        scratch_shapes=[pltpu.VMEM((2, GDN_HEADS, GDN_DK, GDN_DV), F32),
                        pltpu.VMEM((2, GDN_HEADS, nb, GDN_DV), BF16)],
        compiler_params=_params(("parallel", "arbitrary")),
        name="gdn_scan",
    )(*args, *args)


def _gdn_out_kernel(of_ref, ob_ref, og_ref, g_ref, o_ref):
    o = of_ref[...] + ob_ref[...]
    gate = _silu(og_ref[...].astype(F32))
    g = g_ref[...]
    for h in range(GDN_HEADS):
        hs = slice(h * LANE, (h + 1) * LANE)
        o_ref[:, hs] = (_rms(o[:, hs], g) * gate[:, hs]).astype(BF16)


def _gdn_out(o_f, o_b, P, g_out, rows):
    HW = GDN_HEADS * LANE
    bm = _pick_block(TILE["gdn_out_rows"], rows)
    return pl.pallas_call(
        _gdn_out_kernel,
        grid=(rows // bm,),
        in_specs=[pl.BlockSpec((bm, HW), lambda i: (i, 0)),
                  pl.BlockSpec((bm, HW), lambda i: (i, 0)),
                  pl.BlockSpec((bm, HW), lambda i: (i, OFF_OG // HW)),
                  pl.BlockSpec((1, LANE), lambda i: (0, 0))],
        out_specs=pl.BlockSpec((bm, HW), lambda i: (i, 0)),
        out_shape=jax.ShapeDtypeStruct((rows, HW), BF16),
        compiler_params=_params(("parallel",)),
        name="gdn_out",
    )(o_f, o_b, P, g_out.reshape(1, LANE))


def _residual_rows(x_ref, gate, o_ref, post=None):
    def body(r, carry):
        rs = pl.ds(pl.multiple_of(r * NORM_ROWS, NORM_ROWS), NORM_ROWS)
        v = x_ref[rs, :] + gate * o_ref[rs, :]
        o_ref[rs, :] = v if post is None else post(v)
        return carry

    lax.fori_loop(0, x_ref.shape[0] // NORM_ROWS, body, 0)


def _merge_kernel(xa_ref, xb_ref, mod_ref, g_ref, oal_ref, oac_ref, obl_ref, obc_ref, oc_ref, wga_ref, wgb_ref,
                  wgc_ref, wua_ref, wub_ref, wuc_ref, wo_ref, o_ref, hx_ref, oa_ref, ob_ref, *, nlat):
    j = pl.program_id(1)
    m = mod_ref[0]

    @pl.when(j == 0)
    def _():
        _for_row_source(nlat, xa_ref, xb_ref,
                        lambda x_ref: _norm_mod_rows(x_ref, g_ref[...], m[0:1], m[1:2], hx_ref))

        def stage(dst_ref):
            return lambda src_ref: dst_ref.__setitem__(Ellipsis, src_ref[...])

        _for_row_source(nlat, oal_ref, oac_ref, stage(oa_ref))
        _for_row_source(nlat, obl_ref, obc_ref, stage(ob_ref))
        o_ref[...] = jnp.zeros_like(o_ref)

    hx = hx_ref[...]
    oa, ob, oc = oa_ref[...], ob_ref[...], oc_ref[...]
    hc = wo_ref.shape[0]
    ga, gb, gc = _mm_nt(hx, wga_ref[...]), _mm_nt(hx, wgb_ref[...]), _mm_nt(hx, wgc_ref[...])
    ua, ub, uc = _mm(oa, wua_ref[...]), _mm(ob, wub_ref[...]), _mm(oc, wuc_ref[...])
    acc = None
    for cs in (slice(0, hc // 2), slice(hc // 2, hc)):
        y = _sigmoid(ga[:, cs]) * ua[:, cs] + _sigmoid(gb[:, cs]) * ub[:, cs] + _sigmoid(gc[:, cs]) * uc[:, cs]
        t = _mm(y.astype(BF16), wo_ref[cs, :])
        acc = t if acc is None else acc + t
    o_ref[...] += acc

    @pl.when(j == pl.num_programs(1) - 1)
    def _():
        _for_row_source(nlat, xa_ref, xb_ref, lambda x_ref: _residual_rows(x_ref, m[2:3], o_ref))


def _merge(src, mod_l, g1, oa, ob, oc, wg, wua, wub, wuc, wo, geom, rows):
    D = src[0].shape[1]
    bm = _pick_block(TILE["merge_rows"], geom["T"], geom["RC"])
    hc = TILE["merge_cols"]
    nj = D // hc
    sel = geom["mod_sel"](bm)
    rowb = lambda w_: pl.BlockSpec((bm, w_), lambda i, j: (i, 0))
    gate_w = lambda br: pl.BlockSpec((hc, D), lambda i, j: (br * nj + j, 0))
    nlat, spec_a, spec_b = _row_source_specs(src, bm, D, geom["RL"])
    lat = lambda w_: pl.BlockSpec((bm, w_), lambda i, j: (jnp.minimum(i, nlat - 1), 0))
    ctx = lambda w_: pl.BlockSpec((bm, w_), lambda i, j: (jnp.maximum(i - nlat, 0), 0))
    wa_, wb_ = oa[0].shape[1], ob[0].shape[1]
    return pl.pallas_call(
        functools.partial(_merge_kernel, nlat=nlat),
        grid=(rows // bm, nj),
        in_specs=[spec_a, spec_b,
                  pl.BlockSpec((1, N_MOD, D), lambda i, j: (sel(i), 0, 0)),
                  pl.BlockSpec((1, D), lambda i, j: (0, 0)),
                  lat(wa_), ctx(wa_), lat(wb_), ctx(wb_), rowb(oc.shape[1]),
                  gate_w(0), gate_w(1), gate_w(2),
                  pl.BlockSpec((wua.shape[0], hc), lambda i, j: (0, j)),
                  pl.BlockSpec((wub.shape[0], hc), lambda i, j: (0, j)),
                  pl.BlockSpec((wuc.shape[0], hc), lambda i, j: (0, j)),
                  pl.BlockSpec((hc, D), lambda i, j: (j, 0))],
        out_specs=rowb(D),
        out_shape=jax.ShapeDtypeStruct((rows, D), F32),
        scratch_shapes=[pltpu.VMEM((bm, D), BF16), pltpu.VMEM((bm, wa_), BF16), pltpu.VMEM((bm, wb_), BF16)],
        compiler_params=_params(("parallel", "arbitrary"), VMEM_BIG),
        name="merge",
    )(src[0], src[1], mod_l, g1.reshape(1, D), oa[0], oa[1], ob[0], ob[1], oc, wg, wg, wg, wua, wub, wuc, wo)


def _mlp_kernel(x_ref, mod_ref, g_ref, w1_ref, w2_ref, gf_ref, o_ref, hx_ref, *, final):
    j = pl.program_id(1)
    m = mod_ref[0]

    @pl.when(j == 0)
    def _():
        _norm_mod_rows(x_ref, g_ref[...], m[3:4], m[4:5], hx_ref)
        o_ref[...] = jnp.zeros_like(o_ref)

    hx = hx_ref[...]
    fc = w2_ref.shape[1]
    h1 = _mm(hx, w1_ref[0].astype(BF16))
    acc = None
    for cs in (slice(0, fc // 2), slice(fc // 2, fc)):
        h = jnp.maximum(h1[:, cs], 0.0)
        t = _mm((h * h).astype(BF16), w2_ref[0, cs, :].astype(BF16))
        acc = t if acc is None else acc + t
    o_ref[...] += acc

    @pl.when(j == pl.num_programs(1) - 1)
    def _():
        gf = gf_ref[...]
        _residual_rows(x_ref, m[5:6], o_ref, (lambda v: _rms(v, gf)) if final else None)


def _mlp(x1, mod_l, g2, w1, w2, l, g_final, geom, final):
    rows, D = x1.shape
    F = w1.shape[2]
    bm = _pick_block(TILE["mlp_rows"], geom["T"], geom["RC"])
    fc = TILE["mlp_cols"]
    sel = geom["mod_sel"](bm)
    return pl.pallas_call(
        functools.partial(_mlp_kernel, final=final),
        grid=(rows // bm, F // fc),
        in_specs=[pl.BlockSpec((bm, D), lambda i, j: (i, 0), pipeline_mode=pl.Buffered(1)),
                  pl.BlockSpec((1, N_MOD, D), lambda i, j: (sel(i), 0, 0)),
                  pl.BlockSpec((1, D), lambda i, j: (0, 0)),
                  pl.BlockSpec((1, D, fc), lambda i, j: (l, 0, j)),
                  pl.BlockSpec((1, fc, D), lambda i, j: (l, j, 0)),
                  pl.BlockSpec((1, D), lambda i, j: (0, 0))],
        out_specs=pl.BlockSpec((bm, D), lambda i, j: (i, 0)),
        out_shape=jax.ShapeDtypeStruct((rows, D), F32),
        scratch_shapes=[pltpu.VMEM((bm, D), BF16)],
        compiler_params=_params(("parallel", "arbitrary"), VMEM_BIG),
        name="mlp",
    )(x1, mod_l, g2.reshape(1, D), w1, w2, g_final.reshape(1, D))


def _rope_tables(T, bm):
    t = np.arange(T)
    row, col = t // GRID_W, t % GRID_W

    def table(n_rot):
        n = n_rot // 2
        half = n // 2
        inv = ROPE_THETA ** (-np.arange(half, dtype=np.float64) / half)
        cos = np.zeros((T + bm, LANE))
        sin = np.zeros((T + bm, LANE))
        for s, pos in enumerate((row, col)):
            ang = pos[:, None].astype(np.float32).astype(np.float64) * inv.astype(np.float32)[None, :]
            ang = ang.astype(np.float32).astype(np.float64)
            c, sn = np.cos(ang), np.sin(ang)
            cos[:T, s * n:s * n + half] = c
            cos[:T, s * n + half:(s + 1) * n] = c
            sin[:T, s * n:s * n + half] = -sn
            sin[:T, s * n + half:(s + 1) * n] = sn
        cos[T:, :] = 1.0
        return jnp.asarray(cos, F32), jnp.asarray(sin, F32)

    cg, sg = table(GQA_HD)
    cm, sm = table(MLA_ROPE)
    return cg, sg, cm, sm


def _relayout_kernel(w_ref, wa_ref, wg_ref, *, src_off):
    for name, dst, width in (("gq", OFF_GQ, 1024), ("cq", OFF_CQ, 512), ("ckv", OFF_CKV, 512), ("gk", OFF_GK, 256),
                             ("gv", OFF_GV, 256), ("og", OFF_OG, 512), ("qkv", OFF_QKV, 1536),
                             ("kpe", OFF_MISC, MLA_ROPE), ("beta", OFF_MISC + MISC_BETA, 4 * GDN_HEADS)):
        s = src_off[name]
        wa_ref[dst:dst + width, :] = w_ref[0, s:s + width, :].astype(BF16)
    assert src_off["dec"] == src_off["beta"] + 2 * GDN_HEADS
    used = OFF_MISC + MISC_DEC + 2 * GDN_HEADS
    wa_ref[used:, :] = jnp.zeros((N_SMALL - used, wa_ref.shape[1]), BF16)
    s = src_off["bg"]
    wg_ref[...] = w_ref[0, s:s + wg_ref.shape[0], :].astype(BF16)


def _relayout_w_in(w_in, l):
    w_t = jnp.swapaxes(w_in, 1, 2)
    L, N, D = w_t.shape
    splits = (("cq", MLA_Q_RANK), ("ckv", MLA_KV_RANK), ("kpe", MLA_ROPE), ("gq", GQA_HEADS * GQA_HD),
              ("gk", GQA_KV_HEADS * GQA_HD), ("gv", GQA_KV_HEADS * GQA_HD),
              ("qkv", GDN_HEADS * (2 * GDN_DK + GDN_DV)), ("beta", 2 * GDN_HEADS), ("dec", 2 * GDN_HEADS),
              ("og", GDN_HEADS * GDN_DV), ("bg", 3 * D))
    src_off, o = {}, 0
    for name, n in splits:
        src_off[name] = o
        o += n
    assert o == N
    kc = TILE["relayout_k"]
    return pl.pallas_call(
        functools.partial(_relayout_kernel, src_off=src_off),
        grid=(D // kc,),
        in_specs=[pl.BlockSpec((1, N, kc), lambda i: (l, 0, i))],
        out_specs=[pl.BlockSpec((N_SMALL, kc), lambda i: (0, i)),
                   pl.BlockSpec((3 * D, kc), lambda i: (0, i))],
        out_shape=[jax.ShapeDtypeStruct((N_SMALL, D), BF16), jax.ShapeDtypeStruct((3 * D, D), BF16)],
        compiler_params=_params(("parallel",), VMEM_BIG),
        name="w_in_relayout",
    )(w_t)


def _prep_layer_weights(w_qb, w_kvb):
    qb = w_qb.reshape(-1, MLA_HEADS, MLA_NOPE + MLA_ROPE)
    qpe = jnp.pad(qb[:, :, MLA_NOPE:], ((0, 0), (0, 0), (0, LANE - MLA_ROPE)))
    wqb = jnp.concatenate([qb[:, :, :MLA_NOPE].reshape(-1, MLA_HEADS * LANE),
                           qpe.reshape(-1, MLA_HEADS * LANE)], axis=1).astype(BF16)
    kvb = w_kvb.reshape(-1, MLA_HEADS, MLA_NOPE + MLA_V)
    wkvb = jnp.concatenate([kvb[:, :, :MLA_NOPE].reshape(-1, MLA_HEADS * LANE),
                            kvb[:, :, MLA_NOPE:].reshape(-1, MLA_HEADS * LANE)], axis=1).astype(BF16)
    return wqb, wkvb


def kernel(x, c, ctx, c_ctx, w_mod, b_mod, g_norm1, w_in, g_mla_q, w_mla_qb, g_mla_kv, w_mla_kvb, g_gqa_q, g_gqa_k,
           w_conv, a_log, dt_bias, g_gdn_out, w_up_a, w_up_b, w_up_c, w_out, g_norm2, w_ff1, w_ff2, g_final):
    B, T, D = x.shape
    C = ctx.shape[1]
    L = w_mod.shape[0]
    assert C == GDN_BLOCK and T % GDN_BLOCK == 0 and T % GRID_W == 0 and B < 8
    RL, RC = B * T, B * C
    geom = dict(B=B, T=T, C=C, RL=RL, RC=RC, R=RL + RC)
    geom["mod_sel"] = lambda bm: (lambda i: jnp.where(i < RL // bm, i // (T // bm), B))

    mod = _modulation(c, c_ctx, w_mod, b_mod)
    tabs = _rope_tables(T, _pick_block(TILE["prep_rows"], T, RC))
    src = (x.reshape(RL, D), ctx.reshape(RC, D), 0)
    mla_scale = (MLA_NOPE + MLA_ROPE) ** -0.5
    gqa_scale = GQA_HD ** -0.5

    for l in range(L):
        last = l == L - 1
        wa, wg = _relayout_w_in(w_in, l)
        wqb, wkvb = _prep_layer_weights(w_mla_qb[l], w_mla_kvb[l])
        P = _in_proj(src, mod[l], g_norm1[l], wa, geom)
        qm, km, vm, qg, kg, vg = _attn_prep(P, g_mla_q[l], g_mla_kv[l], wqb, wkvb, g_gqa_q[l], g_gqa_k[l], tabs, geom)
        rows = RL if last else RL + RC
        oa = _attention(qm, km, vm, mla_scale, geom, "attn_mla")
        ob = _attention(qg, kg, vg, gqa_scale, geom, "attn_gqa")
        gates, u, w, qgd, kgt, qk = _gdn_local(P, w_conv[l], a_log[l], dt_bias[l], geom)
        o_f, o_b = _gdn_scan(u, w, qgd, kgt, qk, gates, geom)
        oc = _gdn_out(o_f, o_b, P, g_gdn_out[l], rows)
        if last:
            oa, ob = (oa, oa), (ob, ob)
        else:
            oa = (oa, _attention_ctx(qm, km, vm, mla_scale, geom, "attn_mla_ctx"))
            ob = (ob, _attention_ctx(qg, kg, vg, gqa_scale, geom, "attn_gqa_ctx"))
        x1 = _merge(src, mod[l], g_norm1[l], oa, ob, oc, wg, w_up_a[l].astype(BF16), w_up_b[l].astype(BF16),
                    w_up_c[l].astype(BF16), w_out[l].astype(BF16), geom, rows)
        xs = _mlp(x1, mod[l], g_norm2[l], w_ff1, w_ff2, l, g_final, geom, last)
        src = (xs, xs, RL)
    return xs.reshape(B, T, D)
```

```python
import functools
import math

import numpy as np
import jax
import jax.numpy as jnp
from jax import lax
from jax.experimental import pallas as pl
from jax.experimental.pallas import tpu as pltpu

F32 = jnp.float32
BF16 = jnp.bfloat16

GRID_W = 64
EPS = 1e-6
ROPE_THETA = 10000.0
MLA_HEADS, MLA_Q_RANK, MLA_KV_RANK, MLA_NOPE, MLA_ROPE, MLA_V = 4, 512, 512, 128, 64, 128
GQA_HEADS, GQA_KV_HEADS, GQA_HD = 8, 2, 128
GDN_HEADS, GDN_DK, GDN_DV, GDN_CONV, GDN_CHUNK = 4, 128, 128, 5, 64
N_MOD = 6
LANE = 128
GDN_BLOCK = 256
HALO_ROWS = 16
V7X_VMEM_BYTES = 64 * 1024 * 1024
VMEM_BIG = V7X_VMEM_BYTES * 7 // 8

TILE = dict(
    adaln_cols=2048,
    inproj_rows=1024, inproj_cols=1024,
    prep_rows=512,
    attn_rows=2048, attn_keys=256,
    gdn_out_rows=512,
    merge_rows=512, merge_cols=512,
    mlp_rows=1024, mlp_cols=512,
    relayout_k=256,
)

OFF_GQ, OFF_CQ, OFF_CKV, OFF_GK, OFF_GV, OFF_OG, OFF_QKV, OFF_MISC = 0, 1024, 1536, 2048, 2304, 2560, 3072, 4608
N_SMALL = 5120
MISC_BETA, MISC_DEC = 64, 72


def _mm(a, b):
    return jnp.dot(a, b, preferred_element_type=F32)


def _mm_nt(a, b):
    return lax.dot_general(a, b, (((1,), (1,)), ((), ())), preferred_element_type=F32)


def _silu(x):
    return x / (1.0 + jnp.exp(-x))


def _sigmoid(x):
    return 1.0 / (1.0 + jnp.exp(-x))


def _rms(x, g):
    return x * lax.rsqrt(jnp.mean(x * x, axis=-1, keepdims=True) + EPS) * g


def _norm_mod(x, g, shift, scale):
    return _rms(x, g) * (1.0 + scale) + shift


def _params(sem, vmem=None):
    kw = dict(dimension_semantics=sem)
    if vmem is not None:
        kw["vmem_limit_bytes"] = vmem
    return pltpu.CompilerParams(**kw)


def _pick_block(pref, *sizes):
    b = pref
    while any(s % b for s in sizes):
        b //= 2
    return b


def _mod_kernel(s_ref, w_ref, b_ref, o_ref):
    s = _silu(s_ref[...])
    o_ref[0] = _mm(s.astype(BF16), w_ref[0].astype(BF16)) + b_ref[0]


def _modulation(c, c_ctx, w_mod, b_mod):
    L, D, N = w_mod.shape
    B = c.shape[0]
    rows = jnp.zeros((8, D), F32).at[:B].set(c).at[B].set(c_ctx)
    bn = TILE["adaln_cols"]
    out = pl.pallas_call(
        _mod_kernel,
        grid=(L, N // bn),
        in_specs=[pl.BlockSpec((8, D), lambda l, j: (0, 0)),
                  pl.BlockSpec((1, D, bn), lambda l, j: (l, 0, j)),
                  pl.BlockSpec((1, 1, bn), lambda l, j: (l, 0, j))],
        out_specs=pl.BlockSpec((1, 8, bn), lambda l, j: (l, 0, j)),
        out_shape=jax.ShapeDtypeStruct((L, 8, N), F32),
        compiler_params=_params(("parallel", "parallel"), VMEM_BIG),
        name="adaln_mod",
    )(rows, w_mod, b_mod.reshape(L, 1, N))
    return out.reshape(L, 8, N_MOD, D)


NORM_ROWS = 64


def _norm_mod_rows(x_ref, g, shift, scale, hx_ref):
    def body(r, carry):
        rs = pl.ds(pl.multiple_of(r * NORM_ROWS, NORM_ROWS), NORM_ROWS)
        hx_ref[rs, :] = _norm_mod(x_ref[rs, :], g, shift, scale).astype(BF16)
        return carry

    lax.fori_loop(0, x_ref.shape[0] // NORM_ROWS, body, 0)


def _row_source_specs(src, bm, D, n_latent_rows):
    nlat = n_latent_rows // bm
    off_b = src[2] // bm
    spec_a = pl.BlockSpec((bm, D), lambda i, j: (jnp.minimum(i, nlat - 1), 0))
    spec_b = pl.BlockSpec((bm, D), lambda i, j: (off_b + jnp.maximum(i - nlat, 0), 0), pipeline_mode=pl.Buffered(1))
    return nlat, spec_a, spec_b


def _for_row_source(nlat, xa_ref, xb_ref, fn):
    i = pl.program_id(0)

    @pl.when(i < nlat)
    def _():
        fn(xa_ref)

    @pl.when(i >= nlat)
    def _():
        fn(xb_ref)


W_RING = 3


def _inproj_kernel(xa_ref, xb_ref, mod_ref, g_ref, w_hbm, o_ref, hx_ref, wbuf_ref, sem, *, nlat, n_col, n_steps):
    bn = wbuf_ref.shape[1]
    s = pl.program_id(0) * n_col + pl.program_id(1)

    def tile_copy(step):
        slot = step % W_RING
        row0 = pl.multiple_of((step % n_col) * bn, bn)
        return pltpu.make_async_copy(w_hbm.at[pl.ds(row0, bn), :], wbuf_ref.at[slot], sem.at[slot])

    @pl.when(s == 0)
    def _():
        for first in range(W_RING - 1):
            tile_copy(first).start()

    @pl.when(s + W_RING - 1 < n_steps)
    def _():
        tile_copy(s + W_RING - 1).start()

    @pl.when(pl.program_id(1) == 0)
    def _():
        m = mod_ref[0]
        _for_row_source(nlat, xa_ref, xb_ref,
                        lambda x_ref: _norm_mod_rows(x_ref, g_ref[...], m[0:1], m[1:2], hx_ref))

    tile_copy(s).wait()
    o_ref[...] = _mm_nt(hx_ref[...], wbuf_ref[s % W_RING]).astype(BF16)


def _in_proj(src, mod_l, g1, wa, geom):
    R, D = geom["R"], src[0].shape[1]
    N = wa.shape[0]
    bm = _pick_block(TILE["inproj_rows"], geom["T"], geom["RC"])
    bn = TILE["inproj_cols"]
    sel = geom["mod_sel"](bm)
    nlat, spec_a, spec_b = _row_source_specs(src, bm, D, geom["RL"])
    n_row, n_col = R // bm, N // bn
    return pl.pallas_call(
        functools.partial(_inproj_kernel, nlat=nlat, n_col=n_col, n_steps=n_row * n_col),
        grid=(n_row, n_col),
        in_specs=[spec_a, spec_b,
                  pl.BlockSpec((1, N_MOD, D), lambda i, j: (sel(i), 0, 0)),
                  pl.BlockSpec((1, D), lambda i, j: (0, 0)),
                  pl.BlockSpec(memory_space=pl.ANY)],
        out_specs=pl.BlockSpec((bm, bn), lambda i, j: (i, j)),
        out_shape=jax.ShapeDtypeStruct((R, N), BF16),
        scratch_shapes=[pltpu.VMEM((bm, D), BF16), pltpu.VMEM((W_RING, bn, D), BF16),
                        pltpu.SemaphoreType.DMA((W_RING,))],
        compiler_params=_params(("arbitrary", "arbitrary"), VMEM_BIG),
        name="in_proj",
    )(src[0], src[1], mod_l, g1.reshape(1, D), wa)


def _prep_kernel(gq_ref, cq_ref, ckv_ref, gk_ref, gv_ref, misc_ref, gmq_ref, gmkv_ref, wqb_ref, wkvb_ref,
                 ggq_ref, ggk_ref, cosg_ref, sing_ref, cosm_ref, sinm_ref,
                 qm_ref, km_ref, vm_ref, qg_ref, kg_ref, vg_ref):
    lane = lax.broadcasted_iota(jnp.int32, (1, LANE), 1)
    cosm, sinm = cosm_ref[...], sinm_ref[...]
    cosg, sing = cosg_ref[...], sing_ref[...]
    first_m = (lane & 31) < 16
    first_g = (lane & 63) < 32
    ones_col = jnp.where(lane == 0, 1.0, 0.0).astype(BF16) + jnp.zeros((cq_ref.shape[0], LANE), BF16)

    def rope_m(x):
        partner = jnp.where(first_m, pltpu.roll(x, LANE - 16, 1), pltpu.roll(x, 16, 1))
        return x * cosm + partner * sinm

    def rope_g(x):
        partner = jnp.where(first_g, pltpu.roll(x, LANE - 32, 1), pltpu.roll(x, 32, 1))
        return x * cosg + partner * sing

    qa = _mm(_rms(cq_ref[...].astype(F32), gmq_ref[...]).astype(BF16), wqb_ref[...])
    kva = _mm(_rms(ckv_ref[...].astype(F32), gmkv_ref[...]).astype(BF16), wkvb_ref[...])
    kpe = rope_m(jnp.where(lane < MLA_ROPE, misc_ref[...].astype(F32), 0.0)).astype(BF16)
    nh = MLA_HEADS * LANE
    for h in range(MLA_HEADS):
        hs = slice(h * LANE, (h + 1) * LANE)
        ps = slice(nh + h * LANE, nh + (h + 1) * LANE)
        qm_ref[h, :, 0:LANE] = qa[:, hs].astype(BF16)
        qm_ref[h, :, LANE:2 * LANE] = rope_m(qa[:, ps]).astype(BF16)
        km_ref[h, :, 0:LANE] = kva[:, hs].astype(BF16)
        km_ref[h, :, LANE:2 * LANE] = kpe
        vm_ref[h, :, 0:LANE] = kva[:, ps].astype(BF16)
        vm_ref[h, :, LANE:2 * LANE] = ones_col

    ggq, ggk = ggq_ref[...], ggk_ref[...]
    for h in range(GQA_HEADS):
        hs = slice(h * LANE, (h + 1) * LANE)
        qg_ref[h] = rope_g(_rms(gq_ref[:, hs].astype(F32), ggq)).astype(BF16)
    for h in range(GQA_KV_HEADS):
        hs = slice(h * LANE, (h + 1) * LANE)
        kg_ref[h] = rope_g(_rms(gk_ref[:, hs].astype(F32), ggk)).astype(BF16)
        vg_ref[h, :, 0:LANE] = gv_ref[:, hs]
        vg_ref[h, :, LANE:2 * LANE] = ones_col


def _attn_prep(P, gmq, gmkv, wqb, wkvb, ggq, ggk, tabs, geom):
    R = P.shape[0]
    bm = _pick_block(TILE["prep_rows"], geom["T"], geom["RC"])
    nlat = geom["RL"] // bm
    per = geom["T"] // bm
    tsel = lambda i: jnp.where(i < nlat, i % per, per)
    col = lambda w, off: pl.BlockSpec((bm, w), lambda i: (i, off // w))
    full = lambda a: pl.BlockSpec(a.shape, lambda i: (0,) * a.ndim)
    tab = pl.BlockSpec((bm, LANE), lambda i: (tsel(i), 0))
    outs = [((MLA_HEADS, R, 2 * LANE), 2 * LANE), ((MLA_HEADS, R, 2 * LANE), 2 * LANE),
            ((MLA_HEADS, R, 2 * LANE), 2 * LANE), ((GQA_HEADS, R, LANE), LANE), ((GQA_KV_HEADS, R, LANE), LANE),
            ((GQA_KV_HEADS, R, 2 * LANE), 2 * LANE)]
    gmq, gmkv, ggq, ggk = (a.reshape(1, -1) for a in (gmq, gmkv, ggq, ggk))
    return pl.pallas_call(
        _prep_kernel,
        grid=(R // bm,),
        in_specs=[col(1024, OFF_GQ), col(512, OFF_CQ), col(512, OFF_CKV), col(256, OFF_GK), col(256, OFF_GV),
                  col(LANE, OFF_MISC), full(gmq), full(gmkv), full(wqb), full(wkvb), full(ggq), full(ggk),
                  tab, tab, tab, tab],
        out_specs=[pl.BlockSpec((s[0], bm, w), lambda i: (0, i, 0)) for s, w in outs],
        out_shape=[jax.ShapeDtypeStruct(s, BF16) for s, _ in outs],
        compiler_params=_params(("parallel",), VMEM_BIG),
        name="attn_prep",
    )(P, P, P, P, P, P, gmq, gmkv, wqb, wkvb, ggq, ggk, *tabs)


def _softmax_pv(q, kv_chunks, s_ref, o_ref, exp2_scale):
    sub = min(q.shape)
    for r0 in range(0, q.shape[0], sub):
        rows = slice(r0, r0 + sub)
        qr = q[rows]
        m_acc = None
        off = 0
        for k_ref, _, st, n in kv_chunks:
            s = _mm_nt(qr, k_ref[0, st:st + n, :])
            s_ref[rows, off:off + n] = s
            for t in range(n // LANE):
                tile = s[:, t * LANE:(t + 1) * LANE]
                m_acc = tile if m_acc is None else jnp.maximum(m_acc, tile)
            off += n
        m = jnp.max(m_acc, axis=-1, keepdims=True)
        acc = None
        off = 0
        for _, v_ref, st, n in kv_chunks:
            p = jnp.exp2((s_ref[rows, off:off + n] - m) * exp2_scale).astype(BF16)
            pv = _mm(p, v_ref[0, st:st + n, :])
            acc = pv if acc is None else acc + pv
            off += n
        o_ref[rows, :] = (acc[:, :LANE] / acc[:, LANE:LANE + 1]).astype(BF16)


def _attn_kernel(q_ref, kx_ref, kc_ref, vx_ref, vc_ref, o_ref, s_ref, *, exp2_scale, kchunk):
    chunks = [(kx_ref, vx_ref, i * kchunk, kchunk) for i in range(kx_ref.shape[1] // kchunk)]
    chunks.append((kc_ref, vc_ref, 0, kc_ref.shape[1]))
    _softmax_pv(q_ref[0], chunks, s_ref, o_ref, exp2_scale)


def _attn_ctx_kernel(q_ref, kc_ref, vc_ref, o_ref, s_ref, *, exp2_scale):
    _softmax_pv(q_ref[0], [(kc_ref, vc_ref, 0, kc_ref.shape[1])], s_ref, o_ref, exp2_scale)


def _attention(q, k, v, scale, geom, name):
    H, _, dk = q.shape
    Hk, _, dve = v.shape
    grp = H // Hk
    B, T, C, RL = geom["B"], geom["T"], geom["C"], geom["RL"]
    bq = _pick_block(TILE["attn_rows"], T)
    nq = T // bq
    cb = RL // C
    return pl.pallas_call(
        functools.partial(_attn_kernel, exp2_scale=scale * math.log2(math.e),
                          kchunk=TILE["attn_keys"]),
        grid=(B, H, nq),
        in_specs=[pl.BlockSpec((1, bq, dk), lambda b, h, j: (h, b * nq + j, 0)),
                  pl.BlockSpec((1, T, dk), lambda b, h, j: (h // grp, b, 0)),
                  pl.BlockSpec((1, C, dk), lambda b, h, j: (h // grp, cb + b, 0)),
                  pl.BlockSpec((1, T, dve), lambda b, h, j: (h // grp, b, 0)),
                  pl.BlockSpec((1, C, dve), lambda b, h, j: (h // grp, cb + b, 0))],
        out_specs=pl.BlockSpec((bq, LANE), lambda b, h, j: (b * nq + j, h)),
        out_shape=jax.ShapeDtypeStruct((RL, H * LANE), BF16),
        scratch_shapes=[pltpu.VMEM((bq, T + C), F32)],
        compiler_params=_params(("parallel", "parallel", "parallel"), VMEM_BIG),
        name=name,
    )(q, k, k, v, v)


def _attention_ctx(q, k, v, scale, geom, name):
    H, _, dk = q.shape
    Hk, _, dve = v.shape
    grp = H // Hk
    B, C, RL, RC = geom["B"], geom["C"], geom["RL"], geom["RC"]
    cb = RL // C
    return pl.pallas_call(
        functools.partial(_attn_ctx_kernel, exp2_scale=scale * math.log2(math.e)),
        grid=(B, H),
        in_specs=[pl.BlockSpec((1, C, dk), lambda b, h: (h, cb + b, 0)),
                  pl.BlockSpec((1, C, dk), lambda b, h: (h // grp, cb + b, 0)),
                  pl.BlockSpec((1, C, dve), lambda b, h: (h // grp, cb + b, 0))],
        out_specs=pl.BlockSpec((C, LANE), lambda b, h: (b, h)),
        out_shape=jax.ShapeDtypeStruct((RC, H * LANE), BF16),
        scratch_shapes=[pltpu.VMEM((C, C), F32)],
        compiler_params=_params(("parallel", "parallel")),
        name=name,
    )(q, k, v)


def _gdn_conv_gates(cur_ref, prev_ref, next_ref, misc_ref, wc_ref, gvec_ref, ext_ref, *, nlat, per_seq, nblk):
    r = jnp.minimum(pl.program_id(0), nblk - 1)
    pos = r % per_seq
    is_ctx = r >= nlat
    pf = jnp.where(jnp.logical_or(is_ctx, pos == 0), 0.0, 1.0).astype(F32)
    nf = jnp.where(jnp.logical_or(is_ctx, pos == per_seq - 1), 0.0, 1.0).astype(F32)
    nb = GDN_BLOCK
    hal = HALO_ROWS
    ext_ref[0:hal, :] = prev_ref[...].astype(F32) * pf
    ext_ref[hal:hal + nb, :] = cur_ref[...].astype(F32)
    ext_ref[hal + nb:2 * hal + nb, :] = next_ref[...].astype(F32) * nf
    pad = GDN_CONV // 2
    n_qk = 2 * GDN_HEADS
    heads = []
    for c in range(3 * GDN_HEADS):
        cs = slice(c * LANE, (c + 1) * LANE)
        acc = None
        for j in range(GDN_CONV):
            t = ext_ref[hal - pad + j:hal - pad + j + nb, cs] * wc_ref[j:j + 1, cs]
            acc = t if acc is None else acc + t
        y = _silu(acc)
        if c < n_qk:
            y = y * lax.rsqrt(jnp.sum(y * y, axis=-1, keepdims=True) + EPS)
        heads.append(y)

    lane = lax.broadcasted_iota(jnp.int32, (1, LANE), 1)
    raw = misc_ref[...].astype(F32)
    beta = _sigmoid(raw)
    z = raw + gvec_ref[1:2, :]
    softplus = jnp.maximum(z, 0.0) + jnp.log1p(jnp.exp(-jnp.abs(z)))
    g = -jnp.exp(gvec_ref[0:1, :]) * softplus
    is_beta = jnp.logical_and(lane >= MISC_BETA, lane < MISC_DEC)
    is_g = jnp.logical_and(lane >= MISC_DEC, lane < MISC_DEC + 2 * GDN_HEADS)
    return heads, jnp.where(is_beta, beta, jnp.where(is_g, g, 0.0))


def _split3(x):
    h = x.astype(BF16)
    r = x - h.astype(F32)
    m = r.astype(BF16)
    l = (r - m.astype(F32)).astype(BF16)
    return h, m, l


def _gdn_chunk_local(heads, gt, u_ref, w_ref, qg_ref, kgt_ref, qk_ref):
    nb = GDN_BLOCK
    nh = GDN_HEADS
    ri = lax.broadcasted_iota(jnp.int32, (nb, nb), 0)
    ci = lax.broadcasted_iota(jnp.int32, (nb, nb), 1)
    same = (ri >> 6) == (ci >> 6)
    low = jnp.logical_and(same, ri >= ci)
    upp = jnp.logical_and(same, ri <= ci)
    slow = jnp.logical_and(same, ri > ci)
    supp = jnp.logical_and(same, ri < ci)
    ltri = jnp.where(low, 1.0, 0.0).astype(BF16)
    utri = jnp.where(upp, 1.0, 0.0).astype(BF16)
    eye = jnp.where(ri == ci, 1.0, 0.0)
    pair = (ri >> 1) == (ci >> 1)
    offs = [jnp.logical_and((ri >> (lv + 1)) == (ci >> (lv + 1)), (ri >> lv) != (ci >> lv))
            for lv in range(1, int(math.log2(GDN_CHUNK)))]

    g3 = _split3(gt)
    g3t = _split3(gt.T)
    cum_c = (sum(_mm(ltri, p) for p in reversed(g3)), sum(_mm(utri, p) for p in reversed(g3)))
    cum_r = (sum(_mm(p, utri) for p in reversed(g3t)), sum(_mm(p, ltri) for p in reversed(g3t)))
    tot_c = cum_c[0] + cum_c[1] - gt

    scale = GDN_DK ** -0.5
    insts = [(d, h) for d in range(2) for h in range(nh)]
    a_all, tinv_all, rhs_all = [], [], []

    def prep(d, h):
        mask, smask = (low, slow) if d == 0 else (upp, supp)
        ib = MISC_BETA + d * nh + h
        ig = MISC_DEC + d * nh + h
        beta = gt[:, ib:ib + 1]
        gc = cum_c[d][:, ig:ig + 1]
        gr = cum_r[d][ig:ig + 1, :]
        gl = tot_c[:, ig:ig + 1]
        q, k, v = heads[h], heads[nh + h], heads[2 * nh + h]
        decay = jnp.where(mask, jnp.exp(jnp.where(mask, gc - gr, 0.0)), 0.0)
        kb = k * beta
        k16 = k.astype(BF16)
        a = jnp.where(smask, _mm_nt(kb.astype(BF16), k16) * decay, 0.0)
        a_all.append(a.astype(BF16))
        tinv_all.append(eye - jnp.where(pair, a, 0.0))
        rhs_all.append(jnp.concatenate([v * beta, kb * jnp.exp(gc)], axis=1).astype(BF16))
        hs = slice(h * LANE, (h + 1) * LANE)
        qs = q * scale
        qk_ref[d, :, h * nb:(h + 1) * nb] = (_mm_nt(qs.astype(BF16), k16) * decay).astype(BF16)
        qg_ref[d, :, hs] = (qs * jnp.exp(gc)).astype(BF16)
        kgt_ref[d, hs, :] = (k * jnp.exp(gl - gc)).T.astype(BF16)

    zero16 = jnp.zeros((nb, nb), BF16)

    def level(i, off):
        t16 = tinv_all[i].astype(BF16)
        ta = _mm(t16, jnp.where(off, a_all[i], zero16))
        tinv_all[i] = tinv_all[i] - _mm(ta.astype(BF16), t16)

    def finish(i):
        d, h = insts[i]
        x = _mm(tinv_all[i].astype(BF16), rhs_all[i])
        hs = slice(h * LANE, (h + 1) * LANE)
        u_ref[d, :, hs] = x[:, :LANE]
        w_ref[d, :, hs] = x[:, LANE:].astype(BF16)

    for d, h in insts:
        prep(d, h)
    for off in offs:
        for i in range(len(insts)):
            level(i, off)
    for i in range(len(insts)):
        finish(i)


def _gdn_local_kernel(cur_ref, prev_ref, next_ref, misc_ref, wc_ref, gvec_ref,
                      gates_ref, u_ref, w_ref, qg_ref, kgt_ref, qk_ref, ext_ref, *, nlat, per_seq, nblk):
    heads, gt = _gdn_conv_gates(cur_ref, prev_ref, next_ref, misc_ref, wc_ref, gvec_ref, ext_ref,
                                nlat=nlat, per_seq=per_seq, nblk=nblk)
    gates_ref[...] = gt
    _gdn_chunk_local(heads, gt, u_ref, w_ref, qg_ref, kgt_ref, qk_ref)


def _gdn_local(P, w_conv, a_log, dt_bias, geom):
    R = P.shape[0]
    nb = GDN_BLOCK
    HW = GDN_HEADS * LANE
    W = 3 * HW
    hal = HALO_ROWS
    sub = nb // hal
    gvec = jnp.zeros((2, LANE), F32)
    gvec = gvec.at[0, MISC_DEC:MISC_DEC + 2 * GDN_HEADS].set(a_log.reshape(-1))
    gvec = gvec.at[1, MISC_DEC:MISC_DEC + 2 * GDN_HEADS].set(dt_bias.reshape(-1))
    qc = OFF_QKV // W
    nblk = R // nb
    cb = ob = lambda t: t
    return pl.pallas_call(
        functools.partial(_gdn_local_kernel, nlat=geom["RL"] // nb, per_seq=geom["T"] // nb, nblk=nblk),
        grid=(nblk,),
        in_specs=[pl.BlockSpec((nb, W), lambda t: (cb(t), qc)),
                  pl.BlockSpec((hal, W), lambda t: (jnp.maximum(cb(t) * sub - 1, 0), qc)),
                  pl.BlockSpec((hal, W), lambda t: (jnp.minimum((cb(t) + 1) * sub, R // hal - 1), qc)),
                  pl.BlockSpec((nb, LANE), lambda t: (cb(t), OFF_MISC // LANE)),
                  pl.BlockSpec((GDN_CONV, W), lambda t: (0, 0)),
                  pl.BlockSpec((2, LANE), lambda t: (0, 0))],
        out_specs=[pl.BlockSpec((nb, LANE), lambda t: (ob(t), 0)),
                   pl.BlockSpec((2, nb, HW), lambda t: (0, ob(t), 0)),
                   pl.BlockSpec((2, nb, HW), lambda t: (0, ob(t), 0)),
                   pl.BlockSpec((2, nb, HW), lambda t: (0, ob(t), 0)),
                   pl.BlockSpec((2, HW, nb), lambda t: (0, 0, ob(t))),
                   pl.BlockSpec((2, nb, GDN_HEADS * nb), lambda t: (0, ob(t), 0))],
        out_shape=[jax.ShapeDtypeStruct((R, LANE), F32),
                   jax.ShapeDtypeStruct((2, R, HW), F32),
                   jax.ShapeDtypeStruct((2, R, HW), BF16),
                   jax.ShapeDtypeStruct((2, R, HW), BF16),
                   jax.ShapeDtypeStruct((2, HW, R), BF16),
                   jax.ShapeDtypeStruct((2, R, GDN_HEADS * nb), BF16)],
        scratch_shapes=[pltpu.VMEM((nb + 2 * hal, W), F32)],
        compiler_params=_params(("parallel",), VMEM_BIG),
        name="gdn_local",
    )(P, P, P, P, w_conv, gvec)


def _gdn_c_kernel(*refs):
    nb = GDN_BLOCK
    nh = GDN_HEADS
    ins, (of_ref, ob_ref, s_ref, vn_ref) = (refs[0:6], refs[6:12]), refs[12:]
    o_refs = (of_ref, ob_ref)

    @pl.when(pl.program_id(1) == 0)
    def _():
        s_ref[...] = jnp.zeros_like(s_ref)
        vn_ref[...] = jnp.zeros_like(vn_ref)

    rowi = lax.broadcasted_iota(jnp.int32, (nb, 1), 0)
    nchunk = nb // GDN_CHUNK
    for step in range(nchunk):
        for d in range(2):
            u_ref, w_ref, qg_ref, kgt_ref, qk_ref, gates_ref = ins[d]
            c = step if d == 0 else nchunk - 1 - step
            rs = slice(c * GDN_CHUNK, (c + 1) * GDN_CHUNK)
            decay_all = jnp.exp(jnp.sum(gates_ref[rs, :], axis=0, keepdims=True))
            cmask = jnp.logical_and(rowi >= c * GDN_CHUNK, rowi < (c + 1) * GDN_CHUNK)
            for h in range(nh):
                hs = slice(h * LANE, (h + 1) * LANE)
                ig = MISC_DEC + d * nh + h
                s = s_ref[d, h]
                s16 = s.astype(BF16)
                v_new = u_ref[0, rs, hs] - _mm(w_ref[0, rs, hs], s16)
                vn_ref[d, h, rs, :] = v_new.astype(BF16)
                vn = vn_ref[d, h]
                o_refs[d][rs, hs] = _mm(qg_ref[0, rs, hs], s16) + _mm(qk_ref[0, rs, h * nb:(h + 1) * nb], vn)
                v_cur = jnp.where(cmask, vn, jnp.zeros_like(vn))
                s_ref[d, h] = s * decay_all[:, ig:ig + 1] + _mm(kgt_ref[0, hs, :], v_cur)


def _gdn_scan(u, w, qg, kgt, qk, gates, geom):
    nb = GDN_BLOCK
    HW = GDN_HEADS * LANE
    B, RL, R = geom["B"], geom["RL"], geom["R"]
    per = geom["T"] // nb
    nlat = RL // nb

    def blk(d):
        return lambda b, j: jnp.where(j == 0, nlat + b, b * per + (j - 1 if d == 0 else per - j))

    def dir_specs(d):
        bk = blk(d)
        tok = lambda w_: pl.BlockSpec((1, nb, w_), lambda b, j: (d, bk(b, j), 0))
        return [tok(HW), tok(HW), tok(HW),
                pl.BlockSpec((1, HW, nb), lambda b, j: (d, 0, bk(b, j))),
                tok(GDN_HEADS * nb),
                pl.BlockSpec((nb, LANE), lambda b, j: (bk(b, j), 0))]

    args = (u, w, qg, kgt, qk, gates)
    return pl.pallas_call(
        _gdn_c_kernel,
        grid=(B, per + 1),
        in_specs=dir_specs(0) + dir_specs(1),
        out_specs=[pl.BlockSpec((nb, HW), lambda b, j: (blk(0)(b, j), 0)),
                   pl.BlockSpec((nb, HW), lambda b, j: (blk(1)(b, j), 0))],
        out_shape=[jax.ShapeDtypeStruct((R, HW), F32), jax.ShapeDtypeStruct((R, HW), F32)],
        scratch_shapes=[pltpu.VMEM((2, GDN_HEADS, GDN_DK, GDN_DV), F32),
                        pltpu.VMEM((2, GDN_HEADS, nb, GDN_DV), BF16)],
        compiler_params=_params(("parallel", "arbitrary")),
        name="gdn_scan",
    )(*args, *args)


def _gdn_out_kernel(of_ref, ob_ref, og_ref, g_ref, o_ref):
    o = of_ref[...] + ob_ref[...]
    gate = _silu(og_ref[...].astype(F32))
    g = g_ref[...]
    for h in range(GDN_HEADS):
        hs = slice(h * LANE, (h + 1) * LANE)
        o_ref[:, hs] = (_rms(o[:, hs], g) * gate[:, hs]).astype(BF16)


def _gdn_out(o_f, o_b, P, g_out, rows):
    HW = GDN_HEADS * LANE
    bm = _pick_block(TILE["gdn_out_rows"], rows)
    return pl.pallas_call(
        _gdn_out_kernel,
        grid=(rows // bm,),
        in_specs=[pl.BlockSpec((bm, HW), lambda i: (i, 0)),
                  pl.BlockSpec((bm, HW), lambda i: (i, 0)),
                  pl.BlockSpec((bm, HW), lambda i: (i, OFF_OG // HW)),
                  pl.BlockSpec((1, LANE), lambda i: (0, 0))],
        out_specs=pl.BlockSpec((bm, HW), lambda i: (i, 0)),
        out_shape=jax.ShapeDtypeStruct((rows, HW), BF16),
        compiler_params=_params(("parallel",)),
        name="gdn_out",
    )(o_f, o_b, P, g_out.reshape(1, LANE))


def _residual_rows(x_ref, gate, o_ref, post=None):
    def body(r, carry):
        rs = pl.ds(pl.multiple_of(r * NORM_ROWS, NORM_ROWS), NORM_ROWS)
        v = x_ref[rs, :] + gate * o_ref[rs, :]
        o_ref[rs, :] = v if post is None else post(v)
        return carry

    lax.fori_loop(0, x_ref.shape[0] // NORM_ROWS, body, 0)


def _merge_kernel(xa_ref, xb_ref, mod_ref, g_ref, oal_ref, oac_ref, obl_ref, obc_ref, oc_ref, wga_ref, wgb_ref,
                  wgc_ref, wua_ref, wub_ref, wuc_ref, wo_ref, o_ref, hx_ref, oa_ref, ob_ref, *, nlat):
    j = pl.program_id(1)
    m = mod_ref[0]

    @pl.when(j == 0)
    def _():
        _for_row_source(nlat, xa_ref, xb_ref,
                        lambda x_ref: _norm_mod_rows(x_ref, g_ref[...], m[0:1], m[1:2], hx_ref))

        def stage(dst_ref):
            return lambda src_ref: dst_ref.__setitem__(Ellipsis, src_ref[...])

        _for_row_source(nlat, oal_ref, oac_ref, stage(oa_ref))
        _for_row_source(nlat, obl_ref, obc_ref, stage(ob_ref))
        o_ref[...] = jnp.zeros_like(o_ref)

    hx = hx_ref[...]
    oa, ob, oc = oa_ref[...], ob_ref[...], oc_ref[...]
    hc = wo_ref.shape[0]
    ga, gb, gc = _mm_nt(hx, wga_ref[...]), _mm_nt(hx, wgb_ref[...]), _mm_nt(hx, wgc_ref[...])
    ua, ub, uc = _mm(oa, wua_ref[...]), _mm(ob, wub_ref[...]), _mm(oc, wuc_ref[...])
    acc = None
    for cs in (slice(0, hc // 2), slice(hc // 2, hc)):
        y = _sigmoid(ga[:, cs]) * ua[:, cs] + _sigmoid(gb[:, cs]) * ub[:, cs] + _sigmoid(gc[:, cs]) * uc[:, cs]
        t = _mm(y.astype(BF16), wo_ref[cs, :])
        acc = t if acc is None else acc + t
    o_ref[...] += acc

    @pl.when(j == pl.num_programs(1) - 1)
    def _():
        _for_row_source(nlat, xa_ref, xb_ref, lambda x_ref: _residual_rows(x_ref, m[2:3], o_ref))


def _merge(src, mod_l, g1, oa, ob, oc, wg, wua, wub, wuc, wo, geom, rows):
    D = src[0].shape[1]
    bm = _pick_block(TILE["merge_rows"], geom["T"], geom["RC"])
    hc = TILE["merge_cols"]
    nj = D // hc
    sel = geom["mod_sel"](bm)
    rowb = lambda w_: pl.BlockSpec((bm, w_), lambda i, j: (i, 0))
    gate_w = lambda br: pl.BlockSpec((hc, D), lambda i, j: (br * nj + j, 0))
    nlat, spec_a, spec_b = _row_source_specs(src, bm, D, geom["RL"])
    lat = lambda w_: pl.BlockSpec((bm, w_), lambda i, j: (jnp.minimum(i, nlat - 1), 0))
    ctx = lambda w_: pl.BlockSpec((bm, w_), lambda i, j: (jnp.maximum(i - nlat, 0), 0))
    wa_, wb_ = oa[0].shape[1], ob[0].shape[1]
    return pl.pallas_call(
        functools.partial(_merge_kernel, nlat=nlat),
        grid=(rows // bm, nj),
        in_specs=[spec_a, spec_b,
                  pl.BlockSpec((1, N_MOD, D), lambda i, j: (sel(i), 0, 0)),
                  pl.BlockSpec((1, D), lambda i, j: (0, 0)),
                  lat(wa_), ctx(wa_), lat(wb_), ctx(wb_), rowb(oc.shape[1]),
                  gate_w(0), gate_w(1), gate_w(2),
                  pl.BlockSpec((wua.shape[0], hc), lambda i, j: (0, j)),
                  pl.BlockSpec((wub.shape[0], hc), lambda i, j: (0, j)),
                  pl.BlockSpec((wuc.shape[0], hc), lambda i, j: (0, j)),
                  pl.BlockSpec((hc, D), lambda i, j: (j, 0))],
        out_specs=rowb(D),
        out_shape=jax.ShapeDtypeStruct((rows, D), F32),
        scratch_shapes=[pltpu.VMEM((bm, D), BF16), pltpu.VMEM((bm, wa_), BF16), pltpu.VMEM((bm, wb_), BF16)],
        compiler_params=_params(("parallel", "arbitrary"), VMEM_BIG),
        name="merge",
    )(src[0], src[1], mod_l, g1.reshape(1, D), oa[0], oa[1], ob[0], ob[1], oc, wg, wg, wg, wua, wub, wuc, wo)


def _mlp_kernel(x_ref, mod_ref, g_ref, w1_ref, w2_ref, gf_ref, o_ref, hx_ref, *, final):
    j = pl.program_id(1)
    m = mod_ref[0]

    @pl.when(j == 0)
    def _():
        _norm_mod_rows(x_ref, g_ref[...], m[3:4], m[4:5], hx_ref)
        o_ref[...] = jnp.zeros_like(o_ref)

    hx = hx_ref[...]
    fc = w2_ref.shape[1]
    h1 = _mm(hx, w1_ref[0].astype(BF16))
    acc = None
    for cs in (slice(0, fc // 2), slice(fc // 2, fc)):
        h = jnp.maximum(h1[:, cs], 0.0)
        t = _mm((h * h).astype(BF16), w2_ref[0, cs, :].astype(BF16))
        acc = t if acc is None else acc + t
    o_ref[...] += acc

    @pl.when(j == pl.num_programs(1) - 1)
    def _():
        gf = gf_ref[...]
        _residual_rows(x_ref, m[5:6], o_ref, (lambda v: _rms(v, gf)) if final else None)


def _mlp(x1, mod_l, g2, w1, w2, l, g_final, geom, final):
    rows, D = x1.shape
    F = w1.shape[2]
    bm = _pick_block(TILE["mlp_rows"], geom["T"], geom["RC"])
    fc = TILE["mlp_cols"]
    sel = geom["mod_sel"](bm)
    return pl.pallas_call(
        functools.partial(_mlp_kernel, final=final),
        grid=(rows // bm, F // fc),
        in_specs=[pl.BlockSpec((bm, D), lambda i, j: (i, 0), pipeline_mode=pl.Buffered(1)),
                  pl.BlockSpec((1, N_MOD, D), lambda i, j: (sel(i), 0, 0)),
                  pl.BlockSpec((1, D), lambda i, j: (0, 0)),
                  pl.BlockSpec((1, D, fc), lambda i, j: (l, 0, j)),
                  pl.BlockSpec((1, fc, D), lambda i, j: (l, j, 0)),
                  pl.BlockSpec((1, D), lambda i, j: (0, 0))],
        out_specs=pl.BlockSpec((bm, D), lambda i, j: (i, 0)),
        out_shape=jax.ShapeDtypeStruct((rows, D), F32),
        scratch_shapes=[pltpu.VMEM((bm, D), BF16)],
        compiler_params=_params(("parallel", "arbitrary"), VMEM_BIG),
        name="mlp",
    )(x1, mod_l, g2.reshape(1, D), w1, w2, g_final.reshape(1, D))


def _rope_tables(T, bm):
    t = np.arange(T)
    row, col = t // GRID_W, t % GRID_W

    def table(n_rot):
        n = n_rot // 2
        half = n // 2
        inv = ROPE_THETA ** (-np.arange(half, dtype=np.float64) / half)
        cos = np.zeros((T + bm, LANE))
        sin = np.zeros((T + bm, LANE))
        for s, pos in enumerate((row, col)):
            ang = pos[:, None].astype(np.float32).astype(np.float64) * inv.astype(np.float32)[None, :]
            ang = ang.astype(np.float32).astype(np.float64)
            c, sn = np.cos(ang), np.sin(ang)
            cos[:T, s * n:s * n + half] = c
            cos[:T, s * n + half:(s + 1) * n] = c
            sin[:T, s * n:s * n + half] = -sn
            sin[:T, s * n + half:(s + 1) * n] = sn
        cos[T:, :] = 1.0
        return jnp.asarray(cos, F32), jnp.asarray(sin, F32)

    cg, sg = table(GQA_HD)
    cm, sm = table(MLA_ROPE)
    return cg, sg, cm, sm


def _relayout_kernel(w_ref, wa_ref, wg_ref, *, src_off):
    for name, dst, width in (("gq", OFF_GQ, 1024), ("cq", OFF_CQ, 512), ("ckv", OFF_CKV, 512), ("gk", OFF_GK, 256),
                             ("gv", OFF_GV, 256), ("og", OFF_OG, 512), ("qkv", OFF_QKV, 1536),
                             ("kpe", OFF_MISC, MLA_ROPE), ("beta", OFF_MISC + MISC_BETA, 4 * GDN_HEADS)):
        s = src_off[name]
        wa_ref[dst:dst + width, :] = w_ref[0, s:s + width, :].astype(BF16)
    assert src_off["dec"] == src_off["beta"] + 2 * GDN_HEADS
    used = OFF_MISC + MISC_DEC + 2 * GDN_HEADS
    wa_ref[used:, :] = jnp.zeros((N_SMALL - used, wa_ref.shape[1]), BF16)
    s = src_off["bg"]
    wg_ref[...] = w_ref[0, s:s + wg_ref.shape[0], :].astype(BF16)


def _relayout_w_in(w_in, l):
    w_t = jnp.swapaxes(w_in, 1, 2)
    L, N, D = w_t.shape
    splits = (("cq", MLA_Q_RANK), ("ckv", MLA_KV_RANK), ("kpe", MLA_ROPE), ("gq", GQA_HEADS * GQA_HD),
              ("gk", GQA_KV_HEADS * GQA_HD), ("gv", GQA_KV_HEADS * GQA_HD),
              ("qkv", GDN_HEADS * (2 * GDN_DK + GDN_DV)), ("beta", 2 * GDN_HEADS), ("dec", 2 * GDN_HEADS),
              ("og", GDN_HEADS * GDN_DV), ("bg", 3 * D))
    src_off, o = {}, 0
    for name, n in splits:
        src_off[name] = o
        o += n
    assert o == N
    kc = TILE["relayout_k"]
    return pl.pallas_call(
        functools.partial(_relayout_kernel, src_off=src_off),
        grid=(D // kc,),
        in_specs=[pl.BlockSpec((1, N, kc), lambda i: (l, 0, i))],
        out_specs=[pl.BlockSpec((N_SMALL, kc), lambda i: (0, i)),
                   pl.BlockSpec((3 * D, kc), lambda i: (0, i))],
        out_shape=[jax.ShapeDtypeStruct((N_SMALL, D), BF16), jax.ShapeDtypeStruct((3 * D, D), BF16)],
        compiler_params=_params(("parallel",), VMEM_BIG),
        name="w_in_relayout",
    )(w_t)


def _prep_layer_weights(w_qb, w_kvb):
    qb = w_qb.reshape(-1, MLA_HEADS, MLA_NOPE + MLA_ROPE)
    qpe = jnp.pad(qb[:, :, MLA_NOPE:], ((0, 0), (0, 0), (0, LANE - MLA_ROPE)))
    wqb = jnp.concatenate([qb[:, :, :MLA_NOPE].reshape(-1, MLA_HEADS * LANE),
                           qpe.reshape(-1, MLA_HEADS * LANE)], axis=1).astype(BF16)
    kvb = w_kvb.reshape(-1, MLA_HEADS, MLA_NOPE + MLA_V)
    wkvb = jnp.concatenate([kvb[:, :, :MLA_NOPE].reshape(-1, MLA_HEADS * LANE),
                            kvb[:, :, MLA_NOPE:].reshape(-1, MLA_HEADS * LANE)], axis=1).astype(BF16)
    return wqb, wkvb


def kernel(x, c, ctx, c_ctx, w_mod, b_mod, g_norm1, w_in, g_mla_q, w_mla_qb, g_mla_kv, w_mla_kvb, g_gqa_q, g_gqa_k,
           w_conv, a_log, dt_bias, g_gdn_out, w_up_a, w_up_b, w_up_c, w_out, g_norm2, w_ff1, w_ff2, g_final):
    B, T, D = x.shape
    C = ctx.shape[1]
    L = w_mod.shape[0]
    assert C == GDN_BLOCK and T % GDN_BLOCK == 0 and T % GRID_W == 0 and B < 8
    RL, RC = B * T, B * C
    geom = dict(B=B, T=T, C=C, RL=RL, RC=RC, R=RL + RC)
    geom["mod_sel"] = lambda bm: (lambda i: jnp.where(i < RL // bm, i // (T // bm), B))

    mod = _modulation(c, c_ctx, w_mod, b_mod)
    tabs = _rope_tables(T, _pick_block(TILE["prep_rows"], T, RC))
    src = (x.reshape(RL, D), ctx.reshape(RC, D), 0)
    mla_scale = (MLA_NOPE + MLA_ROPE) ** -0.5
    gqa_scale = GQA_HD ** -0.5

    for l in range(L):
        last = l == L - 1
        wa, wg = _relayout_w_in(w_in, l)
        wqb, wkvb = _prep_layer_weights(w_mla_qb[l], w_mla_kvb[l])
        P = _in_proj(src, mod[l], g_norm1[l], wa, geom)
        qm, km, vm, qg, kg, vg = _attn_prep(P, g_mla_q[l], g_mla_kv[l], wqb, wkvb, g_gqa_q[l], g_gqa_k[l], tabs, geom)
        rows = RL if last else RL + RC
        oa = _attention(qm, km, vm, mla_scale, geom, "attn_mla")
        ob = _attention(qg, kg, vg, gqa_scale, geom, "attn_gqa")
        gates, u, w, qgd, kgt, qk = _gdn_local(P, w_conv[l], a_log[l], dt_bias[l], geom)
        o_f, o_b = _gdn_scan(u, w, qgd, kgt, qk, gates, geom)
        oc = _gdn_out(o_f, o_b, P, g_gdn_out[l], rows)
        if last:
            oa, ob = (oa, oa), (ob, ob)
        else:
            oa = (oa, _attention_ctx(qm, km, vm, mla_scale, geom, "attn_mla_ctx"))
            ob = (ob, _attention_ctx(qg, kg, vg, gqa_scale, geom, "attn_gqa_ctx"))
        x1 = _merge(src, mod[l], g_norm1[l], oa, ob, oc, wg, w_up_a[l].astype(BF16), w_up_b[l].astype(BF16),
                    w_up_c[l].astype(BF16), w_out[l].astype(BF16), geom, rows)
        xs = _mlp(x1, mod[l], g_norm2[l], w_ff1, w_ff2, l, g_final, geom, last)
        src = (xs, xs, RL)
    return xs.reshape(B, T, D)
```
